```python
import math
import jax, jax.numpy as jnp
from jax import lax
import numpy as np

D_MODEL = 1024
BATCH = 2
SEQ = 8192
DEPTH = 1

D_MIX = D_MODEL
N_DIFF_HEADS = 4
DIFF_HEAD_DIM = 64
DIFF_V_DIM = 2 * DIFF_HEAD_DIM
ATTN_WIDTH = N_DIFF_HEADS * DIFF_V_DIM
QK_WIDTH = N_DIFF_HEADS * 2 * DIFF_HEAD_DIM
POOL_WINDOWS = (2, 4, 8, 16)
POOL_GROUPS = len(POOL_WINDOWS)
POOL_WIDTH = D_MIX - ATTN_WIDTH
POOL_GROUP_DIM = POOL_WIDTH // POOL_GROUPS
IN_COLS = 2 * QK_WIDTH + ATTN_WIDTH + POOL_WIDTH
Q_BLOCK = 128
N_GROUPS = 4
E_PER_GROUP = 8
N_EXPERTS = N_GROUPS * E_PER_GROUP
TOP_K = 2
D_EXPERT = D_MODEL // 2
ROW_BLOCK = 128
EPS = 1e-6

kernel_name = "hybrid_diffattn_pool_hmoe_encoder"


def rmsnorm(x, g):
    xf = x.astype(jnp.float32)
    y = xf * lax.rsqrt(jnp.mean(xf * xf, axis=-1, keepdims=True) + EPS)
    return (y * g.astype(jnp.float32)).astype(x.dtype)


def diff_attention(q, k, v, lam, lam_init, subln_g):
    B, S = q.shape[:2]
    H, Dh = N_DIFF_HEADS, DIFF_HEAD_DIM
    nqb = S // Q_BLOCK
    scale = 1.0 / math.sqrt(Dh)
    slopes = 2.0 ** (-8.0 * jnp.arange(1, H + 1, dtype=jnp.float32) / H)
    kpos = jnp.arange(S, dtype=jnp.int32)
    q_blocks = q.reshape(B, nqb, Q_BLOCK, H, 2, Dh).transpose(1, 0, 2, 3, 4, 5)

    def block(args):
        q_blk, i = args
        s = jnp.einsum('bqhcd,bkhcd->bhcqk', q_blk, k,
                       preferred_element_type=jnp.float32) * scale
        qpos = i * Q_BLOCK + jnp.arange(Q_BLOCK, dtype=jnp.int32)
        dist = jnp.abs(qpos[:, None] - kpos[None, :]).astype(jnp.float32)
        s = s - slopes[None, :, None, None, None] * dist
        p = jax.nn.softmax(s, axis=-1)
        a = p[:, :, 0] - lam * p[:, :, 1]
        return jnp.einsum('bhqk,bkhe->bqhe', a.astype(v.dtype), v)

    o = lax.map(block, (q_blocks, jnp.arange(nqb, dtype=jnp.int32)))
    o = o.transpose(1, 0, 2, 3, 4).reshape(B, S, H, DIFF_V_DIM)
    o = rmsnorm(o, subln_g) * (1.0 - lam_init)
    return o.reshape(B, S, ATTN_WIDTH)


def multiscale_pool(u, w_pool, pool_scale):
    B, S, _ = u.shape
    uf = u.astype(jnp.float32).reshape(B, S, POOL_GROUPS, POOL_GROUP_DIM)
    csum = jnp.concatenate([jnp.zeros((B, 1, POOL_GROUPS, POOL_GROUP_DIM), jnp.float32),
                            jnp.cumsum(uf, axis=1)], axis=1)
    t = jnp.arange(S, dtype=jnp.int32)
    means = []
    for g, w in enumerate(POOL_WINDOWS):
        lo = jnp.clip(t - w // 2, 0, S)
        hi = jnp.clip(t + w // 2, 0, S)
        cnt = (hi - lo).astype(jnp.float32)
        sg = csum[:, hi, g] - csum[:, lo, g]
        means.append(sg / cnt[None, :, None])
    pooled = jnp.stack(means, axis=2)
    d = (pooled - uf).astype(u.dtype)
    y = jnp.einsum('bsgc,gcd->bsgd', d, w_pool).reshape(B, S, POOL_WIDTH)
    return y * pool_scale


def hierarchical_moe(xn, w_gr, b_gr, w_er, b_er, w_gate, w_up, w_down):
    B, S, D = xn.shape
    T = B * S
    xt = xn.reshape(T, D)
    g_logits = (xt @ w_gr).astype(jnp.float32) + b_gr.astype(jnp.float32)
    p_group = jax.nn.softmax(g_logits, axis=-1)
    gsel = jnp.argmax(g_logits, axis=-1).astype(jnp.int32)
    pg = jnp.take_along_axis(p_group, gsel[:, None], axis=-1)
    e_logits = (xt @ w_er.reshape(D, N_EXPERTS)).astype(jnp.float32).reshape(T, N_GROUPS, E_PER_GROUP)
    e_logits = e_logits + b_er.astype(jnp.float32)
    e_logits = jnp.take_along_axis(e_logits, gsel[:, None, None], axis=1)[:, 0]
    p_e = jax.nn.softmax(e_logits, axis=-1)
    top_p, top_i = lax.top_k(p_e, TOP_K)
    top_p = top_p / jnp.sum(top_p, axis=-1, keepdims=True)
    gates = pg * top_p
    expert_ids = gsel[:, None] * E_PER_GROUP + top_i.astype(jnp.int32)

    A = T * TOP_K
    nb = (A + ROW_BLOCK - 1) // ROW_BLOCK + N_EXPERTS
    P = nb * ROW_BLOCK
    flat_e = expert_ids.reshape(-1)
    flat_tok = jnp.repeat(jnp.arange(T, dtype=jnp.int32), TOP_K)
    flat_g = gates.reshape(-1)
    order = jnp.argsort(flat_e)
    se = flat_e[order]
    counts = jnp.bincount(flat_e, length=N_EXPERTS)
    starts = jnp.cumsum(counts) - counts
    padded = (counts + ROW_BLOCK - 1) // ROW_BLOCK * ROW_BLOCK
    pad_ends = jnp.cumsum(padded)
    pad_starts = pad_ends - padded
    dest = pad_starts[se] + (jnp.arange(A, dtype=jnp.int32) - starts[se])
    buf_tok = jnp.full((P,), T, jnp.int32).at[dest].set(flat_tok[order])
    buf_gate = jnp.zeros((P,), jnp.float32).at[dest].set(flat_g[order])
    block_e = jnp.minimum(jnp.searchsorted(pad_ends, jnp.arange(nb, dtype=jnp.int32) * ROW_BLOCK,
                                           side='right'), N_EXPERTS - 1).astype(jnp.int32)
    xpad = jnp.concatenate([xt, jnp.zeros((1, D), xt.dtype)], axis=0)
    xs = xpad[buf_tok].reshape(nb, ROW_BLOCK, D)

    def expert_block(args):
        xb, e = args
        hdn = jax.nn.silu(xb @ w_gate[e]) * (xb @ w_up[e])
        return hdn @ w_down[e]

    ys = lax.map(expert_block, (xs, block_e)).reshape(P, D)
    ys = ys * buf_gate[:, None].astype(ys.dtype)
    out = jnp.zeros((T + 1, D), ys.dtype).at[buf_tok].add(ys)[:T]
    return out.reshape(B, S, D)


def setup_inputs(seed: int = 0) -> dict:
    key = jax.random.key(seed)
    ks = jax.random.split(key, 20)
    f32 = jnp.float32
    L, D = DEPTH, D_MODEL
    nrm = lambda k, shape, s: jax.random.normal(k, shape, f32) * s
    return {
        "x": jax.random.normal(ks[0], (BATCH, SEQ, D), f32),
        "norm1_g": 1.0 + nrm(ks[1], (L, D), 0.02),
        "w_in": nrm(ks[2], (L, D, IN_COLS), D ** -0.5),
        "lambda_q1": nrm(ks[3], (L, DIFF_HEAD_DIM), 0.1),
        "lambda_k1": nrm(ks[4], (L, DIFF_HEAD_DIM), 0.1),
        "lambda_q2": nrm(ks[5], (L, DIFF_HEAD_DIM), 0.1),
        "lambda_k2": nrm(ks[6], (L, DIFF_HEAD_DIM), 0.1),
        "subln_g": 1.0 + nrm(ks[7], (L, DIFF_V_DIM), 0.02),
        "w_pool": nrm(ks[8], (L, POOL_GROUPS, POOL_GROUP_DIM, POOL_GROUP_DIM), POOL_GROUP_DIM ** -0.5),
        "pool_scale": 1.0 + nrm(ks[9], (L, POOL_WIDTH), 0.02),
        "w_out": nrm(ks[10], (L, D_MIX, D), D_MIX ** -0.5),
        "norm2_g": 1.0 + nrm(ks[11], (L, D), 0.02),
        "w_group_router": nrm(ks[12], (L, D, N_GROUPS), D ** -0.5),
        "b_group_router": nrm(ks[13], (L, N_GROUPS), 0.01),
        "w_expert_router": nrm(ks[14], (L, D, N_GROUPS, E_PER_GROUP), D ** -0.5),
        "b_expert_router": nrm(ks[15], (L, N_GROUPS, E_PER_GROUP), 0.01),
        "w_gate": nrm(ks[16], (L, N_EXPERTS, D, D_EXPERT), D ** -0.5),
        "w_up": nrm(ks[17], (L, N_EXPERTS, D, D_EXPERT), D ** -0.5),
        "w_down": nrm(ks[18], (L, N_EXPERTS, D_EXPERT, D), D_EXPERT ** -0.5),
        "final_g": 1.0 + nrm(ks[19], (D,), 0.02),
    }


def reference(x, norm1_g, w_in, lambda_q1, lambda_k1, lambda_q2, lambda_k2, subln_g,
              w_pool, pool_scale, w_out, norm2_g, w_group_router, b_group_router,
              w_expert_router, b_expert_router, w_gate, w_up, w_down, final_g):
    B, S, _ = x.shape
    H, Dh = N_DIFF_HEADS, DIFF_HEAD_DIM
    h = x
    for l in range(DEPTH):
        hn = rmsnorm(h, norm1_g[l])
        proj = hn @ w_in[l]
        q = proj[..., :QK_WIDTH].reshape(B, S, H, 2, Dh)
        k = proj[..., QK_WIDTH:2 * QK_WIDTH].reshape(B, S, H, 2, Dh)
        v = proj[..., 2 * QK_WIDTH:2 * QK_WIDTH + ATTN_WIDTH].reshape(B, S, H, DIFF_V_DIM)
        u = proj[..., 2 * QK_WIDTH + ATTN_WIDTH:]
        lam_init = 0.8 - 0.6 * math.exp(-0.3 * l)
        lam = (jnp.exp(jnp.sum(lambda_q1[l].astype(jnp.float32) * lambda_k1[l].astype(jnp.float32)))
               - jnp.exp(jnp.sum(lambda_q2[l].astype(jnp.float32) * lambda_k2[l].astype(jnp.float32)))
               + lam_init)
        a = diff_attention(q, k, v, lam, lam_init, subln_g[l])
        p = multiscale_pool(u, w_pool[l], pool_scale[l])
        h = h + jnp.concatenate([a, p], axis=-1) @ w_out[l]
        h = h + hierarchical_moe(rmsnorm(h, norm2_g[l]), w_group_router[l], b_group_router[l],
                                 w_expert_router[l], b_expert_router[l],
                                 w_gate[l], w_up[l], w_down[l])
    return rmsnorm(h, final_g)
```

```python
import functools
import math

import jax
import jax.numpy as jnp
from jax import lax
from jax.experimental import pallas as pl
from jax.experimental.pallas import tpu as pltpu

F32 = jnp.float32
BF16 = jnp.bfloat16

N_HEADS = 4
HEAD_DIM = 64
V_DIM = 128
POOL_WINDOWS = (2, 4, 8, 16)
N_GROUPS = 4
E_PER_GROUP = 8
N_EXPERTS = N_GROUPS * E_PER_GROUP
TOP_K = 2
EPS = 1e-6
LAM_INIT = 0.8 - 0.6 * math.exp(-0.3 * 0)
NEG = -1e30

LANES = 128
QK_PAD = 128
VT_ROWS = 144
POS_SPLIT = 128

TM_PROJ = 512
TQ = 256
TK = 512
EXP_BLK = 256
TM_COMB = 256
VMEM_LIMIT = 56 * 1024 * 1024


def _nt_dot(a, b):
    return lax.dot_general(a, b, (((1,), (1,)), ((), ())), preferred_element_type=F32)


def _in_proj_kernel(x_ref, g_ref, wqt_ref, wk_ref, wvt_ref, wu_ref,
                    qt_ref, k_ref, vt_ref, u_ref, *, seq):
    tm = x_ref.shape[0]
    i = pl.program_id(0)
    j0 = (i % (seq // tm)) * tm
    x = x_ref[...]
    ms = jnp.mean(x * x, axis=-1, keepdims=True)
    hn = (x * lax.rsqrt(ms + EPS) * g_ref[...]).astype(BF16)

    qt = _nt_dot(wqt_ref[...], hn) * (1.0 / math.sqrt(HEAD_DIM))
    for g in range(2 * N_HEADS):
        qt_ref[0, g] = qt[g * QK_PAD:(g + 1) * QK_PAD].astype(BF16)

    kk = jnp.dot(hn, wk_ref[...], preferred_element_type=F32)
    pos = j0 + lax.broadcasted_iota(jnp.int32, (tm, QK_PAD), 0)
    lane = lax.broadcasted_iota(jnp.int32, (tm, QK_PAD), 1)
    hi = (pos & ~(POS_SPLIT - 1)).astype(F32)
    lo = (pos & (POS_SPLIT - 1)).astype(F32)
    feat = jnp.where(lane == HEAD_DIM + 2, hi, jnp.where(lane == HEAD_DIM + 3, lo, 0.0))
    feat = jnp.where((lane == HEAD_DIM) | (lane == HEAD_DIM + 1), 1.0, feat)
    for g in range(2 * N_HEADS):
        k_ref[0, g] = (kk[:, g * QK_PAD:(g + 1) * QK_PAD] + feat).astype(BF16)

    vt = _nt_dot(wvt_ref[...], hn)
    row = lax.broadcasted_iota(jnp.int32, (VT_ROWS, tm), 0)
    for h in range(N_HEADS):
        blk = vt[h * VT_ROWS:(h + 1) * VT_ROWS]
        vt_ref[0, h, 0] = jnp.where(row == V_DIM, 1.0, blk).astype(BF16)

    u_ref[...] = jnp.dot(hn, wu_ref[...], preferred_element_type=F32)


def _in_proj(x2, g1, wqt, wk, wvt, wu, *, batch, seq):
    t, d = x2.shape
    tm = TM_PROJ
    nblk = seq // tm
    kern = functools.partial(_in_proj_kernel, seq=seq)
    full = lambda shape: pl.BlockSpec(shape, lambda i: (0,) * len(shape))
    return pl.pallas_call(
        kern,
        grid=(t // tm,),
        in_specs=[
            pl.BlockSpec((tm, d), lambda i: (i, 0)),
            full(g1.shape), full(wqt.shape), full(wk.shape), full(wvt.shape), full(wu.shape),
        ],
        out_specs=[
            pl.BlockSpec((1, 2 * N_HEADS, QK_PAD, tm), lambda i: (i // nblk, 0, 0, i % nblk)),
            pl.BlockSpec((1, 2 * N_HEADS, tm, QK_PAD), lambda i: (i // nblk, 0, i % nblk, 0)),
            pl.BlockSpec((1, N_HEADS, 1, VT_ROWS, tm), lambda i: (i // nblk, 0, i % nblk, 0, 0)),
            pl.BlockSpec((tm, wu.shape[1]), lambda i: (i, 0)),
        ],
        out_shape=[
            jax.ShapeDtypeStruct((batch, 2 * N_HEADS, QK_PAD, seq), BF16),
            jax.ShapeDtypeStruct((batch, 2 * N_HEADS, seq, QK_PAD), BF16),
            jax.ShapeDtypeStruct((batch, N_HEADS, nblk, VT_ROWS, tm), BF16),
            jax.ShapeDtypeStruct((t, wu.shape[1]), F32),
        ],
        compiler_params=pltpu.CompilerParams(
            dimension_semantics=("arbitrary",), vmem_limit_bytes=VMEM_LIMIT),
        name="in_proj",
    )(x2, g1, wqt, wk, wvt, wu)


def _attention_kernel(lam_ref, qt_ref, k_ref, vt_ref, g_ref, o_ref, qv_ref, acc_ref, m_ref):
    h = pl.program_id(1)
    qi = pl.program_id(2)
    tq = qt_ref.shape[-1]
    nkb, _, tk = vt_ref.shape[2:]
    jdiag = (qi * tq) // tk

    slope = jnp.where(h == 0, 2.0 ** -2, jnp.where(h == 1, 2.0 ** -4,
                      jnp.where(h == 2, 2.0 ** -6, 2.0 ** -8))).astype(F32)

    lv = lam_ref[...]
    lam = (jnp.exp(jnp.sum(lv[0:1] * lv[1:2], axis=-1, keepdims=True))
           - jnp.exp(jnp.sum(lv[2:3] * lv[3:4], axis=-1, keepdims=True)) + LAM_INIT)

    r = lax.broadcasted_iota(jnp.int32, (QK_PAD, tq), 0)
    ipos = qi * tq + lax.broadcasted_iota(jnp.int32, (QK_PAD, tq), 1)
    ihi = (ipos & ~(POS_SPLIT - 1)).astype(F32)
    ilo = (ipos & (POS_SPLIT - 1)).astype(F32)
    fi = jnp.where(r == HEAD_DIM, ihi * slope, jnp.where(r == HEAD_DIM + 1, ilo * slope, 0.0))
    fj = jnp.where((r == HEAD_DIM + 2) | (r == HEAD_DIM + 3), slope, 0.0)
    for c in range(2):
        q = qt_ref[0, c].astype(F32)
        qv_ref[0, c] = (q + fi - fj).astype(BF16)
        qv_ref[1, c] = (q - fi + fj).astype(BF16)
        qv_ref[2, c] = qt_ref[0, c]

    acc_ref[...] = jnp.zeros(acc_ref.shape, F32)
    m_ref[...] = jnp.full(m_ref.shape, NEG, F32)

    def step(kb, variant, bias):
        for c in range(2):
            kblk = k_ref[0, c, pl.ds(pl.multiple_of(kb * tk, tk), tk), :]
            st = jnp.dot(kblk, qv_ref[variant, c], preferred_element_type=F32)
            if bias is not None:
                st = st + bias
            m_old = m_ref[c]
            m_new = jnp.maximum(m_old, jnp.max(st, axis=0, keepdims=True))
            alpha = jnp.exp(m_old - m_new)
            p = jnp.exp(st - m_new).astype(BF16)
            pv = jnp.dot(vt_ref[0, 0, kb], p, preferred_element_type=F32)
            acc_ref[c] = acc_ref[c] * alpha + pv
            m_ref[c] = m_new

    def left_body(kb, carry):
        step(kb, 1, None)
        return carry

    def right_body(kb, carry):
        step(kb, 0, None)
        return carry

    lax.fori_loop(0, jdiag, left_body, 0)
    jpos = jdiag * tk + lax.broadcasted_iota(jnp.int32, (tk, tq), 0)
    iposd = qi * tq + lax.broadcasted_iota(jnp.int32, (tk, tq), 1)
    bias = -slope * jnp.abs(iposd - jpos).astype(F32)
    step(jdiag, 2, bias)
    lax.fori_loop(jdiag + 1, nkb, right_body, 0)

    a0 = acc_ref[0]
    a1 = acc_ref[1]
    o = (a0[:V_DIM] * (1.0 / a0[V_DIM:V_DIM + 1])
         - lam * (a1[:V_DIM] * (1.0 / a1[V_DIM:V_DIM + 1])))
    ms = jnp.mean(o * o, axis=0, keepdims=True)
    y = o * lax.rsqrt(ms + EPS) * g_ref[...] * (1.0 - LAM_INIT)
    o_ref[0] = y.T.astype(BF16)


def _attention(lam_in, qt, k, vt, gsub, *, batch, seq):
    nkb = seq // TK
    return pl.pallas_call(
        _attention_kernel,
        grid=(batch, N_HEADS, seq // TQ),
        in_specs=[
            pl.BlockSpec(lam_in.shape, lambda b, h, i: (0, 0)),
            pl.BlockSpec((1, 2, QK_PAD, TQ), lambda b, h, i: (b, h, 0, i)),
            pl.BlockSpec((1, 2, seq, QK_PAD), lambda b, h, i: (b, h, 0, 0)),
            pl.BlockSpec((1, 1, nkb, VT_ROWS, TK), lambda b, h, i: (b, h, 0, 0, 0)),
            pl.BlockSpec(gsub.shape, lambda b, h, i: (0, 0)),
        ],
        out_specs=pl.BlockSpec((1, TQ, V_DIM), lambda b, h, i: (b, i, h)),
        out_shape=jax.ShapeDtypeStruct((batch, seq, N_HEADS * V_DIM), BF16),
        scratch_shapes=[
            pltpu.VMEM((3, 2, QK_PAD, TQ), BF16),
            pltpu.VMEM((2, VT_ROWS, TQ), F32),
            pltpu.VMEM((2, 1, TQ), F32),
        ],
        compiler_params=pltpu.CompilerParams(
            dimension_semantics=("arbitrary", "arbitrary", "arbitrary"),
            vmem_limit_bytes=VMEM_LIMIT),
        name="diff_attention",
    )(lam_in, qt, k, vt, gsub)


def _mix_route_kernel(x_ref, a_ref, u_ref, up_ref, un_ref, wp_ref, ps_ref, wo_ref, g2_ref,
                      wr_ref, br_ref, h_ref, xn_ref, gate_ref, id_ref, *, seq):
    tm = x_ref.shape[1]
    j0 = pl.program_id(1) * tm
    halo = up_ref.shape[1]

    ext = jnp.concatenate([up_ref[0], u_ref[0], un_ref[0]], axis=0)
    epos = j0 - halo + lax.broadcasted_iota(jnp.int32, (tm + 2 * halo, 1), 0)
    ext = jnp.where((epos >= 0) & (epos < seq), ext, 0.0)
    tpos = j0 + lax.broadcasted_iota(jnp.int32, (tm, 1), 0)
    ys = []
    for g, w in enumerate(POOL_WINDOWS):
        e = ext[:, g * LANES:(g + 1) * LANES]
        n = e.shape[0]
        s = e[1:] + e[:-1]
        width, start = 2, 0
        while width < w:
            s = s[:-width] + s[width:]
            width *= 2
        off = halo - w // 2
        win = s[off:off + tm]
        cnt = (jnp.minimum(tpos + w // 2, seq) - jnp.maximum(tpos - w // 2, 0)).astype(F32)
        d = (win / cnt - u_ref[0][:, g * LANES:(g + 1) * LANES]).astype(BF16)
        yg = jnp.dot(d, wp_ref[g], preferred_element_type=F32)
        ys.append(yg * ps_ref[:, g * LANES:(g + 1) * LANES])
        del n, start
    p = jnp.concatenate(ys, axis=-1).astype(BF16)

    mix = jnp.concatenate([a_ref[0], p], axis=-1)
    hcur = x_ref[0] + jnp.dot(mix, wo_ref[...], preferred_element_type=F32)
    h_ref[0] = hcur
    ms = jnp.mean(hcur * hcur, axis=-1, keepdims=True)
    xn = hcur * lax.rsqrt(ms + EPS) * g2_ref[...]
    xn_ref[0] = xn

    logits = jnp.dot(xn.astype(BF16), wr_ref[...], preferred_element_type=F32) + br_ref[...]
    gl = logits[:, :LANES]
    el = logits[:, LANES:]
    lane = lax.broadcasted_iota(jnp.int32, (tm, LANES), 1)
    gmax = jnp.max(gl, axis=-1, keepdims=True)
    gsel = jnp.min(jnp.where(gl == gmax, lane, LANES), axis=-1, keepdims=True)
    pg = 1.0 / jnp.sum(jnp.exp(gl - gmax), axis=-1, keepdims=True)
    elm = jnp.where((lane >> 3) == gsel, el, NEG)
    m1 = jnp.max(elm, axis=-1, keepdims=True)
    i1 = jnp.min(jnp.where(elm == m1, lane, LANES), axis=-1, keepdims=True)
    elm2 = jnp.where(lane == i1, NEG, elm)
    m2 = jnp.max(elm2, axis=-1, keepdims=True)
    i2 = jnp.min(jnp.where(elm2 == m2, lane, LANES), axis=-1, keepdims=True)
    e2 = jnp.exp(m2 - m1)
    t1 = 1.0 / (1.0 + e2)
    gate_ref[0] = jnp.where(lane == 0, pg * t1, jnp.where(lane == 1, pg * (e2 * t1), 0.0))
    id_ref[0] = jnp.where(lane == 0, i1, jnp.where(lane == 1, i2, 0))


def _mix_route(x, a, u, wp, ps, wo, g2, wr, br, *, batch, seq):
    tm = TM_PROJ
    d = x.shape[-1]
    halo = 8
    nh = tm // halo
    kern = functools.partial(_mix_route_kernel, seq=seq)
    full = lambda arr: pl.BlockSpec(arr.shape, lambda b, i: (0,) * arr.ndim)
    tile = lambda w: pl.BlockSpec((1, tm, w), lambda b, i: (b, i, 0))
    return pl.pallas_call(
        kern,
        grid=(batch, seq // tm),
        in_specs=[
            tile(d), tile(a.shape[-1]), tile(u.shape[-1]),
            pl.BlockSpec((1, halo, u.shape[-1]), lambda b, i: (b, jnp.maximum(i * nh - 1, 0), 0)),
            pl.BlockSpec((1, halo, u.shape[-1]),
                         lambda b, i: (b, jnp.minimum((i + 1) * nh, seq // halo - 1), 0)),
            full(wp), full(ps), full(wo), full(g2), full(wr), full(br),
        ],
        out_specs=[tile(d), tile(d), tile(LANES), tile(LANES)],
        out_shape=[
            jax.ShapeDtypeStruct((batch, seq, d), F32),
            jax.ShapeDtypeStruct((batch, seq, d), F32),
            jax.ShapeDtypeStruct((batch, seq, LANES), F32),
            jax.ShapeDtypeStruct((batch, seq, LANES), jnp.int32),
        ],
        compiler_params=pltpu.CompilerParams(
            dimension_semantics=("arbitrary", "arbitrary"), vmem_limit_bytes=VMEM_LIMIT),
        name="mix_route",
    )(x, a, u, u, u, wp, ps, wo, g2, wr, br)


def _experts_kernel(be_ref, tok_ref, xn_hbm, gate_ref, wg_ref, wu_ref, wd_ref, y_ref, xbuf, sem):
    del be_ref
    blk = xbuf.shape[0]

    def row_copy(r):
        return pltpu.make_async_copy(xn_hbm.at[tok_ref[0, 0, r]], xbuf.at[r], sem)

    def start(r, carry):
        row_copy(r).start()
        return carry

    def wait(r, carry):
        row_copy(r).wait()
        return carry

    lax.fori_loop(0, blk, start, 0)
    lax.fori_loop(0, blk, wait, 0)
    xb = xbuf[...].astype(BF16)
    gt = jnp.dot(xb, wg_ref[0].astype(BF16), preferred_element_type=F32)
    up = jnp.dot(xb, wu_ref[0].astype(BF16), preferred_element_type=F32)
    hid = (gt * (1.0 / (1.0 + jnp.exp(-gt))) * up).astype(BF16)
    y = jnp.dot(hid, wd_ref[0].astype(BF16), preferred_element_type=F32)
    y_ref[...] = y * gate_ref[...]


def _experts(block_e, buf_tok, xn2, buf_gate, w_gate, w_up, w_down):
    nb = block_e.shape[0]
    d = xn2.shape[-1]
    de = w_gate.shape[-1]
    grid_spec = pltpu.PrefetchScalarGridSpec(
        num_scalar_prefetch=1,
        grid=(nb,),
        in_specs=[
            pl.BlockSpec((1, 1, EXP_BLK), lambda i, be: (i, 0, 0), memory_space=pltpu.SMEM),
            pl.BlockSpec(memory_space=pl.ANY),
            pl.BlockSpec((EXP_BLK, 1), lambda i, be: (i, 0)),
            pl.BlockSpec((1, d, de), lambda i, be: (be[i], 0, 0)),
            pl.BlockSpec((1, d, de), lambda i, be: (be[i], 0, 0)),
            pl.BlockSpec((1, de, d), lambda i, be: (be[i], 0, 0)),
        ],
        out_specs=pl.BlockSpec((EXP_BLK, d), lambda i, be: (i, 0)),
        scratch_shapes=[pltpu.VMEM((EXP_BLK, d), F32), pltpu.SemaphoreType.DMA(())],
    )
    return pl.pallas_call(
        _experts_kernel,
        grid_spec=grid_spec,
        out_shape=jax.ShapeDtypeStruct((nb * EXP_BLK, d), F32),
        compiler_params=pltpu.CompilerParams(
            dimension_semantics=("arbitrary",), vmem_limit_bytes=VMEM_LIMIT),
        name="experts",
    )(block_e, buf_tok.reshape(nb, 1, EXP_BLK), xn2, buf_gate.reshape(-1, 1), w_gate, w_up, w_down)


def _combine_kernel(pos_ref, h_ref, y_hbm, g_ref, o_ref, ybuf, sem):
    tm = h_ref.shape[0]

    def row_copy(r, k):
        return pltpu.make_async_copy(y_hbm.at[pos_ref[0, k, r]], ybuf.at[k, r], sem)

    def start(r, carry):
        row_copy(r, 0).start()
        row_copy(r, 1).start()
        return carry

    def wait(r, carry):
        row_copy(r, 0).wait()
        row_copy(r, 1).wait()
        return carry

    lax.fori_loop(0, tm, start, 0)
    lax.fori_loop(0, tm, wait, 0)
    hcur = h_ref[...] + (ybuf[0] + ybuf[1])
    ms = jnp.mean(hcur * hcur, axis=-1, keepdims=True)
    o_ref[...] = hcur * lax.rsqrt(ms + EPS) * g_ref[...]


def _combine(pos, h2, ys, gfin):
    t, d = h2.shape
    tm = TM_COMB
    nt = t // tm
    return pl.pallas_call(
        _combine_kernel,
        grid=(nt,),
        in_specs=[
            pl.BlockSpec((1, TOP_K, tm), lambda i: (i, 0, 0), memory_space=pltpu.SMEM),
            pl.BlockSpec((tm, d), lambda i: (i, 0)),
            pl.BlockSpec(memory_space=pl.ANY),
            pl.BlockSpec(gfin.shape, lambda i: (0, 0)),
        ],
        out_specs=pl.BlockSpec((tm, d), lambda i: (i, 0)),
        out_shape=jax.ShapeDtypeStruct((t, d), F32),
        scratch_shapes=[pltpu.VMEM((TOP_K, tm, d), F32), pltpu.SemaphoreType.DMA(())],
        compiler_params=pltpu.CompilerParams(
            dimension_semantics=("arbitrary",), vmem_limit_bytes=VMEM_LIMIT),
        name="combine",
    )(pos.reshape(nt, tm, TOP_K).transpose(0, 2, 1), h2, ys, gfin)


def _pad_head_cols(w):
    d = w.shape[0]
    w = w.reshape(d, 2 * N_HEADS, HEAD_DIM)
    return jnp.pad(w, ((0, 0), (0, 0), (0, QK_PAD - HEAD_DIM))).reshape(d, 2 * N_HEADS * QK_PAD)


def _dispatch_plan(ids, gates, t):
    a = t * TOP_K
    nb = (a + EXP_BLK - 1) // EXP_BLK + N_EXPERTS
    flat_e = ids.reshape(-1)
    flat_g = gates.reshape(-1)
    flat_tok = jnp.repeat(jnp.arange(t, dtype=jnp.int32), TOP_K)
    order = jnp.argsort(flat_e)
    se = flat_e[order]
    counts = jnp.bincount(flat_e, length=N_EXPERTS)
    starts = jnp.cumsum(counts) - counts
    padded = (counts + EXP_BLK - 1) // EXP_BLK * EXP_BLK
    pad_ends = jnp.cumsum(padded)
    pad_starts = pad_ends - padded
    dest = (pad_starts[se] + (jnp.arange(a, dtype=jnp.int32) - starts[se])).astype(jnp.int32)
    buf_tok = jnp.zeros((nb * EXP_BLK,), jnp.int32).at[dest].set(flat_tok[order])
    buf_gate = jnp.zeros((nb * EXP_BLK,), F32).at[dest].set(flat_g[order])
    pos = jnp.zeros((a,), jnp.int32).at[order].set(dest)
    block_e = jnp.minimum(
        jnp.searchsorted(pad_ends, jnp.arange(nb, dtype=jnp.int32) * EXP_BLK, side='right'),
        N_EXPERTS - 1).astype(jnp.int32)
    return block_e, buf_tok, buf_gate, pos


def kernel(x, norm1_g, w_in, lambda_q1, lambda_k1, lambda_q2, lambda_k2, subln_g, w_pool, pool_scale,
           w_out, norm2_g, w_group_router, b_group_router, w_expert_router, b_expert_router,
           w_gate, w_up, w_down, final_g):
    batch, seq, d = x.shape
    t = batch * seq
    qk_w = 2 * N_HEADS * HEAD_DIM
    av_w = N_HEADS * V_DIM
    l = 0
    assert seq % TK == 0 and seq % TM_PROJ == 0 and t % TM_COMB == 0 and TK == TM_PROJ

    wi = w_in[l]
    wqt = _pad_head_cols(wi[:, :qk_w]).T.astype(BF16)
    wk = _pad_head_cols(wi[:, qk_w:2 * qk_w]).astype(BF16)
    wv = wi[:, 2 * qk_w:2 * qk_w + av_w].reshape(d, N_HEADS, V_DIM)
    wvt = jnp.pad(wv, ((0, 0), (0, 0), (0, VT_ROWS - V_DIM))).reshape(d, N_HEADS * VT_ROWS).T.astype(BF16)
    wu = wi[:, 2 * qk_w + av_w:].astype(BF16)

    qt, k, vt, u = _in_proj(x.reshape(t, d), norm1_g[l].reshape(1, d), wqt, wk, wvt, wu,
                            batch=batch, seq=seq)

    lam_in = jnp.zeros((8, LANES), F32)
    lam_in = lam_in.at[0, :HEAD_DIM].set(lambda_q1[l]).at[1, :HEAD_DIM].set(lambda_k1[l])
    lam_in = lam_in.at[2, :HEAD_DIM].set(lambda_q2[l]).at[3, :HEAD_DIM].set(lambda_k2[l])
    gsub = jnp.broadcast_to(subln_g[l].reshape(V_DIM, 1), (V_DIM, TQ)).astype(F32)
    a = _attention(lam_in, qt, k, vt, gsub, batch=batch, seq=seq)

    wr = jnp.zeros((d, 2 * LANES), F32)
    wr = wr.at[:, :N_GROUPS].set(w_group_router[l])
    wr = wr.at[:, LANES:LANES + N_EXPERTS].set(w_expert_router[l].reshape(d, N_EXPERTS)).astype(BF16)
    br = jnp.full((1, 2 * LANES), NEG, F32)
    br = br.at[0, :N_GROUPS].set(b_group_router[l])
    br = br.at[0, LANES:LANES + N_EXPERTS].set(b_expert_router[l].reshape(N_EXPERTS))
    h, xn2, gates, ids = _mix_route(
        x, a, u.reshape(batch, seq, -1), w_pool[l].astype(BF16), pool_scale[l].reshape(1, -1),
        w_out[l].astype(BF16), norm2_g[l].reshape(1, d), wr, br, batch=batch, seq=seq)

    block_e, buf_tok, buf_gate, pos = _dispatch_plan(
        ids.reshape(t, LANES)[:, :TOP_K], gates.reshape(t, LANES)[:, :TOP_K], t)
    ys = _experts(block_e, buf_tok, xn2.reshape(t, d), buf_gate, w_gate[l], w_up[l], w_down[l])
    out = _combine(pos, h.reshape(t, d), ys, final_g.reshape(1, d))
    return out.reshape(batch, seq, d)
```

```python
import functools
import math

import jax
import jax.numpy as jnp
from jax import lax
from jax.experimental import pallas as pl
from jax.experimental.pallas import tpu as pltpu

F32 = jnp.float32
BF16 = jnp.bfloat16

N_HEADS = 4
HEAD_DIM = 64
V_DIM = 128
POOL_WINDOWS = (2, 4, 8, 16)
N_GROUPS = 4
E_PER_GROUP = 8
N_EXPERTS = N_GROUPS * E_PER_GROUP
TOP_K = 2
EPS = 1e-6
LAM_INIT = 0.8 - 0.6 * math.exp(-0.3 * 0)
NEG = -1e30

LANES = 128
QK_PAD = 128
VT_ROWS = 144
POS_SPLIT = 128

TM_PROJ = 512
TQ = 256
TK = 512
EXP_BLK = 256
TM_COMB = 256
VMEM_LIMIT = 56 * 1024 * 1024


def _nt_dot(a, b):
    return lax.dot_general(a, b, (((1,), (1,)), ((), ())), preferred_element_type=F32)


def _in_proj_kernel(x_ref, g_ref, wqt_ref, wk_ref, wvt_ref, wu_ref,
                    qt_ref, k_ref, vt_ref, u_ref, *, seq):
    tm = x_ref.shape[0]
    i = pl.program_id(0)
    j0 = (i % (seq // tm)) * tm
    x = x_ref[...]
    ms = jnp.mean(x * x, axis=-1, keepdims=True)
    hn = (x * lax.rsqrt(ms + EPS) * g_ref[...]).astype(BF16)

    qt = _nt_dot(wqt_ref[...], hn) * (1.0 / math.sqrt(HEAD_DIM))
    for g in range(2 * N_HEADS):
        qt_ref[0, g] = qt[g * QK_PAD:(g + 1) * QK_PAD].astype(BF16)

    kk = jnp.dot(hn, wk_ref[...], preferred_element_type=F32)
    pos = j0 + lax.broadcasted_iota(jnp.int32, (tm, QK_PAD), 0)
    lane = lax.broadcasted_iota(jnp.int32, (tm, QK_PAD), 1)
    hi = (pos & ~(POS_SPLIT - 1)).astype(F32)
    lo = (pos & (POS_SPLIT - 1)).astype(F32)
    feat = jnp.where(lane == HEAD_DIM + 2, hi, jnp.where(lane == HEAD_DIM + 3, lo, 0.0))
    feat = jnp.where((lane == HEAD_DIM) | (lane == HEAD_DIM + 1), 1.0, feat)
    for g in range(2 * N_HEADS):
        k_ref[0, g] = (kk[:, g * QK_PAD:(g + 1) * QK_PAD] + feat).astype(BF16)

    vt = _nt_dot(wvt_ref[...], hn)
    row = lax.broadcasted_iota(jnp.int32, (VT_ROWS, tm), 0)
    for h in range(N_HEADS):
        blk = vt[h * VT_ROWS:(h + 1) * VT_ROWS]
        vt_ref[0, h, 0] = jnp.where(row == V_DIM, 1.0, blk).astype(BF16)

    u_ref[...] = jnp.dot(hn, wu_ref[...], preferred_element_type=F32)


def _in_proj(x2, g1, wqt, wk, wvt, wu, *, batch, seq):
    t, d = x2.shape
    tm = TM_PROJ
    nblk = seq // tm
    kern = functools.partial(_in_proj_kernel, seq=seq)
    full = lambda shape: pl.BlockSpec(shape, lambda i: (0,) * len(shape))
    return pl.pallas_call(
        kern,
        grid=(t // tm,),
        in_specs=[
            pl.BlockSpec((tm, d), lambda i: (i, 0)),
            full(g1.shape), full(wqt.shape), full(wk.shape), full(wvt.shape), full(wu.shape),
        ],
        out_specs=[
            pl.BlockSpec((1, 2 * N_HEADS, QK_PAD, tm), lambda i: (i // nblk, 0, 0, i % nblk)),
            pl.BlockSpec((1, 2 * N_HEADS, tm, QK_PAD), lambda i: (i // nblk, 0, i % nblk, 0)),
            pl.BlockSpec((1, N_HEADS, 1, VT_ROWS, tm), lambda i: (i // nblk, 0, i % nblk, 0, 0)),
            pl.BlockSpec((tm, wu.shape[1]), lambda i: (i, 0)),
        ],
        out_shape=[
            jax.ShapeDtypeStruct((batch, 2 * N_HEADS, QK_PAD, seq), BF16),
            jax.ShapeDtypeStruct((batch, 2 * N_HEADS, seq, QK_PAD), BF16),
            jax.ShapeDtypeStruct((batch, N_HEADS, nblk, VT_ROWS, tm), BF16),
            jax.ShapeDtypeStruct((t, wu.shape[1]), F32),
        ],
        compiler_params=pltpu.CompilerParams(
            dimension_semantics=("arbitrary",), vmem_limit_bytes=VMEM_LIMIT),
        name="in_proj",
    )(x2, g1, wqt, wk, wvt, wu)


def _attention_kernel(lam_ref, qt_ref, k_ref, vt_ref, g_ref, o_ref,
                      qv_ref, acc_ref, m_ref, sa_ref, sb_ref, mba_ref, mbb_ref):
    h = pl.program_id(1)
    qi = pl.program_id(2)
    tq = qt_ref.shape[-1]
    nkb, _, tk = vt_ref.shape[2:]
    jdiag = (qi * tq) // tk

    slope = jnp.where(h == 0, 2.0 ** -2, jnp.where(h == 1, 2.0 ** -4,
                      jnp.where(h == 2, 2.0 ** -6, 2.0 ** -8))).astype(F32)

    lv = lam_ref[...]
    lam = (jnp.exp(jnp.sum(lv[0:1] * lv[1:2], axis=-1, keepdims=True))
           - jnp.exp(jnp.sum(lv[2:3] * lv[3:4], axis=-1, keepdims=True)) + LAM_INIT)

    r = lax.broadcasted_iota(jnp.int32, (QK_PAD, tq), 0)
    ipos = qi * tq + lax.broadcasted_iota(jnp.int32, (QK_PAD, tq), 1)
    ihi = (ipos & ~(POS_SPLIT - 1)).astype(F32)
    ilo = (ipos & (POS_SPLIT - 1)).astype(F32)
    fi = jnp.where(r == HEAD_DIM, ihi * slope, jnp.where(r == HEAD_DIM + 1, ilo * slope, 0.0))
    fj = jnp.where((r == HEAD_DIM + 2) | (r == HEAD_DIM + 3), slope, 0.0)
    for c in range(2):
        q = qt_ref[0, c].astype(F32)
        qv_ref[0, c] = (q + fi - fj).astype(BF16)
        qv_ref[1, c] = (q - fi + fj).astype(BF16)
        qv_ref[2, c] = qt_ref[0, c]

    acc_ref[...] = jnp.zeros(acc_ref.shape, F32)
    m_ref[...] = jnp.full(m_ref.shape, NEG, F32)

    def score_stage(kb, variant, s_ref, mb_ref, bias=None):
        for c in range(2):
            kblk = k_ref[0, c, pl.ds(pl.multiple_of(kb * tk, tk), tk), :]
            st = jnp.dot(kblk, qv_ref[variant, c], preferred_element_type=F32)
            if bias is not None:
                st = st + bias
            s_ref[c] = st
            mb_ref[c] = jnp.max(st, axis=0, keepdims=True)

    def softmax_stage(kb, s_ref, mb_ref):
        for c in range(2):
            m_old = m_ref[c]
            m_new = jnp.maximum(m_old, mb_ref[c])
            alpha = jnp.exp(m_old - m_new)
            p = jnp.exp(s_ref[c] - m_new).astype(BF16)
            pv = jnp.dot(vt_ref[0, 0, kb], p, preferred_element_type=F32)
            acc_ref[c] = acc_ref[c] * alpha + pv
            m_ref[c] = m_new

    def visit_block(n):
        o = n - 1
        return jnp.where(n == 0, jdiag, o + (o >= jdiag).astype(jnp.int32))

    def visit_variant(kb):
        return jnp.where(kb < jdiag, 1, 0)

    jpos = jdiag * tk + lax.broadcasted_iota(jnp.int32, (tk, tq), 0)
    iposd = qi * tq + lax.broadcasted_iota(jnp.int32, (tk, tq), 1)
    bias = -slope * jnp.abs(iposd - jpos).astype(F32)
    score_stage(jdiag, 2, sa_ref, mba_ref, bias)

    def pair_body(t, carry):
        kb0 = visit_block(2 * t)
        kb1 = visit_block(2 * t + 1)
        kb2 = visit_block(2 * t + 2)
        score_stage(kb1, visit_variant(kb1), sb_ref, mbb_ref)
        softmax_stage(kb0, sa_ref, mba_ref)
        score_stage(kb2, visit_variant(kb2), sa_ref, mba_ref)
        softmax_stage(kb1, sb_ref, mbb_ref)
        return carry

    lax.fori_loop(0, nkb // 2 - 1, pair_body, 0)
    kb_last = visit_block(nkb - 1)
    score_stage(kb_last, visit_variant(kb_last), sb_ref, mbb_ref)
    softmax_stage(visit_block(nkb - 2), sa_ref, mba_ref)
    softmax_stage(kb_last, sb_ref, mbb_ref)

    a0 = acc_ref[0]
    a1 = acc_ref[1]
    o = (a0[:V_DIM] * (1.0 / a0[V_DIM:V_DIM + 1])
         - lam * (a1[:V_DIM] * (1.0 / a1[V_DIM:V_DIM + 1])))
    ms = jnp.mean(o * o, axis=0, keepdims=True)
    y = o * lax.rsqrt(ms + EPS) * g_ref[...] * (1.0 - LAM_INIT)
    o_ref[0] = y.T.astype(BF16)


def _attention(lam_in, qt, k, vt, gsub, *, batch, seq):
    nkb = seq // TK
    return pl.pallas_call(
        _attention_kernel,
        grid=(batch, N_HEADS, seq // TQ),
        in_specs=[
            pl.BlockSpec(lam_in.shape, lambda b, h, i: (0, 0)),
            pl.BlockSpec((1, 2, QK_PAD, TQ), lambda b, h, i: (b, h, 0, i)),
            pl.BlockSpec((1, 2, seq, QK_PAD), lambda b, h, i: (b, h, 0, 0)),
            pl.BlockSpec((1, 1, nkb, VT_ROWS, TK), lambda b, h, i: (b, h, 0, 0, 0)),
            pl.BlockSpec(gsub.shape, lambda b, h, i: (0, 0)),
        ],
        out_specs=pl.BlockSpec((1, TQ, V_DIM), lambda b, h, i: (b, i, h)),
        out_shape=jax.ShapeDtypeStruct((batch, seq, N_HEADS * V_DIM), BF16),
        scratch_shapes=[
            pltpu.VMEM((3, 2, QK_PAD, TQ), BF16),
            pltpu.VMEM((2, VT_ROWS, TQ), F32),
            pltpu.VMEM((2, 1, TQ), F32),
            pltpu.VMEM((2, TK, TQ), F32),
            pltpu.VMEM((2, TK, TQ), F32),
            pltpu.VMEM((2, 1, TQ), F32),
            pltpu.VMEM((2, 1, TQ), F32),
        ],
        compiler_params=pltpu.CompilerParams(
            dimension_semantics=("arbitrary", "arbitrary", "arbitrary"),
            vmem_limit_bytes=VMEM_LIMIT),
        name="diff_attention",
    )(lam_in, qt, k, vt, gsub)


def _mix_route_kernel(x_ref, a_ref, u_ref, up_ref, un_ref, wp_ref, ps_ref, wo_ref, g2_ref,
                      wr_ref, br_ref, h_ref, xn_ref, gate_ref, id_ref, *, seq):
    tm = x_ref.shape[1]
    j0 = pl.program_id(1) * tm
    halo = up_ref.shape[1]

    ext = jnp.concatenate([up_ref[0], u_ref[0], un_ref[0]], axis=0)
    epos = j0 - halo + lax.broadcasted_iota(jnp.int32, (tm + 2 * halo, 1), 0)
    ext = jnp.where((epos >= 0) & (epos < seq), ext, 0.0)
    tpos = j0 + lax.broadcasted_iota(jnp.int32, (tm, 1), 0)
    ys = []
    for g, w in enumerate(POOL_WINDOWS):
        e = ext[:, g * LANES:(g + 1) * LANES]
        n = e.shape[0]
        s = e[1:] + e[:-1]
        width, start = 2, 0
        while width < w:
            s = s[:-width] + s[width:]
            width *= 2
        off = halo - w // 2
        win = s[off:off + tm]
        cnt = (jnp.minimum(tpos + w // 2, seq) - jnp.maximum(tpos - w // 2, 0)).astype(F32)
        d = (win / cnt - u_ref[0][:, g * LANES:(g + 1) * LANES]).astype(BF16)
        yg = jnp.dot(d, wp_ref[g], preferred_element_type=F32)
        ys.append(yg * ps_ref[:, g * LANES:(g + 1) * LANES])
        del n, start
    p = jnp.concatenate(ys, axis=-1).astype(BF16)

    mix = jnp.concatenate([a_ref[0], p], axis=-1)
    hcur = x_ref[0] + jnp.dot(mix, wo_ref[...], preferred_element_type=F32)
    h_ref[0] = hcur
    ms = jnp.mean(hcur * hcur, axis=-1, keepdims=True)
    xn = hcur * lax.rsqrt(ms + EPS) * g2_ref[...]
    xn_ref[0] = xn

    logits = jnp.dot(xn.astype(BF16), wr_ref[...], preferred_element_type=F32) + br_ref[...]
    gl = logits[:, :LANES]
    el = logits[:, LANES:]
    lane = lax.broadcasted_iota(jnp.int32, (tm, LANES), 1)
    gmax = jnp.max(gl, axis=-1, keepdims=True)
    gsel = jnp.min(jnp.where(gl == gmax, lane, LANES), axis=-1, keepdims=True)
    pg = 1.0 / jnp.sum(jnp.exp(gl - gmax), axis=-1, keepdims=True)
    elm = jnp.where((lane >> 3) == gsel, el, NEG)
    m1 = jnp.max(elm, axis=-1, keepdims=True)
    i1 = jnp.min(jnp.where(elm == m1, lane, LANES), axis=-1, keepdims=True)
    elm2 = jnp.where(lane == i1, NEG, elm)
    m2 = jnp.max(elm2, axis=-1, keepdims=True)
    i2 = jnp.min(jnp.where(elm2 == m2, lane, LANES), axis=-1, keepdims=True)
    e2 = jnp.exp(m2 - m1)
    t1 = 1.0 / (1.0 + e2)
    gate_ref[0] = jnp.where(lane == 0, pg * t1, jnp.where(lane == 1, pg * (e2 * t1), 0.0))
    id_ref[0] = jnp.where(lane == 0, i1, jnp.where(lane == 1, i2, 0))


def _mix_route(x, a, u, wp, ps, wo, g2, wr, br, *, batch, seq):
    tm = TM_PROJ
    d = x.shape[-1]
    halo = 8
    nh = tm // halo
    kern = functools.partial(_mix_route_kernel, seq=seq)
    full = lambda arr: pl.BlockSpec(arr.shape, lambda b, i: (0,) * arr.ndim)
    tile = lambda w: pl.BlockSpec((1, tm, w), lambda b, i: (b, i, 0))
    return pl.pallas_call(
        kern,
        grid=(batch, seq // tm),
        in_specs=[
            tile(d), tile(a.shape[-1]), tile(u.shape[-1]),
            pl.BlockSpec((1, halo, u.shape[-1]), lambda b, i: (b, jnp.maximum(i * nh - 1, 0), 0)),
            pl.BlockSpec((1, halo, u.shape[-1]),
                         lambda b, i: (b, jnp.minimum((i + 1) * nh, seq // halo - 1), 0)),
            full(wp), full(ps), full(wo), full(g2), full(wr), full(br),
        ],
        out_specs=[tile(d), tile(d), tile(LANES), tile(LANES)],
        out_shape=[
            jax.ShapeDtypeStruct((batch, seq, d), F32),
            jax.ShapeDtypeStruct((batch, seq, d), F32),
            jax.ShapeDtypeStruct((batch, seq, LANES), F32),
            jax.ShapeDtypeStruct((batch, seq, LANES), jnp.int32),
        ],
        compiler_params=pltpu.CompilerParams(
            dimension_semantics=("arbitrary", "arbitrary"), vmem_limit_bytes=VMEM_LIMIT),
        name="mix_route",
    )(x, a, u, u, u, wp, ps, wo, g2, wr, br)


def _experts_kernel(be_ref, tok_ref, xn_hbm, gate_ref, wg_ref, wu_ref, wd_ref, y_ref, xbuf, sem):
    del be_ref
    blk = xbuf.shape[0]

    def row_copy(r):
        return pltpu.make_async_copy(xn_hbm.at[tok_ref[0, 0, r]], xbuf.at[r], sem)

    def start(r, carry):
        row_copy(r).start()
        return carry

    def wait(r, carry):
        row_copy(r).wait()
        return carry

    lax.fori_loop(0, blk, start, 0)
    lax.fori_loop(0, blk, wait, 0)
    xb = xbuf[...].astype(BF16)
    gt = jnp.dot(xb, wg_ref[0].astype(BF16), preferred_element_type=F32)
    up = jnp.dot(xb, wu_ref[0].astype(BF16), preferred_element_type=F32)
    hid = (gt * (1.0 / (1.0 + jnp.exp(-gt))) * up).astype(BF16)
    y = jnp.dot(hid, wd_ref[0].astype(BF16), preferred_element_type=F32)
    y_ref[...] = y * gate_ref[...]


def _experts(block_e, buf_tok, xn2, buf_gate, w_gate, w_up, w_down):
    nb = block_e.shape[0]
    d = xn2.shape[-1]
    de = w_gate.shape[-1]
    grid_spec = pltpu.PrefetchScalarGridSpec(
        num_scalar_prefetch=1,
        grid=(nb,),
        in_specs=[
            pl.BlockSpec((1, 1, EXP_BLK), lambda i, be: (i, 0, 0), memory_space=pltpu.SMEM),
            pl.BlockSpec(memory_space=pl.ANY),
            pl.BlockSpec((EXP_BLK, 1), lambda i, be: (i, 0)),
            pl.BlockSpec((1, d, de), lambda i, be: (be[i], 0, 0)),
            pl.BlockSpec((1, d, de), lambda i, be: (be[i], 0, 0)),
            pl.BlockSpec((1, de, d), lambda i, be: (be[i], 0, 0)),
        ],
        out_specs=pl.BlockSpec((EXP_BLK, d), lambda i, be: (i, 0)),
        scratch_shapes=[pltpu.VMEM((EXP_BLK, d), F32), pltpu.SemaphoreType.DMA(())],
    )
    return pl.pallas_call(
        _experts_kernel,
        grid_spec=grid_spec,
        out_shape=jax.ShapeDtypeStruct((nb * EXP_BLK, d), F32),
        compiler_params=pltpu.CompilerParams(
            dimension_semantics=("arbitrary",), vmem_limit_bytes=VMEM_LIMIT),
        name="experts",
    )(block_e, buf_tok.reshape(nb, 1, EXP_BLK), xn2, buf_gate.reshape(-1, 1), w_gate, w_up, w_down)


def _combine_kernel(pos_ref, h_ref, y_hbm, g_ref, o_ref, ybuf, sem):
    tm = h_ref.shape[0]

    def row_copy(r, k):
        return pltpu.make_async_copy(y_hbm.at[pos_ref[0, k, r]], ybuf.at[k, r], sem)

    def start(r, carry):
        row_copy(r, 0).start()
        row_copy(r, 1).start()
        return carry

    def wait(r, carry):
        row_copy(r, 0).wait()
        row_copy(r, 1).wait()
        return carry

    lax.fori_loop(0, tm, start, 0)
    lax.fori_loop(0, tm, wait, 0)
    hcur = h_ref[...] + (ybuf[0] + ybuf[1])
    ms = jnp.mean(hcur * hcur, axis=-1, keepdims=True)
    o_ref[...] = hcur * lax.rsqrt(ms + EPS) * g_ref[...]


def _combine(pos, h2, ys, gfin):
    t, d = h2.shape
    tm = TM_COMB
    nt = t // tm
    return pl.pallas_call(
        _combine_kernel,
        grid=(nt,),
        in_specs=[
            pl.BlockSpec((1, TOP_K, tm), lambda i: (i, 0, 0), memory_space=pltpu.SMEM),
            pl.BlockSpec((tm, d), lambda i: (i, 0)),
            pl.BlockSpec(memory_space=pl.ANY),
            pl.BlockSpec(gfin.shape, lambda i: (0, 0)),
        ],
        out_specs=pl.BlockSpec((tm, d), lambda i: (i, 0)),
        out_shape=jax.ShapeDtypeStruct((t, d), F32),
        scratch_shapes=[pltpu.VMEM((TOP_K, tm, d), F32), pltpu.SemaphoreType.DMA(())],
        compiler_params=pltpu.CompilerParams(
            dimension_semantics=("arbitrary",), vmem_limit_bytes=VMEM_LIMIT),
        name="combine",
    )(pos.reshape(nt, tm, TOP_K).transpose(0, 2, 1), h2, ys, gfin)


def _pad_head_cols(w):
    d = w.shape[0]
    w = w.reshape(d, 2 * N_HEADS, HEAD_DIM)
    return jnp.pad(w, ((0, 0), (0, 0), (0, QK_PAD - HEAD_DIM))).reshape(d, 2 * N_HEADS * QK_PAD)


def _dispatch_plan(ids, gates, t):
    a = t * TOP_K
    nb = (a + EXP_BLK - 1) // EXP_BLK + N_EXPERTS
    flat_e = ids.reshape(-1)
    flat_g = gates.reshape(-1)
    flat_tok = jnp.repeat(jnp.arange(t, dtype=jnp.int32), TOP_K)
    order = jnp.argsort(flat_e)
    se = flat_e[order]
    counts = jnp.bincount(flat_e, length=N_EXPERTS)
    starts = jnp.cumsum(counts) - counts
    padded = (counts + EXP_BLK - 1) // EXP_BLK * EXP_BLK
    pad_ends = jnp.cumsum(padded)
    pad_starts = pad_ends - padded
    dest = (pad_starts[se] + (jnp.arange(a, dtype=jnp.int32) - starts[se])).astype(jnp.int32)
    buf_tok = jnp.zeros((nb * EXP_BLK,), jnp.int32).at[dest].set(flat_tok[order])
    buf_gate = jnp.zeros((nb * EXP_BLK,), F32).at[dest].set(flat_g[order])
    pos = jnp.zeros((a,), jnp.int32).at[order].set(dest)
    block_e = jnp.minimum(
        jnp.searchsorted(pad_ends, jnp.arange(nb, dtype=jnp.int32) * EXP_BLK, side='right'),
        N_EXPERTS - 1).astype(jnp.int32)
    return block_e, buf_tok, buf_gate, pos


def kernel(x, norm1_g, w_in, lambda_q1, lambda_k1, lambda_q2, lambda_k2, subln_g, w_pool, pool_scale,
           w_out, norm2_g, w_group_router, b_group_router, w_expert_router, b_expert_router,
           w_gate, w_up, w_down, final_g):
    batch, seq, d = x.shape
    t = batch * seq
    qk_w = 2 * N_HEADS * HEAD_DIM
    av_w = N_HEADS * V_DIM
    l = 0
    assert seq % (2 * TK) == 0 and seq % TM_PROJ == 0 and t % TM_COMB == 0 and TK == TM_PROJ

    wi = w_in[l]
    wqt = _pad_head_cols(wi[:, :qk_w]).T.astype(BF16)
    wk = _pad_head_cols(wi[:, qk_w:2 * qk_w]).astype(BF16)
    wv = wi[:, 2 * qk_w:2 * qk_w + av_w].reshape(d, N_HEADS, V_DIM)
    wvt = jnp.pad(wv, ((0, 0), (0, 0), (0, VT_ROWS - V_DIM))).reshape(d, N_HEADS * VT_ROWS).T.astype(BF16)
    wu = wi[:, 2 * qk_w + av_w:].astype(BF16)

    qt, k, vt, u = _in_proj(x.reshape(t, d), norm1_g[l].reshape(1, d), wqt, wk, wvt, wu,
                            batch=batch, seq=seq)

    lam_in = jnp.zeros((8, LANES), F32)
    lam_in = lam_in.at[0, :HEAD_DIM].set(lambda_q1[l]).at[1, :HEAD_DIM].set(lambda_k1[l])
    lam_in = lam_in.at[2, :HEAD_DIM].set(lambda_q2[l]).at[3, :HEAD_DIM].set(lambda_k2[l])
    gsub = jnp.broadcast_to(subln_g[l].reshape(V_DIM, 1), (V_DIM, TQ)).astype(F32)
    a = _attention(lam_in, qt, k, vt, gsub, batch=batch, seq=seq)

    wr = jnp.zeros((d, 2 * LANES), F32)
    wr = wr.at[:, :N_GROUPS].set(w_group_router[l])
    wr = wr.at[:, LANES:LANES + N_EXPERTS].set(w_expert_router[l].reshape(d, N_EXPERTS)).astype(BF16)
    br = jnp.full((1, 2 * LANES), NEG, F32)
    br = br.at[0, :N_GROUPS].set(b_group_router[l])
    br = br.at[0, LANES:LANES + N_EXPERTS].set(b_expert_router[l].reshape(N_EXPERTS))
    h, xn2, gates, ids = _mix_route(
        x, a, u.reshape(batch, seq, -1), w_pool[l].astype(BF16), pool_scale[l].reshape(1, -1),
        w_out[l].astype(BF16), norm2_g[l].reshape(1, d), wr, br, batch=batch, seq=seq)

    block_e, buf_tok, buf_gate, pos = _dispatch_plan(
        ids.reshape(t, LANES)[:, :TOP_K], gates.reshape(t, LANES)[:, :TOP_K], t)
    ys = _experts(block_e, buf_tok, xn2.reshape(t, d), buf_gate, w_gate[l], w_up[l], w_down[l])
    out = _combine(pos, h.reshape(t, d), ys, final_g.reshape(1, d))
    return out.reshape(batch, seq, d)
```

```python
import functools
import math

import jax
import jax.numpy as jnp
from jax import lax
from jax.experimental import pallas as pl
from jax.experimental.pallas import tpu as pltpu

F32 = jnp.float32
BF16 = jnp.bfloat16

N_HEADS = 4
HEAD_DIM = 64
V_DIM = 128
POOL_WINDOWS = (2, 4, 8, 16)
N_GROUPS = 4
E_PER_GROUP = 8
N_EXPERTS = N_GROUPS * E_PER_GROUP
TOP_K = 2
EPS = 1e-6
LAM_INIT = 0.8 - 0.6 * math.exp(-0.3 * 0)
NEG = -1e30

LANES = 128
QK_PAD = 128
VT_ROWS = 144
POS_SPLIT = 128

TM_PROJ = 512
TQ = 256
TK = 512
EXP_BLK = 256
TM_COMB = 256
DMA_UNROLL = 8
VMEM_LIMIT = 56 * 1024 * 1024


def _nt_dot(a, b):
    return lax.dot_general(a, b, (((1,), (1,)), ((), ())), preferred_element_type=F32)


def _in_proj_kernel(x_ref, g_ref, wqt_ref, wk_ref, wvt_ref, wu_ref,
                    qt_ref, k_ref, vt_ref, u_ref, *, seq):
    tm = x_ref.shape[0]
    i = pl.program_id(0)
    j0 = (i % (seq // tm)) * tm
    x = x_ref[...]
    ms = jnp.mean(x * x, axis=-1, keepdims=True)
    hn = (x * lax.rsqrt(ms + EPS) * g_ref[...]).astype(BF16)

    qt = _nt_dot(wqt_ref[...], hn) * (1.0 / math.sqrt(HEAD_DIM))
    for g in range(2 * N_HEADS):
        qt_ref[0, g] = qt[g * QK_PAD:(g + 1) * QK_PAD].astype(BF16)

    kk = jnp.dot(hn, wk_ref[...], preferred_element_type=F32)
    pos = j0 + lax.broadcasted_iota(jnp.int32, (tm, QK_PAD), 0)
    lane = lax.broadcasted_iota(jnp.int32, (tm, QK_PAD), 1)
    hi = (pos & ~(POS_SPLIT - 1)).astype(F32)
    lo = (pos & (POS_SPLIT - 1)).astype(F32)
    feat = jnp.where(lane == HEAD_DIM + 2, hi, jnp.where(lane == HEAD_DIM + 3, lo, 0.0))
    feat = jnp.where((lane == HEAD_DIM) | (lane == HEAD_DIM + 1), 1.0, feat)
    for g in range(2 * N_HEADS):
        k_ref[0, g] = (kk[:, g * QK_PAD:(g + 1) * QK_PAD] + feat).astype(BF16)

    vt = _nt_dot(wvt_ref[...], hn)
    row = lax.broadcasted_iota(jnp.int32, (VT_ROWS, tm), 0)
    for h in range(N_HEADS):
        blk = vt[h * VT_ROWS:(h + 1) * VT_ROWS]
        vt_ref[0, h, 0] = jnp.where(row == V_DIM, 1.0, blk).astype(BF16)

    u_ref[...] = jnp.dot(hn, wu_ref[...], preferred_element_type=F32)


def _in_proj(x2, g1, wqt, wk, wvt, wu, *, batch, seq):
    t, d = x2.shape
    tm = TM_PROJ
    nblk = seq // tm
    kern = functools.partial(_in_proj_kernel, seq=seq)
    full = lambda shape: pl.BlockSpec(shape, lambda i: (0,) * len(shape))
    return pl.pallas_call(
        kern,
        grid=(t // tm,),
        in_specs=[
            pl.BlockSpec((tm, d), lambda i: (i, 0)),
            full(g1.shape), full(wqt.shape), full(wk.shape), full(wvt.shape), full(wu.shape),
        ],
        out_specs=[
            pl.BlockSpec((1, 2 * N_HEADS, QK_PAD, tm), lambda i: (i // nblk, 0, 0, i % nblk)),
            pl.BlockSpec((1, 2 * N_HEADS, tm, QK_PAD), lambda i: (i // nblk, 0, i % nblk, 0)),
            pl.BlockSpec((1, N_HEADS, 1, VT_ROWS, tm), lambda i: (i // nblk, 0, i % nblk, 0, 0)),
            pl.BlockSpec((tm, wu.shape[1]), lambda i: (i, 0)),
        ],
        out_shape=[
            jax.ShapeDtypeStruct((batch, 2 * N_HEADS, QK_PAD, seq), BF16),
            jax.ShapeDtypeStruct((batch, 2 * N_HEADS, seq, QK_PAD), BF16),
            jax.ShapeDtypeStruct((batch, N_HEADS, nblk, VT_ROWS, tm), BF16),
            jax.ShapeDtypeStruct((t, wu.shape[1]), F32),
        ],
        compiler_params=pltpu.CompilerParams(
            dimension_semantics=("arbitrary",), vmem_limit_bytes=VMEM_LIMIT),
        name="in_proj",
    )(x2, g1, wqt, wk, wvt, wu)


def _attention_kernel(lam_ref, qt_ref, k_ref, vt_ref, g_ref, o_ref,
                      qv_ref, acc_ref, m_ref, sa_ref, sb_ref, mba_ref, mbb_ref):
    h = pl.program_id(1)
    qi = pl.program_id(2)
    tq = qt_ref.shape[-1]
    nkb, _, tk = vt_ref.shape[2:]
    jdiag = (qi * tq) // tk

    slope = jnp.where(h == 0, 2.0 ** -2, jnp.where(h == 1, 2.0 ** -4,
                      jnp.where(h == 2, 2.0 ** -6, 2.0 ** -8))).astype(F32)

    lv = lam_ref[...]
    lam = (jnp.exp(jnp.sum(lv[0:1] * lv[1:2], axis=-1, keepdims=True))
           - jnp.exp(jnp.sum(lv[2:3] * lv[3:4], axis=-1, keepdims=True)) + LAM_INIT)

    r = lax.broadcasted_iota(jnp.int32, (QK_PAD, tq), 0)
    ipos = qi * tq + lax.broadcasted_iota(jnp.int32, (QK_PAD, tq), 1)
    ihi = (ipos & ~(POS_SPLIT - 1)).astype(F32)
    ilo = (ipos & (POS_SPLIT - 1)).astype(F32)
    fi = jnp.where(r == HEAD_DIM, ihi * slope, jnp.where(r == HEAD_DIM + 1, ilo * slope, 0.0))
    fj = jnp.where((r == HEAD_DIM + 2) | (r == HEAD_DIM + 3), slope, 0.0)
    for c in range(2):
        q = qt_ref[0, c].astype(F32)
        qv_ref[0, c] = (q + fi - fj).astype(BF16)
        qv_ref[1, c] = (q - fi + fj).astype(BF16)
        qv_ref[2, c] = qt_ref[0, c]

    acc_ref[...] = jnp.zeros(acc_ref.shape, F32)
    m_ref[...] = jnp.full(m_ref.shape, NEG, F32)

    def score_stage(kb, variant, s_ref, mb_ref, bias=None):
        for c in range(2):
            kblk = k_ref[0, c, pl.ds(pl.multiple_of(kb * tk, tk), tk), :]
            st = jnp.dot(kblk, qv_ref[variant, c], preferred_element_type=F32)
            if bias is not None:
                st = st + bias
            s_ref[c] = st
            mb_ref[c] = jnp.max(st, axis=0, keepdims=True)

    def softmax_stage(kb, s_ref, mb_ref):
        for c in range(2):
            m_old = m_ref[c]
            m_new = jnp.maximum(m_old, mb_ref[c])
            alpha = jnp.exp(m_old - m_new)
            p = jnp.exp(s_ref[c] - m_new).astype(BF16)
            pv = jnp.dot(vt_ref[0, 0, kb], p, preferred_element_type=F32)
            acc_ref[c] = acc_ref[c] * alpha + pv
            m_ref[c] = m_new

    def visit_block(n):
        o = n - 1
        return jnp.where(n == 0, jdiag, o + (o >= jdiag).astype(jnp.int32))

    def visit_variant(kb):
        return jnp.where(kb < jdiag, 1, 0)

    jpos = jdiag * tk + lax.broadcasted_iota(jnp.int32, (tk, tq), 0)
    iposd = qi * tq + lax.broadcasted_iota(jnp.int32, (tk, tq), 1)
    bias = -slope * jnp.abs(iposd - jpos).astype(F32)
    score_stage(jdiag, 2, sa_ref, mba_ref, bias)

    def pair_body(t, carry):
        kb0 = visit_block(2 * t)
        kb1 = visit_block(2 * t + 1)
        kb2 = visit_block(2 * t + 2)
        score_stage(kb1, visit_variant(kb1), sb_ref, mbb_ref)
        softmax_stage(kb0, sa_ref, mba_ref)
        score_stage(kb2, visit_variant(kb2), sa_ref, mba_ref)
        softmax_stage(kb1, sb_ref, mbb_ref)
        return carry

    lax.fori_loop(0, nkb // 2 - 1, pair_body, 0)
    kb_last = visit_block(nkb - 1)
    score_stage(kb_last, visit_variant(kb_last), sb_ref, mbb_ref)
    softmax_stage(visit_block(nkb - 2), sa_ref, mba_ref)
    softmax_stage(kb_last, sb_ref, mbb_ref)

    a0 = acc_ref[0]
    a1 = acc_ref[1]
    o = (a0[:V_DIM] * (1.0 / a0[V_DIM:V_DIM + 1])
         - lam * (a1[:V_DIM] * (1.0 / a1[V_DIM:V_DIM + 1])))
    ms = jnp.mean(o * o, axis=0, keepdims=True)
    y = o * lax.rsqrt(ms + EPS) * g_ref[...] * (1.0 - LAM_INIT)
    o_ref[0] = y.T.astype(BF16)


def _attention(lam_in, qt, k, vt, gsub, *, batch, seq):
    nkb = seq // TK
    return pl.pallas_call(
        _attention_kernel,
        grid=(batch, N_HEADS, seq // TQ),
        in_specs=[
            pl.BlockSpec(lam_in.shape, lambda b, h, i: (0, 0)),
            pl.BlockSpec((1, 2, QK_PAD, TQ), lambda b, h, i: (b, h, 0, i)),
            pl.BlockSpec((1, 2, seq, QK_PAD), lambda b, h, i: (b, h, 0, 0)),
            pl.BlockSpec((1, 1, nkb, VT_ROWS, TK), lambda b, h, i: (b, h, 0, 0, 0)),
            pl.BlockSpec(gsub.shape, lambda b, h, i: (0, 0)),
        ],
        out_specs=pl.BlockSpec((1, TQ, V_DIM), lambda b, h, i: (b, i, h)),
        out_shape=jax.ShapeDtypeStruct((batch, seq, N_HEADS * V_DIM), BF16),
        scratch_shapes=[
            pltpu.VMEM((3, 2, QK_PAD, TQ), BF16),
            pltpu.VMEM((2, VT_ROWS, TQ), F32),
            pltpu.VMEM((2, 1, TQ), F32),
            pltpu.VMEM((2, TK, TQ), F32),
            pltpu.VMEM((2, TK, TQ), F32),
            pltpu.VMEM((2, 1, TQ), F32),
            pltpu.VMEM((2, 1, TQ), F32),
        ],
        compiler_params=pltpu.CompilerParams(
            dimension_semantics=("arbitrary", "arbitrary", "arbitrary"),
            vmem_limit_bytes=VMEM_LIMIT),
        name="diff_attention",
    )(lam_in, qt, k, vt, gsub)


def _mix_route_kernel(x_ref, a_ref, u_ref, up_ref, un_ref, wp_ref, ps_ref, wo_ref, g2_ref,
                      wr_ref, br_ref, h_ref, xn_ref, gate_ref, id_ref, count_ref, cnt_ref, *, seq):
    tm = x_ref.shape[1]
    j0 = pl.program_id(1) * tm
    halo = up_ref.shape[1]

    ext = jnp.concatenate([up_ref[0], u_ref[0], un_ref[0]], axis=0)
    epos = j0 - halo + lax.broadcasted_iota(jnp.int32, (tm + 2 * halo, 1), 0)
    ext = jnp.where((epos >= 0) & (epos < seq), ext, 0.0)
    tpos = j0 + lax.broadcasted_iota(jnp.int32, (tm, 1), 0)
    ys = []
    for g, w in enumerate(POOL_WINDOWS):
        e = ext[:, g * LANES:(g + 1) * LANES]
        n = e.shape[0]
        s = e[1:] + e[:-1]
        width, start = 2, 0
        while width < w:
            s = s[:-width] + s[width:]
            width *= 2
        off = halo - w // 2
        win = s[off:off + tm]
        cnt = (jnp.minimum(tpos + w // 2, seq) - jnp.maximum(tpos - w // 2, 0)).astype(F32)
        d = (win / cnt - u_ref[0][:, g * LANES:(g + 1) * LANES]).astype(BF16)
        yg = jnp.dot(d, wp_ref[g], preferred_element_type=F32)
        ys.append(yg * ps_ref[:, g * LANES:(g + 1) * LANES])
        del n, start
    p = jnp.concatenate(ys, axis=-1).astype(BF16)

    mix = jnp.concatenate([a_ref[0], p], axis=-1)
    hcur = x_ref[0] + jnp.dot(mix, wo_ref[...], preferred_element_type=F32)
    h_ref[0] = hcur
    ms = jnp.mean(hcur * hcur, axis=-1, keepdims=True)
    xn = hcur * lax.rsqrt(ms + EPS) * g2_ref[...]
    xn_ref[0] = xn

    logits = jnp.dot(xn.astype(BF16), wr_ref[...], preferred_element_type=F32) + br_ref[...]
    gl = logits[:, :LANES]
    el = logits[:, LANES:]
    lane = lax.broadcasted_iota(jnp.int32, (tm, LANES), 1)
    gmax = jnp.max(gl, axis=-1, keepdims=True)
    gsel = jnp.min(jnp.where(gl == gmax, lane, LANES), axis=-1, keepdims=True)
    pg = 1.0 / jnp.sum(jnp.exp(gl - gmax), axis=-1, keepdims=True)
    elm = jnp.where((lane >> 3) == gsel, el, NEG)
    m1 = jnp.max(elm, axis=-1, keepdims=True)
    i1 = jnp.min(jnp.where(elm == m1, lane, LANES), axis=-1, keepdims=True)
    elm2 = jnp.where(lane == i1, NEG, elm)
    m2 = jnp.max(elm2, axis=-1, keepdims=True)
    i2 = jnp.min(jnp.where(elm2 == m2, lane, LANES), axis=-1, keepdims=True)
    e2 = jnp.exp(m2 - m1)
    t1 = 1.0 / (1.0 + e2)
    gate_ref[0] = jnp.where(lane == 0, pg * t1, jnp.where(lane == 1, pg * (e2 * t1), 0.0))

    @pl.when((pl.program_id(0) == 0) & (pl.program_id(1) == 0))
    def _():
        cnt_ref[...] = jnp.zeros(cnt_ref.shape, F32)

    hit1 = lane == i1
    hit2 = lane == i2
    onehot = jnp.where(hit1 | hit2, 1.0, 0.0)
    rr = lax.broadcasted_iota(jnp.int32, (tm, tm), 0)
    cc = lax.broadcasted_iota(jnp.int32, (tm, tm), 1)
    tril = jnp.where(cc < rr, 1.0, 0.0).astype(BF16)
    before = jnp.dot(tril, onehot.astype(BF16), preferred_element_type=F32) + cnt_ref[...]
    r1 = jnp.sum(jnp.where(hit1, before, 0.0), axis=-1, keepdims=True).astype(jnp.int32)
    r2 = jnp.sum(jnp.where(hit2, before, 0.0), axis=-1, keepdims=True).astype(jnp.int32)
    cnt_ref[...] = cnt_ref[...] + jnp.sum(onehot, axis=0, keepdims=True)
    count_ref[...] = cnt_ref[...]
    id_ref[0] = jnp.where(lane == 0, i1, jnp.where(lane == 1, i2,
                          jnp.where(lane == 2, r1, jnp.where(lane == 3, r2, 0))))


def _mix_route(x, a, u, wp, ps, wo, g2, wr, br, *, batch, seq):
    tm = TM_PROJ
    d = x.shape[-1]
    halo = 8
    nh = tm // halo
    kern = functools.partial(_mix_route_kernel, seq=seq)
    full = lambda arr: pl.BlockSpec(arr.shape, lambda b, i: (0,) * arr.ndim)
    tile = lambda w: pl.BlockSpec((1, tm, w), lambda b, i: (b, i, 0))
    return pl.pallas_call(
        kern,
        grid=(batch, seq // tm),
        in_specs=[
            tile(d), tile(a.shape[-1]), tile(u.shape[-1]),
            pl.BlockSpec((1, halo, u.shape[-1]), lambda b, i: (b, jnp.maximum(i * nh - 1, 0), 0)),
            pl.BlockSpec((1, halo, u.shape[-1]),
                         lambda b, i: (b, jnp.minimum((i + 1) * nh, seq // halo - 1), 0)),
            full(wp), full(ps), full(wo), full(g2), full(wr), full(br),
        ],
        out_specs=[tile(d), tile(d), tile(LANES), tile(LANES),
                   pl.BlockSpec((1, LANES), lambda b, i: (0, 0))],
        out_shape=[
            jax.ShapeDtypeStruct((batch, seq, d), F32),
            jax.ShapeDtypeStruct((batch, seq, d), F32),
            jax.ShapeDtypeStruct((batch, seq, LANES), F32),
            jax.ShapeDtypeStruct((batch, seq, LANES), jnp.int32),
            jax.ShapeDtypeStruct((1, LANES), F32),
        ],
        scratch_shapes=[pltpu.VMEM((1, LANES), F32)],
        compiler_params=pltpu.CompilerParams(
            dimension_semantics=("arbitrary", "arbitrary"), vmem_limit_bytes=VMEM_LIMIT),
        name="mix_route",
    )(x, a, u, u, u, wp, ps, wo, g2, wr, br)


def _dispatch_kernel(dest_ref, x_ref, xs_in, xs_hbm, sem):
    del xs_in
    tm = x_ref.shape[0]

    def group(gi, carry):
        for j in range(DMA_UNROLL):
            r = gi * DMA_UNROLL + j
            for k in range(TOP_K):
                pltpu.make_async_copy(x_ref.at[r], xs_hbm.at[dest_ref[0, k, r]], sem).start()
        return carry

    lax.fori_loop(0, tm // DMA_UNROLL, group, 0)
    for k in range(TOP_K):
        pltpu.make_async_copy(x_ref, xs_hbm.at[pl.ds(0, tm)], sem).wait()


def _dispatch(dest, xn2, xs_zero):
    t, d = xn2.shape
    tm = TM_COMB
    nt = t // tm
    return pl.pallas_call(
        _dispatch_kernel,
        grid=(nt,),
        in_specs=[
            pl.BlockSpec((1, TOP_K, tm), lambda i: (i, 0, 0), memory_space=pltpu.SMEM),
            pl.BlockSpec((tm, d), lambda i: (i, 0)),
            pl.BlockSpec(memory_space=pl.ANY),
        ],
        out_specs=pl.BlockSpec(memory_space=pl.ANY),
        out_shape=jax.ShapeDtypeStruct(xs_zero.shape, xs_zero.dtype),
        input_output_aliases={2: 0},
        scratch_shapes=[pltpu.SemaphoreType.DMA(())],
        compiler_params=pltpu.CompilerParams(
            dimension_semantics=("arbitrary",), vmem_limit_bytes=VMEM_LIMIT),
        name="dispatch",
    )(dest.reshape(nt, tm, TOP_K).transpose(0, 2, 1), xn2, xs_zero)


def _experts_kernel(be_ref, nv_ref, xs_ref, wg_ref, wu_ref, wd_ref, y_ref, wgb, wub, wdb):
    i = pl.program_id(0)
    changed = jnp.logical_or(i == 0, be_ref[i] != be_ref[jnp.maximum(i - 1, 0)])

    @pl.when(changed)
    def _():
        wgb[...] = wg_ref[0].astype(BF16)
        wub[...] = wu_ref[0].astype(BF16)
        wdb[...] = wd_ref[0].astype(BF16)

    @pl.when(nv_ref[i] > 0)
    def _():
        xb = xs_ref[...].astype(BF16)
        gt = jnp.dot(xb, wgb[...], preferred_element_type=F32)
        up = jnp.dot(xb, wub[...], preferred_element_type=F32)
        hid = (gt * (1.0 / (1.0 + jnp.exp(-gt))) * up).astype(BF16)
        y_ref[...] = jnp.dot(hid, wdb[...], preferred_element_type=F32)

    @pl.when(nv_ref[i] == 0)
    def _():
        y_ref[...] = jnp.zeros(y_ref.shape, F32)


def _experts(block_e, nvalid, xs, w_gate, w_up, w_down):
    nb = block_e.shape[0]
    d = xs.shape[-1]
    de = w_gate.shape[-1]
    grid_spec = pltpu.PrefetchScalarGridSpec(
        num_scalar_prefetch=2,
        grid=(nb,),
        in_specs=[
            pl.BlockSpec((EXP_BLK, d), lambda i, be, nv: (i, 0)),
            pl.BlockSpec((1, d, de), lambda i, be, nv: (be[i], 0, 0)),
            pl.BlockSpec((1, d, de), lambda i, be, nv: (be[i], 0, 0)),
            pl.BlockSpec((1, de, d), lambda i, be, nv: (be[i], 0, 0)),
        ],
        out_specs=pl.BlockSpec((EXP_BLK, d), lambda i, be, nv: (i, 0)),
        scratch_shapes=[pltpu.VMEM((d, de), BF16), pltpu.VMEM((d, de), BF16), pltpu.VMEM((de, d), BF16)],
    )
    return pl.pallas_call(
        _experts_kernel,
        grid_spec=grid_spec,
        out_shape=jax.ShapeDtypeStruct((nb * EXP_BLK, d), F32),
        compiler_params=pltpu.CompilerParams(
            dimension_semantics=("arbitrary",), vmem_limit_bytes=VMEM_LIMIT),
        name="experts",
    )(block_e, nvalid, xs, w_gate, w_up, w_down)


def _combine_kernel(dest_ref, h_ref, gate_ref, y_hbm, g_ref, o_ref, ybuf, sem):
    tm = h_ref.shape[0]

    def group(gi, carry):
        for j in range(DMA_UNROLL):
            r = gi * DMA_UNROLL + j
            for k in range(TOP_K):
                pltpu.make_async_copy(y_hbm.at[dest_ref[0, k, r]], ybuf.at[k, r], sem).start()
        return carry

    lax.fori_loop(0, tm // DMA_UNROLL, group, 0)
    for k in range(TOP_K):
        pltpu.make_async_copy(y_hbm.at[pl.ds(0, tm)], ybuf.at[k], sem).wait()
    gates = gate_ref[...]
    hcur = h_ref[...] + (gates[:, 0:1] * ybuf[0] + gates[:, 1:2] * ybuf[1])
    ms = jnp.mean(hcur * hcur, axis=-1, keepdims=True)
    o_ref[...] = hcur * lax.rsqrt(ms + EPS) * g_ref[...]


def _combine(dest, h2, gates, ys, gfin):
    t, d = h2.shape
    tm = TM_COMB
    nt = t // tm
    return pl.pallas_call(
        _combine_kernel,
        grid=(nt,),
        in_specs=[
            pl.BlockSpec((1, TOP_K, tm), lambda i: (i, 0, 0), memory_space=pltpu.SMEM),
            pl.BlockSpec((tm, d), lambda i: (i, 0)),
            pl.BlockSpec((tm, LANES), lambda i: (i, 0)),
            pl.BlockSpec(memory_space=pl.ANY),
            pl.BlockSpec(gfin.shape, lambda i: (0, 0)),
        ],
        out_specs=pl.BlockSpec((tm, d), lambda i: (i, 0)),
        out_shape=jax.ShapeDtypeStruct((t, d), F32),
        scratch_shapes=[pltpu.VMEM((TOP_K, tm, d), F32), pltpu.SemaphoreType.DMA(())],
        compiler_params=pltpu.CompilerParams(
            dimension_semantics=("arbitrary",), vmem_limit_bytes=VMEM_LIMIT),
        name="combine",
    )(dest.reshape(nt, tm, TOP_K).transpose(0, 2, 1), h2, gates, ys, gfin)


def _pad_head_cols(w):
    d = w.shape[0]
    w = w.reshape(d, 2 * N_HEADS, HEAD_DIM)
    return jnp.pad(w, ((0, 0), (0, 0), (0, QK_PAD - HEAD_DIM))).reshape(d, 2 * N_HEADS * QK_PAD)


def _route_plan(meta, counts, t):
    ids = meta[:, :TOP_K]
    ranks = meta[:, TOP_K:2 * TOP_K]
    cnt = counts[0, :N_EXPERTS].astype(jnp.int32)
    padded = (cnt + EXP_BLK - 1) // EXP_BLK * EXP_BLK
    pad_ends = jnp.cumsum(padded)
    pad_starts = pad_ends - padded
    experts = jnp.arange(N_EXPERTS, dtype=jnp.int32)
    dest = jnp.sum(jnp.where(ids[..., None] == experts, pad_starts, 0), axis=-1) + ranks
    nb = (t * TOP_K + EXP_BLK - 1) // EXP_BLK + N_EXPERTS
    bstart = jnp.arange(nb, dtype=jnp.int32) * EXP_BLK
    block_e = jnp.minimum(jnp.sum((pad_ends[None, :] <= bstart[:, None]).astype(jnp.int32), axis=1),
                          N_EXPERTS - 1)
    row_end = jnp.sum(jnp.where(block_e[:, None] == experts, pad_starts + cnt, 0), axis=-1)
    nvalid = jnp.clip(row_end - bstart, 0, EXP_BLK)
    return dest.astype(jnp.int32), block_e.astype(jnp.int32), nvalid.astype(jnp.int32)


def kernel(x, norm1_g, w_in, lambda_q1, lambda_k1, lambda_q2, lambda_k2, subln_g, w_pool, pool_scale,
           w_out, norm2_g, w_group_router, b_group_router, w_expert_router, b_expert_router,
           w_gate, w_up, w_down, final_g):
    batch, seq, d = x.shape
    t = batch * seq
    qk_w = 2 * N_HEADS * HEAD_DIM
    av_w = N_HEADS * V_DIM
    l = 0
    assert seq % (2 * TK) == 0 and seq % TM_PROJ == 0 and t % TM_COMB == 0 and TK == TM_PROJ

    wi = w_in[l]
    wqt = _pad_head_cols(wi[:, :qk_w]).T.astype(BF16)
    wk = _pad_head_cols(wi[:, qk_w:2 * qk_w]).astype(BF16)
    wv = wi[:, 2 * qk_w:2 * qk_w + av_w].reshape(d, N_HEADS, V_DIM)
    wvt = jnp.pad(wv, ((0, 0), (0, 0), (0, VT_ROWS - V_DIM))).reshape(d, N_HEADS * VT_ROWS).T.astype(BF16)
    wu = wi[:, 2 * qk_w + av_w:].astype(BF16)

    qt, k, vt, u = _in_proj(x.reshape(t, d), norm1_g[l].reshape(1, d), wqt, wk, wvt, wu,
                            batch=batch, seq=seq)

    lam_in = jnp.zeros((8, LANES), F32)
    lam_in = lam_in.at[0, :HEAD_DIM].set(lambda_q1[l]).at[1, :HEAD_DIM].set(lambda_k1[l])
    lam_in = lam_in.at[2, :HEAD_DIM].set(lambda_q2[l]).at[3, :HEAD_DIM].set(lambda_k2[l])
    gsub = jnp.broadcast_to(subln_g[l].reshape(V_DIM, 1), (V_DIM, TQ)).astype(F32)
    a = _attention(lam_in, qt, k, vt, gsub, batch=batch, seq=seq)

    wr = jnp.zeros((d, 2 * LANES), F32)
    wr = wr.at[:, :N_GROUPS].set(w_group_router[l])
    wr = wr.at[:, LANES:LANES + N_EXPERTS].set(w_expert_router[l].reshape(d, N_EXPERTS)).astype(BF16)
    br = jnp.full((1, 2 * LANES), NEG, F32)
    br = br.at[0, :N_GROUPS].set(b_group_router[l])
    br = br.at[0, LANES:LANES + N_EXPERTS].set(b_expert_router[l].reshape(N_EXPERTS))
    h, xn2, gates, meta, counts = _mix_route(
        x, a, u.reshape(batch, seq, -1), w_pool[l].astype(BF16), pool_scale[l].reshape(1, -1),
        w_out[l].astype(BF16), norm2_g[l].reshape(1, d), wr, br, batch=batch, seq=seq)

    dest, block_e, nvalid = _route_plan(meta.reshape(t, LANES), counts, t)
    xs = _dispatch(dest, xn2.reshape(t, d), jnp.zeros((block_e.shape[0] * EXP_BLK, d), F32))
    ys = _experts(block_e, nvalid, xs, w_gate[l], w_up[l], w_down[l])
    out = _combine(dest, h.reshape(t, d), gates.reshape(t, LANES), ys, final_g.reshape(1, d))
    return out.reshape(batch, seq, d)
```

```python
import functools
import math

import jax
import jax.numpy as jnp
from jax import lax
from jax.experimental import pallas as pl
from jax.experimental.pallas import tpu as pltpu

F32 = jnp.float32
BF16 = jnp.bfloat16

N_HEADS = 4
HEAD_DIM = 64
V_DIM = 128
POOL_WINDOWS = (2, 4, 8, 16)
N_GROUPS = 4
E_PER_GROUP = 8
N_EXPERTS = N_GROUPS * E_PER_GROUP
TOP_K = 2
EPS = 1e-6
LAM_INIT = 0.8 - 0.6 * math.exp(-0.3 * 0)
NEG = -1e30

LANES = 128
QK_PAD = 128
VT_ROWS = 144
POS_SPLIT = 128

TM_PROJ = 512
TQ = 512
TK = 512
ATTN_UNROLL = 6
EXP_BLK = 256
TM_COMB = 256
DMA_UNROLL = 8
VMEM_LIMIT = 56 * 1024 * 1024


def _nt_dot(a, b):
    return lax.dot_general(a, b, (((1,), (1,)), ((), ())), preferred_element_type=F32)


def _in_proj_kernel(x_ref, g_ref, wqt_ref, wk_ref, wvt_ref, wu_ref,
                    qt_ref, k_ref, vt_ref, u_ref, *, seq):
    tm = x_ref.shape[0]
    i = pl.program_id(0)
    j0 = (i % (seq // tm)) * tm
    x = x_ref[...]
    ms = jnp.mean(x * x, axis=-1, keepdims=True)
    hn = (x * lax.rsqrt(ms + EPS) * g_ref[...]).astype(BF16)

    qt = _nt_dot(wqt_ref[...], hn) * (1.0 / math.sqrt(HEAD_DIM))
    for g in range(2 * N_HEADS):
        qt_ref[0, g] = qt[g * QK_PAD:(g + 1) * QK_PAD].astype(BF16)

    kk = jnp.dot(hn, wk_ref[...], preferred_element_type=F32)
    pos = j0 + lax.broadcasted_iota(jnp.int32, (tm, QK_PAD), 0)
    lane = lax.broadcasted_iota(jnp.int32, (tm, QK_PAD), 1)
    hi = (pos & ~(POS_SPLIT - 1)).astype(F32)
    lo = (pos & (POS_SPLIT - 1)).astype(F32)
    feat = jnp.where(lane == HEAD_DIM + 2, hi, jnp.where(lane == HEAD_DIM + 3, lo, 0.0))
    feat = jnp.where((lane == HEAD_DIM) | (lane == HEAD_DIM + 1), 1.0, feat)
    for g in range(2 * N_HEADS):
        k_ref[0, g] = (kk[:, g * QK_PAD:(g + 1) * QK_PAD] + feat).astype(BF16)

    vt = _nt_dot(wvt_ref[...], hn)
    row = lax.broadcasted_iota(jnp.int32, (VT_ROWS, tm), 0)
    for h in range(N_HEADS):
        blk = vt[h * VT_ROWS:(h + 1) * VT_ROWS]
        vt_ref[0, h, 0] = jnp.where(row == V_DIM, 1.0, blk).astype(BF16)

    u_ref[...] = jnp.dot(hn, wu_ref[...], preferred_element_type=F32)


def _in_proj(x2, g1, wqt, wk, wvt, wu, *, batch, seq):
    t, d = x2.shape
    tm = TM_PROJ
    nblk = seq // tm
    kern = functools.partial(_in_proj_kernel, seq=seq)
    full = lambda shape: pl.BlockSpec(shape, lambda i: (0,) * len(shape))
    return pl.pallas_call(
        kern,
        grid=(t // tm,),
        in_specs=[
            pl.BlockSpec((tm, d), lambda i: (i, 0)),
            full(g1.shape), full(wqt.shape), full(wk.shape), full(wvt.shape), full(wu.shape),
        ],
        out_specs=[
            pl.BlockSpec((1, 2 * N_HEADS, QK_PAD, tm), lambda i: (i // nblk, 0, 0, i % nblk)),
            pl.BlockSpec((1, 2 * N_HEADS, tm, QK_PAD), lambda i: (i // nblk, 0, i % nblk, 0)),
            pl.BlockSpec((1, N_HEADS, 1, VT_ROWS, tm), lambda i: (i // nblk, 0, i % nblk, 0, 0)),
            pl.BlockSpec((tm, wu.shape[1]), lambda i: (i, 0)),
        ],
        out_shape=[
            jax.ShapeDtypeStruct((batch, 2 * N_HEADS, QK_PAD, seq), BF16),
            jax.ShapeDtypeStruct((batch, 2 * N_HEADS, seq, QK_PAD), BF16),
            jax.ShapeDtypeStruct((batch, N_HEADS, nblk, VT_ROWS, tm), BF16),
            jax.ShapeDtypeStruct((t, wu.shape[1]), F32),
        ],
        compiler_params=pltpu.CompilerParams(
            dimension_semantics=("arbitrary",), vmem_limit_bytes=VMEM_LIMIT),
        name="in_proj",
    )(x2, g1, wqt, wk, wvt, wu)


def _attention_kernel(lam_ref, qt_ref, k_ref, vt_ref, g_ref, o_ref,
                      qv_ref, acc_ref, m_ref, sa_ref, sb_ref, mba_ref, mbb_ref):
    h = pl.program_id(1)
    qi = pl.program_id(2)
    tq = qt_ref.shape[-1]
    nkb, _, tk = vt_ref.shape[2:]
    jdiag = (qi * tq) // tk

    slope = jnp.where(h == 0, 2.0 ** -2, jnp.where(h == 1, 2.0 ** -4,
                      jnp.where(h == 2, 2.0 ** -6, 2.0 ** -8))).astype(F32)

    lv = lam_ref[...]
    lam = (jnp.exp(jnp.sum(lv[0:1] * lv[1:2], axis=-1, keepdims=True))
           - jnp.exp(jnp.sum(lv[2:3] * lv[3:4], axis=-1, keepdims=True)) + LAM_INIT)

    r = lax.broadcasted_iota(jnp.int32, (QK_PAD, tq), 0)
    ipos = qi * tq + lax.broadcasted_iota(jnp.int32, (QK_PAD, tq), 1)
    ihi = (ipos & ~(POS_SPLIT - 1)).astype(F32)
    ilo = (ipos & (POS_SPLIT - 1)).astype(F32)
    fi = jnp.where(r == HEAD_DIM, ihi * slope, jnp.where(r == HEAD_DIM + 1, ilo * slope, 0.0))
    fj = jnp.where((r == HEAD_DIM + 2) | (r == HEAD_DIM + 3), slope, 0.0)
    for c in range(2):
        q = qt_ref[0, c].astype(F32)
        qv_ref[0, c] = (q + fi - fj).astype(BF16)
        qv_ref[1, c] = (q - fi + fj).astype(BF16)
        qv_ref[2, c] = qt_ref[0, c]

    acc_ref[...] = jnp.zeros(acc_ref.shape, F32)
    m_ref[...] = jnp.full(m_ref.shape, NEG, F32)

    def score_stage(kb, variant, s_ref, mb_ref, bias=None):
        for c in range(2):
            kblk = k_ref[0, c, pl.ds(pl.multiple_of(kb * tk, tk), tk), :]
            st = jnp.dot(kblk, qv_ref[variant, c], preferred_element_type=F32)
            if bias is not None:
                st = st + bias
            s_ref[c] = st
            mb_ref[c] = jnp.max(st, axis=0, keepdims=True)

    def softmax_stage(kb, s_ref, mb_ref):
        for c in range(2):
            m_old = m_ref[c]
            m_new = jnp.maximum(m_old, mb_ref[c])
            alpha = jnp.exp(m_old - m_new)
            p = jnp.exp(s_ref[c] - m_new).astype(BF16)
            pv = jnp.dot(vt_ref[0, 0, kb], p, preferred_element_type=F32)
            acc_ref[c] = acc_ref[c] * alpha + pv
            m_ref[c] = m_new

    def visit_block(n):
        o = n - 1
        return jnp.where(n == 0, jdiag, o + (o >= jdiag).astype(jnp.int32))

    def visit_variant(kb):
        return jnp.where(kb < jdiag, 1, 0)

    jpos = jdiag * tk + lax.broadcasted_iota(jnp.int32, (tk, tq), 0)
    iposd = qi * tq + lax.broadcasted_iota(jnp.int32, (tk, tq), 1)
    bias = -slope * jnp.abs(iposd - jpos).astype(F32)
    score_stage(jdiag, 2, sa_ref, mba_ref, bias)

    bufs = ((sa_ref, mba_ref), (sb_ref, mbb_ref))

    def phase(n, parity, with_scores):
        if with_scores:
            kb_next = visit_block(n + 1)
            score_stage(kb_next, visit_variant(kb_next), *bufs[1 - parity])
        softmax_stage(visit_block(n), *bufs[parity])

    def loop_body(t, carry):
        for j in range(ATTN_UNROLL):
            phase(t * ATTN_UNROLL + j, j % 2, True)
        return carry

    trips = (nkb - 1) // ATTN_UNROLL
    lax.fori_loop(0, trips, loop_body, 0)
    for n in range(trips * ATTN_UNROLL, nkb):
        phase(n, n % 2, n + 1 < nkb)

    a0 = acc_ref[0]
    a1 = acc_ref[1]
    o = (a0[:V_DIM] * (1.0 / a0[V_DIM:V_DIM + 1])
         - lam * (a1[:V_DIM] * (1.0 / a1[V_DIM:V_DIM + 1])))
    ms = jnp.mean(o * o, axis=0, keepdims=True)
    y = o * lax.rsqrt(ms + EPS) * g_ref[...] * (1.0 - LAM_INIT)
    o_ref[0] = y.T.astype(BF16)


def _attention(lam_in, qt, k, vt, gsub, *, batch, seq):
    nkb = seq // TK
    return pl.pallas_call(
        _attention_kernel,
        grid=(batch, N_HEADS, seq // TQ),
        in_specs=[
            pl.BlockSpec(lam_in.shape, lambda b, h, i: (0, 0)),
            pl.BlockSpec((1, 2, QK_PAD, TQ), lambda b, h, i: (b, h, 0, i)),
            pl.BlockSpec((1, 2, seq, QK_PAD), lambda b, h, i: (b, h, 0, 0)),
            pl.BlockSpec((1, 1, nkb, VT_ROWS, TK), lambda b, h, i: (b, h, 0, 0, 0)),
            pl.BlockSpec(gsub.shape, lambda b, h, i: (0, 0)),
        ],
        out_specs=pl.BlockSpec((1, TQ, V_DIM), lambda b, h, i: (b, i, h)),
        out_shape=jax.ShapeDtypeStruct((batch, seq, N_HEADS * V_DIM), BF16),
        scratch_shapes=[
            pltpu.VMEM((3, 2, QK_PAD, TQ), BF16),
            pltpu.VMEM((2, VT_ROWS, TQ), F32),
            pltpu.VMEM((2, 1, TQ), F32),
            pltpu.VMEM((2, TK, TQ), F32),
            pltpu.VMEM((2, TK, TQ), F32),
            pltpu.VMEM((2, 1, TQ), F32),
            pltpu.VMEM((2, 1, TQ), F32),
        ],
        compiler_params=pltpu.CompilerParams(
            dimension_semantics=("arbitrary", "arbitrary", "arbitrary"),
            vmem_limit_bytes=VMEM_LIMIT),
        name="diff_attention",
    )(lam_in, qt, k, vt, gsub)


def _mix_route_kernel(x_ref, a_ref, u_ref, up_ref, un_ref, wp_ref, ps_ref, wo_ref, g2_ref,
                      wr_ref, br_ref, h_ref, xn_ref, gate_ref, id_ref, count_ref, cnt_ref, *, seq):
    tm = x_ref.shape[1]
    j0 = pl.program_id(1) * tm
    halo = up_ref.shape[1]

    ext = jnp.concatenate([up_ref[0], u_ref[0], un_ref[0]], axis=0)
    epos = j0 - halo + lax.broadcasted_iota(jnp.int32, (tm + 2 * halo, 1), 0)
    ext = jnp.where((epos >= 0) & (epos < seq), ext, 0.0)
    tpos = j0 + lax.broadcasted_iota(jnp.int32, (tm, 1), 0)
    ys = []
    for g, w in enumerate(POOL_WINDOWS):
        e = ext[:, g * LANES:(g + 1) * LANES]
        n = e.shape[0]
        s = e[1:] + e[:-1]
        width, start = 2, 0
        while width < w:
            s = s[:-width] + s[width:]
            width *= 2
        off = halo - w // 2
        win = s[off:off + tm]
        cnt = (jnp.minimum(tpos + w // 2, seq) - jnp.maximum(tpos - w // 2, 0)).astype(F32)
        d = (win / cnt - u_ref[0][:, g * LANES:(g + 1) * LANES]).astype(BF16)
        yg = jnp.dot(d, wp_ref[g], preferred_element_type=F32)
        ys.append(yg * ps_ref[:, g * LANES:(g + 1) * LANES])
        del n, start
    p = jnp.concatenate(ys, axis=-1).astype(BF16)

    mix = jnp.concatenate([a_ref[0], p], axis=-1)
    hcur = x_ref[0] + jnp.dot(mix, wo_ref[...], preferred_element_type=F32)
    h_ref[0] = hcur
    ms = jnp.mean(hcur * hcur, axis=-1, keepdims=True)
    xn = hcur * lax.rsqrt(ms + EPS) * g2_ref[...]
    xn_ref[0] = xn

    logits = jnp.dot(xn.astype(BF16), wr_ref[...], preferred_element_type=F32) + br_ref[...]
    gl = logits[:, :LANES]
    el = logits[:, LANES:]
    lane = lax.broadcasted_iota(jnp.int32, (tm, LANES), 1)
    gmax = jnp.max(gl, axis=-1, keepdims=True)
    gsel = jnp.min(jnp.where(gl == gmax, lane, LANES), axis=-1, keepdims=True)
    pg = 1.0 / jnp.sum(jnp.exp(gl - gmax), axis=-1, keepdims=True)
    elm = jnp.where((lane >> 3) == gsel, el, NEG)
    m1 = jnp.max(elm, axis=-1, keepdims=True)
    i1 = jnp.min(jnp.where(elm == m1, lane, LANES), axis=-1, keepdims=True)
    elm2 = jnp.where(lane == i1, NEG, elm)
    m2 = jnp.max(elm2, axis=-1, keepdims=True)
    i2 = jnp.min(jnp.where(elm2 == m2, lane, LANES), axis=-1, keepdims=True)
    e2 = jnp.exp(m2 - m1)
    t1 = 1.0 / (1.0 + e2)
    gate_ref[0] = jnp.where(lane == 0, pg * t1, jnp.where(lane == 1, pg * (e2 * t1), 0.0))

    @pl.when((pl.program_id(0) == 0) & (pl.program_id(1) == 0))
    def _():
        cnt_ref[...] = jnp.zeros(cnt_ref.shape, F32)

    hit1 = lane == i1
    hit2 = lane == i2
    onehot = jnp.where(hit1 | hit2, 1.0, 0.0)
    rr = lax.broadcasted_iota(jnp.int32, (tm, tm), 0)
    cc = lax.broadcasted_iota(jnp.int32, (tm, tm), 1)
    tril = jnp.where(cc < rr, 1.0, 0.0).astype(BF16)
    before = jnp.dot(tril, onehot.astype(BF16), preferred_element_type=F32) + cnt_ref[...]
    r1 = jnp.sum(jnp.where(hit1, before, 0.0), axis=-1, keepdims=True).astype(jnp.int32)
    r2 = jnp.sum(jnp.where(hit2, before, 0.0), axis=-1, keepdims=True).astype(jnp.int32)
    cnt_ref[...] = cnt_ref[...] + jnp.sum(onehot, axis=0, keepdims=True)
    count_ref[...] = cnt_ref[...]
    id_ref[0] = jnp.where(lane == 0, i1, jnp.where(lane == 1, i2,
                          jnp.where(lane == 2, r1, jnp.where(lane == 3, r2, 0))))


def _mix_route(x, a, u, wp, ps, wo, g2, wr, br, *, batch, seq):
    tm = TM_PROJ
    d = x.shape[-1]
    halo = 8
    nh = tm // halo
    kern = functools.partial(_mix_route_kernel, seq=seq)
    full = lambda arr: pl.BlockSpec(arr.shape, lambda b, i: (0,) * arr.ndim)
    tile = lambda w: pl.BlockSpec((1, tm, w), lambda b, i: (b, i, 0))
    return pl.pallas_call(
        kern,
        grid=(batch, seq // tm),
        in_specs=[
            tile(d), tile(a.shape[-1]), tile(u.shape[-1]),
            pl.BlockSpec((1, halo, u.shape[-1]), lambda b, i: (b, jnp.maximum(i * nh - 1, 0), 0)),
            pl.BlockSpec((1, halo, u.shape[-1]),
                         lambda b, i: (b, jnp.minimum((i + 1) * nh, seq // halo - 1), 0)),
            full(wp), full(ps), full(wo), full(g2), full(wr), full(br),
        ],
        out_specs=[tile(d), tile(d), tile(LANES), tile(LANES),
                   pl.BlockSpec((1, LANES), lambda b, i: (0, 0))],
        out_shape=[
            jax.ShapeDtypeStruct((batch, seq, d), F32),
            jax.ShapeDtypeStruct((batch, seq, d), F32),
            jax.ShapeDtypeStruct((batch, seq, LANES), F32),
            jax.ShapeDtypeStruct((batch, seq, LANES), jnp.int32),
            jax.ShapeDtypeStruct((1, LANES), F32),
        ],
        scratch_shapes=[pltpu.VMEM((1, LANES), F32)],
        compiler_params=pltpu.CompilerParams(
            dimension_semantics=("arbitrary", "arbitrary"), vmem_limit_bytes=VMEM_LIMIT),
        name="mix_route",
    )(x, a, u, u, u, wp, ps, wo, g2, wr, br)


def _dispatch_kernel(dest_ref, x_ref, xs_in, xs_hbm, sem):
    del xs_in
    tm = x_ref.shape[0]

    def group(gi, carry):
        for j in range(DMA_UNROLL):
            r = gi * DMA_UNROLL + j
            for k in range(TOP_K):
                pltpu.make_async_copy(x_ref.at[r], xs_hbm.at[dest_ref[0, k, r]], sem).start()
        return carry

    lax.fori_loop(0, tm // DMA_UNROLL, group, 0)
    for k in range(TOP_K):
        pltpu.make_async_copy(x_ref, xs_hbm.at[pl.ds(0, tm)], sem).wait()


def _dispatch(dest, xn2, xs_zero):
    t, d = xn2.shape
    tm = TM_COMB
    nt = t // tm
    return pl.pallas_call(
        _dispatch_kernel,
        grid=(nt,),
        in_specs=[
            pl.BlockSpec((1, TOP_K, tm), lambda i: (i, 0, 0), memory_space=pltpu.SMEM),
            pl.BlockSpec((tm, d), lambda i: (i, 0)),
            pl.BlockSpec(memory_space=pl.ANY),
        ],
        out_specs=pl.BlockSpec(memory_space=pl.ANY),
        out_shape=jax.ShapeDtypeStruct(xs_zero.shape, xs_zero.dtype),
        input_output_aliases={2: 0},
        scratch_shapes=[pltpu.SemaphoreType.DMA(())],
        compiler_params=pltpu.CompilerParams(
            dimension_semantics=("arbitrary",), vmem_limit_bytes=VMEM_LIMIT),
        name="dispatch",
    )(dest.reshape(nt, tm, TOP_K).transpose(0, 2, 1), xn2, xs_zero)


def _experts_kernel(be_ref, nv_ref, xs_ref, wg_ref, wu_ref, wd_ref, y_ref, wgb, wub, wdb):
    i = pl.program_id(0)
    changed = jnp.logical_or(i == 0, be_ref[i] != be_ref[jnp.maximum(i - 1, 0)])

    @pl.when(changed)
    def _():
        wgb[...] = wg_ref[0].astype(BF16)
        wub[...] = wu_ref[0].astype(BF16)
        wdb[...] = wd_ref[0].astype(BF16)

    @pl.when(nv_ref[i] > 0)
    def _():
        xb = xs_ref[...].astype(BF16)
        gt = jnp.dot(xb, wgb[...], preferred_element_type=F32)
        up = jnp.dot(xb, wub[...], preferred_element_type=F32)
        hid = (gt * (1.0 / (1.0 + jnp.exp(-gt))) * up).astype(BF16)
        y_ref[...] = jnp.dot(hid, wdb[...], preferred_element_type=F32)

    @pl.when(nv_ref[i] == 0)
    def _():
        y_ref[...] = jnp.zeros(y_ref.shape, F32)


def _experts(block_e, nvalid, xs, w_gate, w_up, w_down):
    nb = block_e.shape[0]
    d = xs.shape[-1]
    de = w_gate.shape[-1]
    grid_spec = pltpu.PrefetchScalarGridSpec(
        num_scalar_prefetch=2,
        grid=(nb,),
        in_specs=[
            pl.BlockSpec((EXP_BLK, d), lambda i, be, nv: (i, 0)),
            pl.BlockSpec((1, d, de), lambda i, be, nv: (be[i], 0, 0)),
            pl.BlockSpec((1, d, de), lambda i, be, nv: (be[i], 0, 0)),
            pl.BlockSpec((1, de, d), lambda i, be, nv: (be[i], 0, 0)),
        ],
        out_specs=pl.BlockSpec((EXP_BLK, d), lambda i, be, nv: (i, 0)),
        scratch_shapes=[pltpu.VMEM((d, de), BF16), pltpu.VMEM((d, de), BF16), pltpu.VMEM((de, d), BF16)],
    )
    return pl.pallas_call(
        _experts_kernel,
        grid_spec=grid_spec,
        out_shape=jax.ShapeDtypeStruct((nb * EXP_BLK, d), F32),
        compiler_params=pltpu.CompilerParams(
            dimension_semantics=("arbitrary",), vmem_limit_bytes=VMEM_LIMIT),
        name="experts",
    )(block_e, nvalid, xs, w_gate, w_up, w_down)


def _combine_kernel(dest_ref, h_ref, gate_ref, y_hbm, g_ref, o_ref, ybuf, sem):
    tm = h_ref.shape[0]

    def group(gi, carry):
        for j in range(DMA_UNROLL):
            r = gi * DMA_UNROLL + j
            for k in range(TOP_K):
                pltpu.make_async_copy(y_hbm.at[dest_ref[0, k, r]], ybuf.at[k, r], sem).start()
        return carry

    lax.fori_loop(0, tm // DMA_UNROLL, group, 0)
    for k in range(TOP_K):
        pltpu.make_async_copy(y_hbm.at[pl.ds(0, tm)], ybuf.at[k], sem).wait()
    gates = gate_ref[...]
    hcur = h_ref[...] + (gates[:, 0:1] * ybuf[0] + gates[:, 1:2] * ybuf[1])
    ms = jnp.mean(hcur * hcur, axis=-1, keepdims=True)
    o_ref[...] = hcur * lax.rsqrt(ms + EPS) * g_ref[...]


def _combine(dest, h2, gates, ys, gfin):
    t, d = h2.shape
    tm = TM_COMB
    nt = t // tm
    return pl.pallas_call(
        _combine_kernel,
        grid=(nt,),
        in_specs=[
            pl.BlockSpec((1, TOP_K, tm), lambda i: (i, 0, 0), memory_space=pltpu.SMEM),
            pl.BlockSpec((tm, d), lambda i: (i, 0)),
            pl.BlockSpec((tm, LANES), lambda i: (i, 0)),
            pl.BlockSpec(memory_space=pl.ANY),
            pl.BlockSpec(gfin.shape, lambda i: (0, 0)),
        ],
        out_specs=pl.BlockSpec((tm, d), lambda i: (i, 0)),
        out_shape=jax.ShapeDtypeStruct((t, d), F32),
        scratch_shapes=[pltpu.VMEM((TOP_K, tm, d), F32), pltpu.SemaphoreType.DMA(())],
        compiler_params=pltpu.CompilerParams(
            dimension_semantics=("arbitrary",), vmem_limit_bytes=VMEM_LIMIT),
        name="combine",
    )(dest.reshape(nt, tm, TOP_K).transpose(0, 2, 1), h2, gates, ys, gfin)


def _pad_head_cols(w):
    d = w.shape[0]
    w = w.reshape(d, 2 * N_HEADS, HEAD_DIM)
    return jnp.pad(w, ((0, 0), (0, 0), (0, QK_PAD - HEAD_DIM))).reshape(d, 2 * N_HEADS * QK_PAD)


def _route_plan(meta, counts, t):
    ids = meta[:, :TOP_K]
    ranks = meta[:, TOP_K:2 * TOP_K]
    cnt = counts[0, :N_EXPERTS].astype(jnp.int32)
    padded = (cnt + EXP_BLK - 1) // EXP_BLK * EXP_BLK
    pad_ends = jnp.cumsum(padded)
    pad_starts = pad_ends - padded
    experts = jnp.arange(N_EXPERTS, dtype=jnp.int32)
    dest = jnp.sum(jnp.where(ids[..., None] == experts, pad_starts, 0), axis=-1) + ranks
    nb = (t * TOP_K + EXP_BLK - 1) // EXP_BLK + N_EXPERTS
    bstart = jnp.arange(nb, dtype=jnp.int32) * EXP_BLK
    block_e = jnp.minimum(jnp.sum((pad_ends[None, :] <= bstart[:, None]).astype(jnp.int32), axis=1),
                          N_EXPERTS - 1)
    row_end = jnp.sum(jnp.where(block_e[:, None] == experts, pad_starts + cnt, 0), axis=-1)
    nvalid = jnp.clip(row_end - bstart, 0, EXP_BLK)
    return dest.astype(jnp.int32), block_e.astype(jnp.int32), nvalid.astype(jnp.int32)


def kernel(x, norm1_g, w_in, lambda_q1, lambda_k1, lambda_q2, lambda_k2, subln_g, w_pool, pool_scale,
           w_out, norm2_g, w_group_router, b_group_router, w_expert_router, b_expert_router,
           w_gate, w_up, w_down, final_g):
    batch, seq, d = x.shape
    t = batch * seq
    qk_w = 2 * N_HEADS * HEAD_DIM
    av_w = N_HEADS * V_DIM
    l = 0
    assert seq % (2 * TK) == 0 and seq % TM_PROJ == 0 and t % TM_COMB == 0 and TK == TM_PROJ

    wi = w_in[l]
    wqt = _pad_head_cols(wi[:, :qk_w]).T.astype(BF16)
    wk = _pad_head_cols(wi[:, qk_w:2 * qk_w]).astype(BF16)
    wv = wi[:, 2 * qk_w:2 * qk_w + av_w].reshape(d, N_HEADS, V_DIM)
    wvt = jnp.pad(wv, ((0, 0), (0, 0), (0, VT_ROWS - V_DIM))).reshape(d, N_HEADS * VT_ROWS).T.astype(BF16)
    wu = wi[:, 2 * qk_w + av_w:].astype(BF16)

    qt, k, vt, u = _in_proj(x.reshape(t, d), norm1_g[l].reshape(1, d), wqt, wk, wvt, wu,
                            batch=batch, seq=seq)

    lam_in = jnp.zeros((8, LANES), F32)
    lam_in = lam_in.at[0, :HEAD_DIM].set(lambda_q1[l]).at[1, :HEAD_DIM].set(lambda_k1[l])
    lam_in = lam_in.at[2, :HEAD_DIM].set(lambda_q2[l]).at[3, :HEAD_DIM].set(lambda_k2[l])
    gsub = jnp.broadcast_to(subln_g[l].reshape(V_DIM, 1), (V_DIM, TQ)).astype(F32)
    a = _attention(lam_in, qt, k, vt, gsub, batch=batch, seq=seq)

    wr = jnp.zeros((d, 2 * LANES), F32)
    wr = wr.at[:, :N_GROUPS].set(w_group_router[l])
    wr = wr.at[:, LANES:LANES + N_EXPERTS].set(w_expert_router[l].reshape(d, N_EXPERTS)).astype(BF16)
    br = jnp.full((1, 2 * LANES), NEG, F32)
    br = br.at[0, :N_GROUPS].set(b_group_router[l])
    br = br.at[0, LANES:LANES + N_EXPERTS].set(b_expert_router[l].reshape(N_EXPERTS))
    h, xn2, gates, meta, counts = _mix_route(
        x, a, u.reshape(batch, seq, -1), w_pool[l].astype(BF16), pool_scale[l].reshape(1, -1),
        w_out[l].astype(BF16), norm2_g[l].reshape(1, d), wr, br, batch=batch, seq=seq)

    dest, block_e, nvalid = _route_plan(meta.reshape(t, LANES), counts, t)
    xs = _dispatch(dest, xn2.reshape(t, d), jnp.zeros((block_e.shape[0] * EXP_BLK, d), F32))
    ys = _experts(block_e, nvalid, xs, w_gate[l], w_up[l], w_down[l])
    out = _combine(dest, h.reshape(t, d), gates.reshape(t, LANES), ys, final_g.reshape(1, d))
    return out.reshape(batch, seq, d)
```

```python
import functools
import math

import jax
import jax.numpy as jnp
import numpy as np
from jax import lax
from jax.experimental import pallas as pl
from jax.experimental.pallas import tpu as pltpu

F32 = jnp.float32
BF16 = jnp.bfloat16

N_HEADS = 4
HEAD_DIM = 64
V_DIM = 128
POOL_WINDOWS = (2, 4, 8, 16)
N_GROUPS = 4
E_PER_GROUP = 8
N_EXPERTS = N_GROUPS * E_PER_GROUP
TOP_K = 2
EPS = 1e-6
LAM_INIT = 0.8 - 0.6 * math.exp(-0.3 * 0)
NEG = -1e30
LOG2E = math.log2(math.e)


def _bf16_terms(value, n):
    terms, rest = [], value
    for _ in range(n):
        term = float(np.asarray(rest, dtype=jnp.bfloat16))
        terms.append(term)
        rest -= term
    return tuple(terms)


LOG2E_TERMS = _bf16_terms(LOG2E, 3)
N_TERMS = len(LOG2E_TERMS)
SLOPES = tuple(2.0 ** (-8.0 * (h + 1) / N_HEADS) for h in range(N_HEADS))

LANES = 128
QK_PAD = 128
VT_ROWS = 144
POS_SPLIT = 128

TM_PROJ = 512
TQ = 512
TK = 512
ATTN_UNROLL = 6
EXP_BLK = 256
TM_COMB = 256
DMA_UNROLL = 8
VMEM_LIMIT = 56 * 1024 * 1024


def _nt_dot(a, b):
    return lax.dot_general(a, b, (((1,), (1,)), ((), ())), preferred_element_type=F32)


def _in_proj_kernel(x_ref, g_ref, wqt_ref, wk_ref, wvt_ref, wu_ref, kc_ref,
                    qt_ref, k_ref, vt_ref, u_ref, *, seq):
    tm = x_ref.shape[0]
    i = pl.program_id(0)
    j0 = (i % (seq // tm)) * tm
    x = x_ref[...]
    ms = jnp.mean(x * x, axis=-1, keepdims=True)
    hn = (x * lax.rsqrt(ms + EPS) * g_ref[...]).astype(BF16)

    qt = _nt_dot(wqt_ref[...], hn) * (LOG2E / math.sqrt(HEAD_DIM))
    for g in range(2 * N_HEADS):
        qt_ref[0, g] = qt[g * QK_PAD:(g + 1) * QK_PAD].astype(BF16)

    kk = jnp.dot(hn, wk_ref[...], preferred_element_type=F32)
    pos = j0 + lax.broadcasted_iota(jnp.int32, (tm, QK_PAD), 0)
    lane = lax.broadcasted_iota(jnp.int32, (tm, QK_PAD), 1)
    hi = (pos & ~(POS_SPLIT - 1)).astype(F32)
    lo = (pos & (POS_SPLIT - 1)).astype(F32)
    f_hi, f_lo = HEAD_DIM + 2 * N_TERMS, HEAD_DIM + 3 * N_TERMS
    feat = jnp.where((lane >= f_hi) & (lane < f_lo), hi,
                     jnp.where((lane >= f_lo) & (lane < f_lo + N_TERMS), lo, 0.0))
    for g in range(2 * N_HEADS):
        k_ref[0, g] = (kk[:, g * QK_PAD:(g + 1) * QK_PAD] + (feat + kc_ref[g:g + 1, :])).astype(BF16)

    vt = _nt_dot(wvt_ref[...], hn)
    row = lax.broadcasted_iota(jnp.int32, (VT_ROWS, tm), 0)
    for h in range(N_HEADS):
        blk = vt[h * VT_ROWS:(h + 1) * VT_ROWS]
        vt_ref[0, h, 0] = jnp.where(row == V_DIM, 1.0, blk).astype(BF16)

    u_ref[...] = jnp.dot(hn, wu_ref[...], preferred_element_type=F32)


def _key_feature_consts():
    kc = np.zeros((2 * N_HEADS, QK_PAD), np.float32)
    for g in range(2 * N_HEADS):
        for rep in range(2):
            for n, term in enumerate(LOG2E_TERMS):
                kc[g, HEAD_DIM + rep * N_TERMS + n] = term * SLOPES[g // 2]
    return jnp.asarray(kc)


def _in_proj(x2, g1, wqt, wk, wvt, wu, *, batch, seq):
    t, d = x2.shape
    tm = TM_PROJ
    nblk = seq // tm
    kern = functools.partial(_in_proj_kernel, seq=seq)
    full = lambda shape: pl.BlockSpec(shape, lambda i: (0,) * len(shape))
    kc = _key_feature_consts()
    return pl.pallas_call(
        kern,
        grid=(t // tm,),
        in_specs=[
            pl.BlockSpec((tm, d), lambda i: (i, 0)),
            full(g1.shape), full(wqt.shape), full(wk.shape), full(wvt.shape), full(wu.shape),
            full(kc.shape),
        ],
        out_specs=[
            pl.BlockSpec((1, 2 * N_HEADS, QK_PAD, tm), lambda i: (i // nblk, 0, 0, i % nblk)),
            pl.BlockSpec((1, 2 * N_HEADS, tm, QK_PAD), lambda i: (i // nblk, 0, i % nblk, 0)),
            pl.BlockSpec((1, N_HEADS, 1, VT_ROWS, tm), lambda i: (i // nblk, 0, i % nblk, 0, 0)),
            pl.BlockSpec((tm, wu.shape[1]), lambda i: (i, 0)),
        ],
        out_shape=[
            jax.ShapeDtypeStruct((batch, 2 * N_HEADS, QK_PAD, seq), BF16),
            jax.ShapeDtypeStruct((batch, 2 * N_HEADS, seq, QK_PAD), BF16),
            jax.ShapeDtypeStruct((batch, N_HEADS, nblk, VT_ROWS, tm), BF16),
            jax.ShapeDtypeStruct((t, wu.shape[1]), F32),
        ],
        compiler_params=pltpu.CompilerParams(
            dimension_semantics=("arbitrary",), vmem_limit_bytes=VMEM_LIMIT),
        name="in_proj",
    )(x2, g1, wqt, wk, wvt, wu, kc)


def _attention_kernel(lam_ref, qt_ref, k_ref, vt_ref, g_ref, o_ref,
                      qv_ref, acc_ref, m_ref, sa_ref, sb_ref, mba_ref, mbb_ref):
    h = pl.program_id(1)
    qi = pl.program_id(2)
    tq = qt_ref.shape[-1]
    nkb, _, tk = vt_ref.shape[2:]
    jdiag = (qi * tq) // tk

    slope = jnp.where(h == 0, SLOPES[0], jnp.where(h == 1, SLOPES[1],
                      jnp.where(h == 2, SLOPES[2], SLOPES[3]))).astype(F32)

    lv = lam_ref[...]
    lam = (jnp.exp(jnp.sum(lv[0:1] * lv[1:2], axis=-1, keepdims=True))
           - jnp.exp(jnp.sum(lv[2:3] * lv[3:4], axis=-1, keepdims=True)) + LAM_INIT)

    r = lax.broadcasted_iota(jnp.int32, (QK_PAD, tq), 0)
    ipos = qi * tq + lax.broadcasted_iota(jnp.int32, (QK_PAD, tq), 1)
    ihi = (ipos & ~(POS_SPLIT - 1)).astype(F32)
    ilo = (ipos & (POS_SPLIT - 1)).astype(F32)
    f0 = HEAD_DIM
    fi = jnp.where((r >= f0) & (r < f0 + N_TERMS), ihi,
                   jnp.where((r >= f0 + N_TERMS) & (r < f0 + 2 * N_TERMS), ilo, 0.0))
    fj = jnp.zeros((QK_PAD, tq), F32)
    for rep in range(2):
        for n, term in enumerate(LOG2E_TERMS):
            fj = jnp.where(r == f0 + (2 + rep) * N_TERMS + n, term * slope, fj)
    for c in range(2):
        q = qt_ref[0, c].astype(F32)
        qv_ref[0, c] = (q + fi - fj).astype(BF16)
        qv_ref[1, c] = (q - fi + fj).astype(BF16)
        qv_ref[2, c] = qt_ref[0, c]

    acc_ref[...] = jnp.zeros(acc_ref.shape, F32)
    m_ref[...] = jnp.full(m_ref.shape, NEG, F32)

    def score_stage(kb, variant, s_ref, mb_ref, bias=None):
        for c in range(2):
            kblk = k_ref[0, c, pl.ds(pl.multiple_of(kb * tk, tk), tk), :]
            st = jnp.dot(kblk, qv_ref[variant, c], preferred_element_type=F32)
            if bias is not None:
                st = st + bias
            s_ref[c] = st
            mb_ref[c] = jnp.max(st, axis=0, keepdims=True)

    def softmax_stage(kb, s_ref, mb_ref):
        for c in range(2):
            m_old = m_ref[c]
            m_new = jnp.maximum(m_old, mb_ref[c])
            alpha = jnp.exp2(m_old - m_new)
            p = jnp.exp2(s_ref[c] - m_new).astype(BF16)
            pv = jnp.dot(vt_ref[0, 0, kb], p, preferred_element_type=F32)
            acc_ref[c] = acc_ref[c] * alpha + pv
            m_ref[c] = m_new

    def visit_block(n):
        o = n - 1
        return jnp.where(n == 0, jdiag, o + (o >= jdiag).astype(jnp.int32))

    def visit_variant(kb):
        return jnp.where(kb < jdiag, 1, 0)

    jpos = jdiag * tk + lax.broadcasted_iota(jnp.int32, (tk, tq), 0)
    iposd = qi * tq + lax.broadcasted_iota(jnp.int32, (tk, tq), 1)
    bias = (-LOG2E * slope) * jnp.abs(iposd - jpos).astype(F32)
    score_stage(jdiag, 2, sa_ref, mba_ref, bias)

    bufs = ((sa_ref, mba_ref), (sb_ref, mbb_ref))

    def phase(n, parity, with_scores):
        if with_scores:
            kb_next = visit_block(n + 1)
            score_stage(kb_next, visit_variant(kb_next), *bufs[1 - parity])
        softmax_stage(visit_block(n), *bufs[parity])

    def loop_body(t, carry):
        for j in range(ATTN_UNROLL):
            phase(t * ATTN_UNROLL + j, j % 2, True)
        return carry

    trips = (nkb - 1) // ATTN_UNROLL
    lax.fori_loop(0, trips, loop_body, 0)
    for n in range(trips * ATTN_UNROLL, nkb):
        phase(n, n % 2, n + 1 < nkb)

    a0 = acc_ref[0]
    a1 = acc_ref[1]
    o = (a0[:V_DIM] * (1.0 / a0[V_DIM:V_DIM + 1])
         - lam * (a1[:V_DIM] * (1.0 / a1[V_DIM:V_DIM + 1])))
    ms = jnp.mean(o * o, axis=0, keepdims=True)
    y = o * lax.rsqrt(ms + EPS) * g_ref[...] * (1.0 - LAM_INIT)
    o_ref[0] = y.T.astype(BF16)


def _attention(lam_in, qt, k, vt, gsub, *, batch, seq):
    nkb = seq // TK
    return pl.pallas_call(
        _attention_kernel,
        grid=(batch, N_HEADS, seq // TQ),
        in_specs=[
            pl.BlockSpec(lam_in.shape, lambda b, h, i: (0, 0)),
            pl.BlockSpec((1, 2, QK_PAD, TQ), lambda b, h, i: (b, h, 0, i)),
            pl.BlockSpec((1, 2, seq, QK_PAD), lambda b, h, i: (b, h, 0, 0)),
            pl.BlockSpec((1, 1, nkb, VT_ROWS, TK), lambda b, h, i: (b, h, 0, 0, 0)),
            pl.BlockSpec(gsub.shape, lambda b, h, i: (0, 0)),
        ],
        out_specs=pl.BlockSpec((1, TQ, V_DIM), lambda b, h, i: (b, i, h)),
        out_shape=jax.ShapeDtypeStruct((batch, seq, N_HEADS * V_DIM), BF16),
        scratch_shapes=[
            pltpu.VMEM((3, 2, QK_PAD, TQ), BF16),
            pltpu.VMEM((2, VT_ROWS, TQ), F32),
            pltpu.VMEM((2, 1, TQ), F32),
            pltpu.VMEM((2, TK, TQ), F32),
            pltpu.VMEM((2, TK, TQ), F32),
            pltpu.VMEM((2, 1, TQ), F32),
            pltpu.VMEM((2, 1, TQ), F32),
        ],
        compiler_params=pltpu.CompilerParams(
            dimension_semantics=("arbitrary", "arbitrary", "arbitrary"),
            vmem_limit_bytes=VMEM_LIMIT),
        name="diff_attention",
    )(lam_in, qt, k, vt, gsub)


def _mix_route_kernel(x_ref, a_ref, u_ref, up_ref, un_ref, wp_ref, ps_ref, wo_ref, g2_ref,
                      wr_ref, br_ref, h_ref, xn_ref, gate_ref, id_ref, count_ref, cnt_ref, *, seq):
    tm = x_ref.shape[1]
    j0 = pl.program_id(1) * tm
    halo = up_ref.shape[1]

    ext = jnp.concatenate([up_ref[0], u_ref[0], un_ref[0]], axis=0)
    epos = j0 - halo + lax.broadcasted_iota(jnp.int32, (tm + 2 * halo, 1), 0)
    ext = jnp.where((epos >= 0) & (epos < seq), ext, 0.0)
    tpos = j0 + lax.broadcasted_iota(jnp.int32, (tm, 1), 0)
    ys = []
    for g, w in enumerate(POOL_WINDOWS):
        e = ext[:, g * LANES:(g + 1) * LANES]
        n = e.shape[0]
        s = e[1:] + e[:-1]
        width, start = 2, 0
        while width < w:
            s = s[:-width] + s[width:]
            width *= 2
        off = halo - w // 2
        win = s[off:off + tm]
        cnt = (jnp.minimum(tpos + w // 2, seq) - jnp.maximum(tpos - w // 2, 0)).astype(F32)
        d = (win / cnt - u_ref[0][:, g * LANES:(g + 1) * LANES]).astype(BF16)
        yg = jnp.dot(d, wp_ref[g], preferred_element_type=F32)
        ys.append(yg * ps_ref[:, g * LANES:(g + 1) * LANES])
        del n, start
    p = jnp.concatenate(ys, axis=-1).astype(BF16)

    mix = jnp.concatenate([a_ref[0], p], axis=-1)
    hcur = x_ref[0] + jnp.dot(mix, wo_ref[...], preferred_element_type=F32)
    h_ref[0] = hcur
    ms = jnp.mean(hcur * hcur, axis=-1, keepdims=True)
    xn = (hcur * lax.rsqrt(ms + EPS) * g2_ref[...]).astype(BF16)
    half = xn.shape[1] // 2
    lo = pltpu.bitcast(xn[:, :half].astype(F32), jnp.uint32) >> 16
    hi = pltpu.bitcast(xn[:, half:].astype(F32), jnp.uint32) & jnp.uint32(0xFFFF0000)
    xn_ref[0] = hi | lo

    logits = jnp.dot(xn, wr_ref[...], preferred_element_type=F32) + br_ref[...]
    gl = logits[:, :LANES]
    el = logits[:, LANES:]
    lane = lax.broadcasted_iota(jnp.int32, (tm, LANES), 1)
    gmax = jnp.max(gl, axis=-1, keepdims=True)
    gsel = jnp.min(jnp.where(gl == gmax, lane, LANES), axis=-1, keepdims=True)
    pg = 1.0 / jnp.sum(jnp.exp(gl - gmax), axis=-1, keepdims=True)
    elm = jnp.where((lane >> 3) == gsel, el, NEG)
    m1 = jnp.max(elm, axis=-1, keepdims=True)
    i1 = jnp.min(jnp.where(elm == m1, lane, LANES), axis=-1, keepdims=True)
    elm2 = jnp.where(lane == i1, NEG, elm)
    m2 = jnp.max(elm2, axis=-1, keepdims=True)
    i2 = jnp.min(jnp.where(elm2 == m2, lane, LANES), axis=-1, keepdims=True)
    e2 = jnp.exp(m2 - m1)
    t1 = 1.0 / (1.0 + e2)
    gate_ref[0] = jnp.where(lane == 0, pg * t1, jnp.where(lane == 1, pg * (e2 * t1), 0.0))

    @pl.when((pl.program_id(0) == 0) & (pl.program_id(1) == 0))
    def _():
        cnt_ref[...] = jnp.zeros(cnt_ref.shape, F32)

    hit1 = lane == i1
    hit2 = lane == i2
    onehot = jnp.where(hit1 | hit2, 1.0, 0.0)
    rr = lax.broadcasted_iota(jnp.int32, (tm, tm), 0)
    cc = lax.broadcasted_iota(jnp.int32, (tm, tm), 1)
    tril = jnp.where(cc < rr, 1.0, 0.0).astype(BF16)
    before = jnp.dot(tril, onehot.astype(BF16), preferred_element_type=F32) + cnt_ref[...]
    r1 = jnp.sum(jnp.where(hit1, before, 0.0), axis=-1, keepdims=True).astype(jnp.int32)
    r2 = jnp.sum(jnp.where(hit2, before, 0.0), axis=-1, keepdims=True).astype(jnp.int32)
    cnt_ref[...] = cnt_ref[...] + jnp.sum(onehot, axis=0, keepdims=True)
    count_ref[...] = cnt_ref[...]
    id_ref[0] = jnp.where(lane == 0, i1, jnp.where(lane == 1, i2,
                          jnp.where(lane == 2, r1, jnp.where(lane == 3, r2, 0))))


def _mix_route(x, a, u, wp, ps, wo, g2, wr, br, *, batch, seq):
    tm = TM_PROJ
    d = x.shape[-1]
    halo = 8
    nh = tm // halo
    kern = functools.partial(_mix_route_kernel, seq=seq)
    full = lambda arr: pl.BlockSpec(arr.shape, lambda b, i: (0,) * arr.ndim)
    tile = lambda w: pl.BlockSpec((1, tm, w), lambda b, i: (b, i, 0))
    return pl.pallas_call(
        kern,
        grid=(batch, seq // tm),
        in_specs=[
            tile(d), tile(a.shape[-1]), tile(u.shape[-1]),
            pl.BlockSpec((1, halo, u.shape[-1]), lambda b, i: (b, jnp.maximum(i * nh - 1, 0), 0)),
            pl.BlockSpec((1, halo, u.shape[-1]),
                         lambda b, i: (b, jnp.minimum((i + 1) * nh, seq // halo - 1), 0)),
            full(wp), full(ps), full(wo), full(g2), full(wr), full(br),
        ],
        out_specs=[tile(d), tile(d // 2), tile(LANES), tile(LANES),
                   pl.BlockSpec((1, LANES), lambda b, i: (0, 0))],
        out_shape=[
            jax.ShapeDtypeStruct((batch, seq, d), F32),
            jax.ShapeDtypeStruct((batch, seq, d // 2), jnp.uint32),
            jax.ShapeDtypeStruct((batch, seq, LANES), F32),
            jax.ShapeDtypeStruct((batch, seq, LANES), jnp.int32),
            jax.ShapeDtypeStruct((1, LANES), F32),
        ],
        scratch_shapes=[pltpu.VMEM((1, LANES), F32)],
        compiler_params=pltpu.CompilerParams(
            dimension_semantics=("arbitrary", "arbitrary"), vmem_limit_bytes=VMEM_LIMIT),
        name="mix_route",
    )(x, a, u, u, u, wp, ps, wo, g2, wr, br)


def _dispatch_kernel(dest_ref, x_ref, xs_in, xs_hbm, sem):
    del xs_in
    tm = x_ref.shape[0]

    def group(gi, carry):
        for j in range(DMA_UNROLL):
            r = gi * DMA_UNROLL + j
            for k in range(TOP_K):
                pltpu.make_async_copy(x_ref.at[r], xs_hbm.at[dest_ref[0, k, r]], sem).start()
        return carry

    lax.fori_loop(0, tm // DMA_UNROLL, group, 0)
    for k in range(TOP_K):
        pltpu.make_async_copy(x_ref, xs_hbm.at[pl.ds(0, tm)], sem).wait()


def _dispatch(dest, xn2, xs_zero):
    t, d = xn2.shape
    tm = TM_COMB
    nt = t // tm
    return pl.pallas_call(
        _dispatch_kernel,
        grid=(nt,),
        in_specs=[
            pl.BlockSpec((1, TOP_K, tm), lambda i: (i, 0, 0), memory_space=pltpu.SMEM),
            pl.BlockSpec((tm, d), lambda i: (i, 0)),
            pl.BlockSpec(memory_space=pl.ANY),
        ],
        out_specs=pl.BlockSpec(memory_space=pl.ANY),
        out_shape=jax.ShapeDtypeStruct(xs_zero.shape, xs_zero.dtype),
        input_output_aliases={2: 0},
        scratch_shapes=[pltpu.SemaphoreType.DMA(())],
        compiler_params=pltpu.CompilerParams(
            dimension_semantics=("arbitrary",), vmem_limit_bytes=VMEM_LIMIT),
        name="dispatch",
    )(dest.reshape(nt, tm, TOP_K).transpose(0, 2, 1), xn2, xs_zero)


def _experts_kernel(be_ref, nv_ref, xs_ref, wg_ref, wu_ref, wd_ref, y_ref, wgb, wub, wdb):
    i = pl.program_id(0)
    changed = jnp.logical_or(i == 0, be_ref[i] != be_ref[jnp.maximum(i - 1, 0)])

    @pl.when(changed)
    def _():
        wgb[...] = wg_ref[0].astype(BF16)
        wub[...] = wu_ref[0].astype(BF16)
        wdb[...] = wd_ref[0].astype(BF16)

    @pl.when(nv_ref[i] > 0)
    def _():
        words = xs_ref[...]
        lo = pltpu.bitcast(words << 16, F32)
        hi = pltpu.bitcast(words & jnp.uint32(0xFFFF0000), F32)
        xb = jnp.concatenate([lo, hi], axis=1).astype(BF16)
        gt = jnp.dot(xb, wgb[...], preferred_element_type=F32)
        up = jnp.dot(xb, wub[...], preferred_element_type=F32)
        hid = (gt * (1.0 / (1.0 + jnp.exp(-gt))) * up).astype(BF16)
        y_ref[...] = jnp.dot(hid, wdb[...], preferred_element_type=F32)

    @pl.when(nv_ref[i] == 0)
    def _():
        y_ref[...] = jnp.zeros(y_ref.shape, F32)


def _experts(block_e, nvalid, xs, w_gate, w_up, w_down):
    nb = block_e.shape[0]
    d, de = w_gate.shape[-2:]
    grid_spec = pltpu.PrefetchScalarGridSpec(
        num_scalar_prefetch=2,
        grid=(nb,),
        in_specs=[
            pl.BlockSpec((EXP_BLK, xs.shape[-1]), lambda i, be, nv: (i, 0)),
            pl.BlockSpec((1, d, de), lambda i, be, nv: (be[i], 0, 0)),
            pl.BlockSpec((1, d, de), lambda i, be, nv: (be[i], 0, 0)),
            pl.BlockSpec((1, de, d), lambda i, be, nv: (be[i], 0, 0)),
        ],
        out_specs=pl.BlockSpec((EXP_BLK, d), lambda i, be, nv: (i, 0)),
        scratch_shapes=[pltpu.VMEM((d, de), BF16), pltpu.VMEM((d, de), BF16), pltpu.VMEM((de, d), BF16)],
    )
    return pl.pallas_call(
        _experts_kernel,
        grid_spec=grid_spec,
        out_shape=jax.ShapeDtypeStruct((nb * EXP_BLK, d), F32),
        compiler_params=pltpu.CompilerParams(
            dimension_semantics=("arbitrary",), vmem_limit_bytes=VMEM_LIMIT),
        name="experts",
    )(block_e, nvalid, xs, w_gate, w_up, w_down)


def _combine_kernel(dest_ref, h_ref, gate_ref, y_hbm, g_ref, o_ref, ybuf, sem):
    tm = h_ref.shape[0]

    def group(gi, carry):
        for j in range(DMA_UNROLL):
            r = gi * DMA_UNROLL + j
            for k in range(TOP_K):
                pltpu.make_async_copy(y_hbm.at[dest_ref[0, k, r]], ybuf.at[k, r], sem).start()
        return carry

    lax.fori_loop(0, tm // DMA_UNROLL, group, 0)
    for k in range(TOP_K):
        pltpu.make_async_copy(y_hbm.at[pl.ds(0, tm)], ybuf.at[k], sem).wait()
    gates = gate_ref[...]
    hcur = h_ref[...] + (gates[:, 0:1] * ybuf[0] + gates[:, 1:2] * ybuf[1])
    ms = jnp.mean(hcur * hcur, axis=-1, keepdims=True)
    o_ref[...] = hcur * lax.rsqrt(ms + EPS) * g_ref[...]


def _combine(dest, h2, gates, ys, gfin):
    t, d = h2.shape
    tm = TM_COMB
    nt = t // tm
    return pl.pallas_call(
        _combine_kernel,
        grid=(nt,),
        in_specs=[
            pl.BlockSpec((1, TOP_K, tm), lambda i: (i, 0, 0), memory_space=pltpu.SMEM),
            pl.BlockSpec((tm, d), lambda i: (i, 0)),
            pl.BlockSpec((tm, LANES), lambda i: (i, 0)),
            pl.BlockSpec(memory_space=pl.ANY),
            pl.BlockSpec(gfin.shape, lambda i: (0, 0)),
        ],
        out_specs=pl.BlockSpec((tm, d), lambda i: (i, 0)),
        out_shape=jax.ShapeDtypeStruct((t, d), F32),
        scratch_shapes=[pltpu.VMEM((TOP_K, tm, d), F32), pltpu.SemaphoreType.DMA(())],
        compiler_params=pltpu.CompilerParams(
            dimension_semantics=("arbitrary",), vmem_limit_bytes=VMEM_LIMIT),
        name="combine",
    )(dest.reshape(nt, tm, TOP_K).transpose(0, 2, 1), h2, gates, ys, gfin)


def _pad_head_cols(w):
    d = w.shape[0]
    w = w.reshape(d, 2 * N_HEADS, HEAD_DIM)
    return jnp.pad(w, ((0, 0), (0, 0), (0, QK_PAD - HEAD_DIM))).reshape(d, 2 * N_HEADS * QK_PAD)


def _route_plan(meta, counts, t):
    ids = meta[:, :TOP_K]
    ranks = meta[:, TOP_K:2 * TOP_K]
    cnt = counts[0, :N_EXPERTS].astype(jnp.int32)
    padded = (cnt + EXP_BLK - 1) // EXP_BLK * EXP_BLK
    pad_ends = jnp.cumsum(padded)
    pad_starts = pad_ends - padded
    experts = jnp.arange(N_EXPERTS, dtype=jnp.int32)
    dest = jnp.sum(jnp.where(ids[..., None] == experts, pad_starts, 0), axis=-1) + ranks
    nb = (t * TOP_K + EXP_BLK - 1) // EXP_BLK + N_EXPERTS
    bstart = jnp.arange(nb, dtype=jnp.int32) * EXP_BLK
    block_e = jnp.minimum(jnp.sum((pad_ends[None, :] <= bstart[:, None]).astype(jnp.int32), axis=1),
                          N_EXPERTS - 1)
    row_end = jnp.sum(jnp.where(block_e[:, None] == experts, pad_starts + cnt, 0), axis=-1)
    nvalid = jnp.clip(row_end - bstart, 0, EXP_BLK)
    return dest.astype(jnp.int32), block_e.astype(jnp.int32), nvalid.astype(jnp.int32)


def kernel(x, norm1_g, w_in, lambda_q1, lambda_k1, lambda_q2, lambda_k2, subln_g, w_pool, pool_scale,
           w_out, norm2_g, w_group_router, b_group_router, w_expert_router, b_expert_router,
           w_gate, w_up, w_down, final_g):
    batch, seq, d = x.shape
    t = batch * seq
    qk_w = 2 * N_HEADS * HEAD_DIM
    av_w = N_HEADS * V_DIM
    l = 0
    assert seq % (2 * TK) == 0 and seq % TM_PROJ == 0 and t % TM_COMB == 0 and TK == TM_PROJ

    wi = w_in[l]
    wqt = _pad_head_cols(wi[:, :qk_w]).T.astype(BF16)
    wk = _pad_head_cols(wi[:, qk_w:2 * qk_w]).astype(BF16)
    wv = wi[:, 2 * qk_w:2 * qk_w + av_w].reshape(d, N_HEADS, V_DIM)
    wvt = jnp.pad(wv, ((0, 0), (0, 0), (0, VT_ROWS - V_DIM))).reshape(d, N_HEADS * VT_ROWS).T.astype(BF16)
    wu = wi[:, 2 * qk_w + av_w:].astype(BF16)

    qt, k, vt, u = _in_proj(x.reshape(t, d), norm1_g[l].reshape(1, d), wqt, wk, wvt, wu,
                            batch=batch, seq=seq)

    lam_in = jnp.zeros((8, LANES), F32)
    lam_in = lam_in.at[0, :HEAD_DIM].set(lambda_q1[l]).at[1, :HEAD_DIM].set(lambda_k1[l])
    lam_in = lam_in.at[2, :HEAD_DIM].set(lambda_q2[l]).at[3, :HEAD_DIM].set(lambda_k2[l])
    gsub = jnp.broadcast_to(subln_g[l].reshape(V_DIM, 1), (V_DIM, TQ)).astype(F32)
    a = _attention(lam_in, qt, k, vt, gsub, batch=batch, seq=seq)

    wr = jnp.zeros((d, 2 * LANES), F32)
    wr = wr.at[:, :N_GROUPS].set(w_group_router[l])
    wr = wr.at[:, LANES:LANES + N_EXPERTS].set(w_expert_router[l].reshape(d, N_EXPERTS)).astype(BF16)
    br = jnp.full((1, 2 * LANES), NEG, F32)
    br = br.at[0, :N_GROUPS].set(b_group_router[l])
    br = br.at[0, LANES:LANES + N_EXPERTS].set(b_expert_router[l].reshape(N_EXPERTS))
    h, xn2, gates, meta, counts = _mix_route(
        x, a, u.reshape(batch, seq, -1), w_pool[l].astype(BF16), pool_scale[l].reshape(1, -1),
        w_out[l].astype(BF16), norm2_g[l].reshape(1, d), wr, br, batch=batch, seq=seq)

    dest, block_e, nvalid = _route_plan(meta.reshape(t, LANES), counts, t)
    xs = _dispatch(dest, xn2.reshape(t, d // 2),
                   jnp.zeros((block_e.shape[0] * EXP_BLK, d // 2), jnp.uint32))
    ys = _experts(block_e, nvalid, xs, w_gate[l], w_up[l], w_down[l])
    out = _combine(dest, h.reshape(t, d), gates.reshape(t, LANES), ys, final_g.reshape(1, d))
    return out.reshape(batch, seq, d)
```

```python
import functools
import math

import jax
import jax.numpy as jnp
import numpy as np
from jax import lax
from jax.experimental import pallas as pl
from jax.experimental.pallas import tpu as pltpu

F32 = jnp.float32
BF16 = jnp.bfloat16

N_HEADS = 4
HEAD_DIM = 64
V_DIM = 128
POOL_WINDOWS = (2, 4, 8, 16)
N_GROUPS = 4
E_PER_GROUP = 8
N_EXPERTS = N_GROUPS * E_PER_GROUP
TOP_K = 2
EPS = 1e-6
LAM_INIT = 0.8 - 0.6 * math.exp(-0.3 * 0)
NEG = -1e30
LOG2E = math.log2(math.e)


def _bf16_terms(value, n):
    terms, rest = [], value
    for _ in range(n):
        term = float(np.asarray(rest, dtype=jnp.bfloat16))
        terms.append(term)
        rest -= term
    return tuple(terms)


LOG2E_TERMS = _bf16_terms(LOG2E, 3)
N_TERMS = len(LOG2E_TERMS)
SLOPES = tuple(2.0 ** (-8.0 * (h + 1) / N_HEADS) for h in range(N_HEADS))

LANES = 128
QK_PAD = 128
VT_ROWS = 144
POS_SPLIT = 128

TM_PROJ = 512
TQ = 512
TK = 512
ATTN_UNROLL = 6
ATTN_QSUB = 4
EXP_BLK = 512
TM_COMB = 256
DMA_UNROLL = 8
VMEM_LIMIT = 56 * 1024 * 1024


def _nt_dot(a, b):
    return lax.dot_general(a, b, (((1,), (1,)), ((), ())), preferred_element_type=F32)


def _in_proj_kernel(x_ref, g_ref, wqt_ref, wk_ref, wvt_ref, wu_ref, kc_ref,
                    qt_ref, k_ref, vt_ref, u_ref, *, seq):
    tm = x_ref.shape[0]
    i = pl.program_id(0)
    j0 = (i % (seq // tm)) * tm
    x = x_ref[...]
    ms = jnp.mean(x * x, axis=-1, keepdims=True)
    hn = (x * lax.rsqrt(ms + EPS) * g_ref[...]).astype(BF16)

    qt = _nt_dot(wqt_ref[...], hn) * (LOG2E / math.sqrt(HEAD_DIM))
    for g in range(2 * N_HEADS):
        qt_ref[0, g] = qt[g * QK_PAD:(g + 1) * QK_PAD].astype(BF16)

    kk = jnp.dot(hn, wk_ref[...], preferred_element_type=F32)
    pos = j0 + lax.broadcasted_iota(jnp.int32, (tm, QK_PAD), 0)
    lane = lax.broadcasted_iota(jnp.int32, (tm, QK_PAD), 1)
    hi = (pos & ~(POS_SPLIT - 1)).astype(F32)
    lo = (pos & (POS_SPLIT - 1)).astype(F32)
    f_hi, f_lo = HEAD_DIM + 2 * N_TERMS, HEAD_DIM + 3 * N_TERMS
    feat = jnp.where((lane >= f_hi) & (lane < f_lo), hi,
                     jnp.where((lane >= f_lo) & (lane < f_lo + N_TERMS), lo, 0.0))
    for g in range(2 * N_HEADS):
        k_ref[0, g] = (kk[:, g * QK_PAD:(g + 1) * QK_PAD] + (feat + kc_ref[g:g + 1, :])).astype(BF16)

    vt = _nt_dot(wvt_ref[...], hn)
    row = lax.broadcasted_iota(jnp.int32, (VT_ROWS, tm), 0)
    for h in range(N_HEADS):
        blk = vt[h * VT_ROWS:(h + 1) * VT_ROWS]
        vt_ref[0, h, 0] = jnp.where(row == V_DIM, 1.0, blk).astype(BF16)

    u_ref[...] = jnp.dot(hn, wu_ref[...], preferred_element_type=F32)


def _key_feature_consts():
    kc = np.zeros((2 * N_HEADS, QK_PAD), np.float32)
    for g in range(2 * N_HEADS):
        for rep in range(2):
            for n, term in enumerate(LOG2E_TERMS):
                kc[g, HEAD_DIM + rep * N_TERMS + n] = term * SLOPES[g // 2]
    return jnp.asarray(kc)


def _in_proj(x2, g1, wqt, wk, wvt, wu, *, batch, seq):
    t, d = x2.shape
    tm = TM_PROJ
    nblk = seq // tm
    kern = functools.partial(_in_proj_kernel, seq=seq)
    full = lambda shape: pl.BlockSpec(shape, lambda i: (0,) * len(shape))
    kc = _key_feature_consts()
    return pl.pallas_call(
        kern,
        grid=(t // tm,),
        in_specs=[
            pl.BlockSpec((tm, d), lambda i: (i, 0)),
            full(g1.shape), full(wqt.shape), full(wk.shape), full(wvt.shape), full(wu.shape),
            full(kc.shape),
        ],
        out_specs=[
            pl.BlockSpec((1, 2 * N_HEADS, QK_PAD, tm), lambda i: (i // nblk, 0, 0, i % nblk)),
            pl.BlockSpec((1, 2 * N_HEADS, tm, QK_PAD), lambda i: (i // nblk, 0, i % nblk, 0)),
            pl.BlockSpec((1, N_HEADS, 1, VT_ROWS, tm), lambda i: (i // nblk, 0, i % nblk, 0, 0)),
            pl.BlockSpec((tm, wu.shape[1]), lambda i: (i, 0)),
        ],
        out_shape=[
            jax.ShapeDtypeStruct((batch, 2 * N_HEADS, QK_PAD, seq), BF16),
            jax.ShapeDtypeStruct((batch, 2 * N_HEADS, seq, QK_PAD), BF16),
            jax.ShapeDtypeStruct((batch, N_HEADS, nblk, VT_ROWS, tm), BF16),
            jax.ShapeDtypeStruct((t, wu.shape[1]), F32),
        ],
        compiler_params=pltpu.CompilerParams(
            dimension_semantics=("arbitrary",), vmem_limit_bytes=VMEM_LIMIT),
        name="in_proj",
    )(x2, g1, wqt, wk, wvt, wu, kc)


def _attention_kernel(lam_ref, qt_ref, k_ref, vt_ref, g_ref, o_ref,
                      qv_ref, acc_ref, m_ref, sa_ref, sb_ref, mba_ref, mbb_ref):
    tq = o_ref.shape[1] // ATTN_QSUB
    for sub in range(ATTN_QSUB):
        _attention_block(
            pl.program_id(2) * ATTN_QSUB + sub, qt_ref.at[:, :, :, sub * tq:(sub + 1) * tq], k_ref, vt_ref,
            qv_ref.at[sub], acc_ref.at[sub], m_ref.at[sub], sa_ref, sb_ref, mba_ref, mbb_ref)
    for sub in range(ATTN_QSUB):
        _attention_finish(lam_ref, g_ref, o_ref.at[:, sub * tq:(sub + 1) * tq, :], acc_ref.at[sub])


def _attention_block(qi, qt_ref, k_ref, vt_ref, qv_ref, acc_ref, m_ref, sa_ref, sb_ref, mba_ref, mbb_ref):
    h = pl.program_id(1)
    tq = qt_ref.shape[-1]
    nkb, _, tk = vt_ref.shape[2:]
    jdiag = (qi * tq) // tk

    slope = jnp.where(h == 0, SLOPES[0], jnp.where(h == 1, SLOPES[1],
                      jnp.where(h == 2, SLOPES[2], SLOPES[3]))).astype(F32)

    r = lax.broadcasted_iota(jnp.int32, (QK_PAD, tq), 0)
    ipos = qi * tq + lax.broadcasted_iota(jnp.int32, (QK_PAD, tq), 1)
    ihi = (ipos & ~(POS_SPLIT - 1)).astype(F32)
    ilo = (ipos & (POS_SPLIT - 1)).astype(F32)
    f0 = HEAD_DIM
    fi = jnp.where((r >= f0) & (r < f0 + N_TERMS), ihi,
                   jnp.where((r >= f0 + N_TERMS) & (r < f0 + 2 * N_TERMS), ilo, 0.0))
    fj = jnp.zeros((QK_PAD, tq), F32)
    for rep in range(2):
        for n, term in enumerate(LOG2E_TERMS):
            fj = jnp.where(r == f0 + (2 + rep) * N_TERMS + n, term * slope, fj)
    for c in range(2):
        q = qt_ref[0, c].astype(F32)
        qv_ref[0, c] = (q + fi - fj).astype(BF16)
        qv_ref[1, c] = (q - fi + fj).astype(BF16)
        qv_ref[2, c] = qt_ref[0, c]

    acc_ref[...] = jnp.zeros(acc_ref.shape, F32)
    m_ref[...] = jnp.full(m_ref.shape, NEG, F32)

    def score_stage(kb, variant, s_ref, mb_ref, bias=None):
        for c in range(2):
            kblk = k_ref[0, c, pl.ds(pl.multiple_of(kb * tk, tk), tk), :]
            st = jnp.dot(kblk, qv_ref[variant, c], preferred_element_type=F32)
            if bias is not None:
                st = st + bias
            s_ref[c] = st
            mb_ref[c] = jnp.max(st, axis=0, keepdims=True)

    def softmax_stage(kb, s_ref, mb_ref):
        for c in range(2):
            m_old = m_ref[c]
            m_new = jnp.maximum(m_old, mb_ref[c])
            alpha = jnp.exp2(m_old - m_new)
            p = jnp.exp2(s_ref[c] - m_new).astype(BF16)
            pv = jnp.dot(vt_ref[0, 0, kb], p, preferred_element_type=F32)
            acc_ref[c] = acc_ref[c] * alpha + pv
            m_ref[c] = m_new

    def visit_block(n):
        o = n - 1
        return jnp.where(n == 0, jdiag, o + (o >= jdiag).astype(jnp.int32))

    def visit_variant(kb):
        return jnp.where(kb < jdiag, 1, 0)

    jpos = jdiag * tk + lax.broadcasted_iota(jnp.int32, (tk, tq), 0)
    iposd = qi * tq + lax.broadcasted_iota(jnp.int32, (tk, tq), 1)
    bias = (-LOG2E * slope) * jnp.abs(iposd - jpos).astype(F32)
    score_stage(jdiag, 2, sa_ref, mba_ref, bias)

    bufs = ((sa_ref, mba_ref), (sb_ref, mbb_ref))

    def phase(n, parity, with_scores):
        if with_scores:
            kb_next = visit_block(n + 1)
            score_stage(kb_next, visit_variant(kb_next), *bufs[1 - parity])
        softmax_stage(visit_block(n), *bufs[parity])

    def loop_body(t, carry):
        for j in range(ATTN_UNROLL):
            phase(t * ATTN_UNROLL + j, j % 2, True)
        return carry

    trips = (nkb - 1) // ATTN_UNROLL
    lax.fori_loop(0, trips, loop_body, 0)
    for n in range(trips * ATTN_UNROLL, nkb):
        phase(n, n % 2, n + 1 < nkb)


def _attention_finish(lam_ref, g_ref, o_ref, acc_ref):
    lv = lam_ref[...]
    lam = (jnp.exp(jnp.sum(lv[0:1] * lv[1:2], axis=-1, keepdims=True))
           - jnp.exp(jnp.sum(lv[2:3] * lv[3:4], axis=-1, keepdims=True)) + LAM_INIT)
    a0 = acc_ref[0]
    a1 = acc_ref[1]
    o = (a0[:V_DIM] * (1.0 / a0[V_DIM:V_DIM + 1])
         - lam * (a1[:V_DIM] * (1.0 / a1[V_DIM:V_DIM + 1])))
    ms = jnp.mean(o * o, axis=0, keepdims=True)
    y = o * lax.rsqrt(ms + EPS) * g_ref[...] * (1.0 - LAM_INIT)
    o_ref[0] = y.T.astype(BF16)


def _attention(lam_in, qt, k, vt, gsub, *, batch, seq):
    nkb = seq // TK
    return pl.pallas_call(
        _attention_kernel,
        grid=(batch, N_HEADS, seq // (ATTN_QSUB * TQ)),
        in_specs=[
            pl.BlockSpec(lam_in.shape, lambda b, h, i: (0, 0)),
            pl.BlockSpec((1, 2, QK_PAD, ATTN_QSUB * TQ), lambda b, h, i: (b, h, 0, i)),
            pl.BlockSpec((1, 2, seq, QK_PAD), lambda b, h, i: (b, h, 0, 0)),
            pl.BlockSpec((1, 1, nkb, VT_ROWS, TK), lambda b, h, i: (b, h, 0, 0, 0)),
            pl.BlockSpec(gsub.shape, lambda b, h, i: (0, 0)),
        ],
        out_specs=pl.BlockSpec((1, ATTN_QSUB * TQ, V_DIM), lambda b, h, i: (b, i, h)),
        out_shape=jax.ShapeDtypeStruct((batch, seq, N_HEADS * V_DIM), BF16),
        scratch_shapes=[
            pltpu.VMEM((ATTN_QSUB, 3, 2, QK_PAD, TQ), BF16),
            pltpu.VMEM((ATTN_QSUB, 2, VT_ROWS, TQ), F32),
            pltpu.VMEM((ATTN_QSUB, 2, 1, TQ), F32),
            pltpu.VMEM((2, TK, TQ), F32),
            pltpu.VMEM((2, TK, TQ), F32),
            pltpu.VMEM((2, 1, TQ), F32),
            pltpu.VMEM((2, 1, TQ), F32),
        ],
        compiler_params=pltpu.CompilerParams(
            dimension_semantics=("arbitrary", "arbitrary", "arbitrary"),
            vmem_limit_bytes=VMEM_LIMIT),
        name="diff_attention",
    )(lam_in, qt, k, vt, gsub)


def _mix_route_kernel(x_ref, a_ref, u_ref, up_ref, un_ref, wp_ref, ps_ref, wo_ref, g2_ref,
                      wr_ref, br_ref, h_ref, xn_ref, gate_ref, id_ref, count_ref, cnt_ref, *, seq):
    tm = x_ref.shape[1]
    j0 = pl.program_id(1) * tm
    halo = up_ref.shape[1]

    ext = jnp.concatenate([up_ref[0], u_ref[0], un_ref[0]], axis=0)
    epos = j0 - halo + lax.broadcasted_iota(jnp.int32, (tm + 2 * halo, 1), 0)
    ext = jnp.where((epos >= 0) & (epos < seq), ext, 0.0)
    tpos = j0 + lax.broadcasted_iota(jnp.int32, (tm, 1), 0)
    ys = []
    for g, w in enumerate(POOL_WINDOWS):
        e = ext[:, g * LANES:(g + 1) * LANES]
        n = e.shape[0]
        s = e[1:] + e[:-1]
        width, start = 2, 0
        while width < w:
            s = s[:-width] + s[width:]
            width *= 2
        off = halo - w // 2
        win = s[off:off + tm]
        cnt = (jnp.minimum(tpos + w // 2, seq) - jnp.maximum(tpos - w // 2, 0)).astype(F32)
        d = (win / cnt - u_ref[0][:, g * LANES:(g + 1) * LANES]).astype(BF16)
        yg = jnp.dot(d, wp_ref[g], preferred_element_type=F32)
        ys.append(yg * ps_ref[:, g * LANES:(g + 1) * LANES])
        del n, start
    p = jnp.concatenate(ys, axis=-1).astype(BF16)

    mix = jnp.concatenate([a_ref[0], p], axis=-1)
    hcur = x_ref[0] + jnp.dot(mix, wo_ref[...], preferred_element_type=F32)
    h_ref[0] = hcur
    ms = jnp.mean(hcur * hcur, axis=-1, keepdims=True)
    xn = (hcur * lax.rsqrt(ms + EPS) * g2_ref[...]).astype(BF16)
    half = xn.shape[1] // 2
    lo = pltpu.bitcast(xn[:, :half].astype(F32), jnp.uint32) >> 16
    hi = pltpu.bitcast(xn[:, half:].astype(F32), jnp.uint32) & jnp.uint32(0xFFFF0000)
    xn_ref[0] = hi | lo

    logits = jnp.dot(xn, wr_ref[...], preferred_element_type=F32) + br_ref[...]
    gl = logits[:, :LANES]
    el = logits[:, LANES:]
    lane = lax.broadcasted_iota(jnp.int32, (tm, LANES), 1)
    gmax = jnp.max(gl, axis=-1, keepdims=True)
    gsel = jnp.min(jnp.where(gl == gmax, lane, LANES), axis=-1, keepdims=True)
    pg = 1.0 / jnp.sum(jnp.exp(gl - gmax), axis=-1, keepdims=True)
    elm = jnp.where((lane >> 3) == gsel, el, NEG)
    m1 = jnp.max(elm, axis=-1, keepdims=True)
    i1 = jnp.min(jnp.where(elm == m1, lane, LANES), axis=-1, keepdims=True)
    elm2 = jnp.where(lane == i1, NEG, elm)
    m2 = jnp.max(elm2, axis=-1, keepdims=True)
    i2 = jnp.min(jnp.where(elm2 == m2, lane, LANES), axis=-1, keepdims=True)
    e2 = jnp.exp(m2 - m1)
    t1 = 1.0 / (1.0 + e2)
    gate_ref[0] = jnp.where(lane == 0, pg * t1, jnp.where(lane == 1, pg * (e2 * t1), 0.0))

    @pl.when((pl.program_id(0) == 0) & (pl.program_id(1) == 0))
    def _():
        cnt_ref[...] = jnp.zeros(cnt_ref.shape, F32)

    hit1 = lane == i1
    hit2 = lane == i2
    onehot = jnp.where(hit1 | hit2, 1.0, 0.0)
    rr = lax.broadcasted_iota(jnp.int32, (tm, tm), 0)
    cc = lax.broadcasted_iota(jnp.int32, (tm, tm), 1)
    tril = jnp.where(cc < rr, 1.0, 0.0).astype(BF16)
    before = jnp.dot(tril, onehot.astype(BF16), preferred_element_type=F32) + cnt_ref[...]
    r1 = jnp.sum(jnp.where(hit1, before, 0.0), axis=-1, keepdims=True).astype(jnp.int32)
    r2 = jnp.sum(jnp.where(hit2, before, 0.0), axis=-1, keepdims=True).astype(jnp.int32)
    cnt_ref[...] = cnt_ref[...] + jnp.sum(onehot, axis=0, keepdims=True)
    count_ref[...] = cnt_ref[...]
    id_ref[0] = jnp.where(lane == 0, i1, jnp.where(lane == 1, i2,
                          jnp.where(lane == 2, r1, jnp.where(lane == 3, r2, 0))))


def _mix_route(x, a, u, wp, ps, wo, g2, wr, br, *, batch, seq):
    tm = TM_PROJ
    d = x.shape[-1]
    halo = 8
    nh = tm // halo
    kern = functools.partial(_mix_route_kernel, seq=seq)
    full = lambda arr: pl.BlockSpec(arr.shape, lambda b, i: (0,) * arr.ndim)
    tile = lambda w: pl.BlockSpec((1, tm, w), lambda b, i: (b, i, 0))
    return pl.pallas_call(
        kern,
        grid=(batch, seq // tm),
        in_specs=[
            tile(d), tile(a.shape[-1]), tile(u.shape[-1]),
            pl.BlockSpec((1, halo, u.shape[-1]), lambda b, i: (b, jnp.maximum(i * nh - 1, 0), 0)),
            pl.BlockSpec((1, halo, u.shape[-1]),
                         lambda b, i: (b, jnp.minimum((i + 1) * nh, seq // halo - 1), 0)),
            full(wp), full(ps), full(wo), full(g2), full(wr), full(br),
        ],
        out_specs=[tile(d), tile(d // 2), tile(LANES), tile(LANES),
                   pl.BlockSpec((1, LANES), lambda b, i: (0, 0))],
        out_shape=[
            jax.ShapeDtypeStruct((batch, seq, d), F32),
            jax.ShapeDtypeStruct((batch, seq, d // 2), jnp.uint32),
            jax.ShapeDtypeStruct((batch, seq, LANES), F32),
            jax.ShapeDtypeStruct((batch, seq, LANES), jnp.int32),
            jax.ShapeDtypeStruct((1, LANES), F32),
        ],
        scratch_shapes=[pltpu.VMEM((1, LANES), F32)],
        compiler_params=pltpu.CompilerParams(
            dimension_semantics=("arbitrary", "arbitrary"), vmem_limit_bytes=VMEM_LIMIT),
        name="mix_route",
    )(x, a, u, u, u, wp, ps, wo, g2, wr, br)


def _dispatch_kernel(dest_ref, x_ref, xs_in, xs_hbm, sem):
    del xs_in
    tm = x_ref.shape[0]

    def group(gi, carry):
        for j in range(DMA_UNROLL):
            r = gi * DMA_UNROLL + j
            for k in range(TOP_K):
                pltpu.make_async_copy(x_ref.at[r], xs_hbm.at[dest_ref[0, k, r]], sem).start()
        return carry

    lax.fori_loop(0, tm // DMA_UNROLL, group, 0)
    for k in range(TOP_K):
        pltpu.make_async_copy(x_ref, xs_hbm.at[pl.ds(0, tm)], sem).wait()


def _dispatch(dest, xn2, xs_zero):
    t, d = xn2.shape
    tm = TM_COMB
    nt = t // tm
    return pl.pallas_call(
        _dispatch_kernel,
        grid=(nt,),
        in_specs=[
            pl.BlockSpec((1, TOP_K, tm), lambda i: (i, 0, 0), memory_space=pltpu.SMEM),
            pl.BlockSpec((tm, d), lambda i: (i, 0)),
            pl.BlockSpec(memory_space=pl.ANY),
        ],
        out_specs=pl.BlockSpec(memory_space=pl.ANY),
        out_shape=jax.ShapeDtypeStruct(xs_zero.shape, xs_zero.dtype),
        input_output_aliases={2: 0},
        scratch_shapes=[pltpu.SemaphoreType.DMA(())],
        compiler_params=pltpu.CompilerParams(
            dimension_semantics=("arbitrary",), vmem_limit_bytes=VMEM_LIMIT),
        name="dispatch",
    )(dest.reshape(nt, tm, TOP_K).transpose(0, 2, 1), xn2, xs_zero)


def _experts_kernel(be_ref, nv_ref, nx_ref, xs_ref, wg_hbm, wu_hbm, wd_hbm, y_ref,
                    wgf, wuf, wdf, wgb, wub, wdb, slot_ref, sem):
    i = pl.program_id(0)
    changed = jnp.logical_or(i == 0, be_ref[i] != be_ref[jnp.maximum(i - 1, 0)])

    def weight_copies(e, slot):
        return (pltpu.make_async_copy(wg_hbm.at[e], wgf.at[slot], sem.at[slot]),
                pltpu.make_async_copy(wu_hbm.at[e], wuf.at[slot], sem.at[slot]),
                pltpu.make_async_copy(wd_hbm.at[e], wdf.at[slot], sem.at[slot]))

    @pl.when(i == 0)
    def _():
        slot_ref[0] = 0
        for cp in weight_copies(be_ref[0], 0):
            cp.start()

    @pl.when(changed)
    def _():
        slot = jnp.where(i == 0, 0, 1 - slot_ref[0])
        slot_ref[0] = slot
        for cp in weight_copies(be_ref[i], slot):
            cp.wait()
        wgb[...] = wgf[slot].astype(BF16)
        wub[...] = wuf[slot].astype(BF16)
        wdb[...] = wdf[slot].astype(BF16)

        @pl.when(nx_ref[i] >= 0)
        def _():
            for cp in weight_copies(nx_ref[i], 1 - slot):
                cp.start()

    @pl.when(nv_ref[i] > 0)
    def _():
        words = xs_ref[...]
        lo = pltpu.bitcast(words << 16, F32)
        hi = pltpu.bitcast(words & jnp.uint32(0xFFFF0000), F32)
        xb = jnp.concatenate([lo, hi], axis=1).astype(BF16)
        gt = jnp.dot(xb, wgb[...], preferred_element_type=F32)
        up = jnp.dot(xb, wub[...], preferred_element_type=F32)
        hid = (gt * (1.0 / (1.0 + jnp.exp(-gt))) * up).astype(BF16)
        y_ref[...] = jnp.dot(hid, wdb[...], preferred_element_type=F32)

    @pl.when(nv_ref[i] == 0)
    def _():
        y_ref[...] = jnp.zeros(y_ref.shape, F32)


def _experts(block_e, nvalid, next_e, xs, w_gate, w_up, w_down):
    nb = block_e.shape[0]
    d, de = w_gate.shape[-2:]
    grid_spec = pltpu.PrefetchScalarGridSpec(
        num_scalar_prefetch=3,
        grid=(nb,),
        in_specs=[
            pl.BlockSpec((EXP_BLK, xs.shape[-1]), lambda i, be, nv, nx: (i, 0)),
            pl.BlockSpec(memory_space=pl.ANY),
            pl.BlockSpec(memory_space=pl.ANY),
            pl.BlockSpec(memory_space=pl.ANY),
        ],
        out_specs=pl.BlockSpec((EXP_BLK, d), lambda i, be, nv, nx: (i, 0)),
        scratch_shapes=[
            pltpu.VMEM((2, d, de), F32), pltpu.VMEM((2, d, de), F32), pltpu.VMEM((2, de, d), F32),
            pltpu.VMEM((d, de), BF16), pltpu.VMEM((d, de), BF16), pltpu.VMEM((de, d), BF16),
            pltpu.SMEM((1,), jnp.int32), pltpu.SemaphoreType.DMA((2,)),
        ],
    )
    return pl.pallas_call(
        _experts_kernel,
        grid_spec=grid_spec,
        out_shape=jax.ShapeDtypeStruct((nb * EXP_BLK, d), F32),
        compiler_params=pltpu.CompilerParams(
            dimension_semantics=("arbitrary",), vmem_limit_bytes=VMEM_LIMIT),
        name="experts",
    )(block_e, nvalid, next_e, xs, w_gate, w_up, w_down)


def _combine_kernel(dest_ref, h_ref, gate_ref, y_hbm, g_ref, o_ref, ybuf, sem):
    tm = h_ref.shape[0]

    def group(gi, carry):
        for j in range(DMA_UNROLL):
            r = gi * DMA_UNROLL + j
            for k in range(TOP_K):
                pltpu.make_async_copy(y_hbm.at[dest_ref[0, k, r]], ybuf.at[k, r], sem).start()
        return carry

    lax.fori_loop(0, tm // DMA_UNROLL, group, 0)
    for k in range(TOP_K):
        pltpu.make_async_copy(y_hbm.at[pl.ds(0, tm)], ybuf.at[k], sem).wait()
    gates = gate_ref[...]
    hcur = h_ref[...] + (gates[:, 0:1] * ybuf[0] + gates[:, 1:2] * ybuf[1])
    ms = jnp.mean(hcur * hcur, axis=-1, keepdims=True)
    o_ref[...] = hcur * lax.rsqrt(ms + EPS) * g_ref[...]


def _combine(dest, h2, gates, ys, gfin):
    t, d = h2.shape
    tm = TM_COMB
    nt = t // tm
    return pl.pallas_call(
        _combine_kernel,
        grid=(nt,),
        in_specs=[
            pl.BlockSpec((1, TOP_K, tm), lambda i: (i, 0, 0), memory_space=pltpu.SMEM),
            pl.BlockSpec((tm, d), lambda i: (i, 0)),
            pl.BlockSpec((tm, LANES), lambda i: (i, 0)),
            pl.BlockSpec(memory_space=pl.ANY),
            pl.BlockSpec(gfin.shape, lambda i: (0, 0)),
        ],
        out_specs=pl.BlockSpec((tm, d), lambda i: (i, 0)),
        out_shape=jax.ShapeDtypeStruct((t, d), F32),
        scratch_shapes=[pltpu.VMEM((TOP_K, tm, d), F32), pltpu.SemaphoreType.DMA(())],
        compiler_params=pltpu.CompilerParams(
            dimension_semantics=("arbitrary",), vmem_limit_bytes=VMEM_LIMIT),
        name="combine",
    )(dest.reshape(nt, tm, TOP_K).transpose(0, 2, 1), h2, gates, ys, gfin)


def _pad_head_cols(w):
    d = w.shape[0]
    w = w.reshape(d, 2 * N_HEADS, HEAD_DIM)
    return jnp.pad(w, ((0, 0), (0, 0), (0, QK_PAD - HEAD_DIM))).reshape(d, 2 * N_HEADS * QK_PAD)


def _route_plan(meta, counts, t):
    ids = meta[:, :TOP_K]
    ranks = meta[:, TOP_K:2 * TOP_K]
    cnt = counts[0, :N_EXPERTS].astype(jnp.int32)
    padded = (cnt + EXP_BLK - 1) // EXP_BLK * EXP_BLK
    pad_ends = jnp.cumsum(padded)
    pad_starts = pad_ends - padded
    experts = jnp.arange(N_EXPERTS, dtype=jnp.int32)
    dest = jnp.sum(jnp.where(ids[..., None] == experts, pad_starts, 0), axis=-1) + ranks
    nb = (t * TOP_K + EXP_BLK - 1) // EXP_BLK + N_EXPERTS
    bstart = jnp.arange(nb, dtype=jnp.int32) * EXP_BLK
    last_e = jnp.max(jnp.where(cnt > 0, experts, 0))
    block_e = jnp.minimum(jnp.sum((pad_ends[None, :] <= bstart[:, None]).astype(jnp.int32), axis=1), last_e)
    row_end = jnp.sum(jnp.where(block_e[:, None] == experts, pad_starts + cnt, 0), axis=-1)
    nvalid = jnp.clip(row_end - bstart, 0, EXP_BLK)
    later = (experts[None, :] > experts[:, None]) & (cnt[None, :] > 0)
    next_of = jnp.min(jnp.where(later, experts[None, :], N_EXPERTS), axis=1)
    next_of = jnp.where(next_of == N_EXPERTS, -1, next_of)
    next_e = jnp.sum(jnp.where(block_e[:, None] == experts, next_of, 0), axis=-1)
    return (dest.astype(jnp.int32), block_e.astype(jnp.int32), nvalid.astype(jnp.int32),
            next_e.astype(jnp.int32))


def kernel(x, norm1_g, w_in, lambda_q1, lambda_k1, lambda_q2, lambda_k2, subln_g, w_pool, pool_scale,
           w_out, norm2_g, w_group_router, b_group_router, w_expert_router, b_expert_router,
           w_gate, w_up, w_down, final_g):
    batch, seq, d = x.shape
    t = batch * seq
    qk_w = 2 * N_HEADS * HEAD_DIM
    av_w = N_HEADS * V_DIM
    l = 0
    assert seq % (2 * TK) == 0 and seq % TM_PROJ == 0 and t % TM_COMB == 0 and TK == TM_PROJ
    assert seq % (ATTN_QSUB * TQ) == 0 and ATTN_UNROLL % 2 == 0

    wi = w_in[l]
    wqt = _pad_head_cols(wi[:, :qk_w]).T.astype(BF16)
    wk = _pad_head_cols(wi[:, qk_w:2 * qk_w]).astype(BF16)
    wv = wi[:, 2 * qk_w:2 * qk_w + av_w].reshape(d, N_HEADS, V_DIM)
    wvt = jnp.pad(wv, ((0, 0), (0, 0), (0, VT_ROWS - V_DIM))).reshape(d, N_HEADS * VT_ROWS).T.astype(BF16)
    wu = wi[:, 2 * qk_w + av_w:].astype(BF16)

    qt, k, vt, u = _in_proj(x.reshape(t, d), norm1_g[l].reshape(1, d), wqt, wk, wvt, wu,
                            batch=batch, seq=seq)

    lam_in = jnp.zeros((8, LANES), F32)
    lam_in = lam_in.at[0, :HEAD_DIM].set(lambda_q1[l]).at[1, :HEAD_DIM].set(lambda_k1[l])
    lam_in = lam_in.at[2, :HEAD_DIM].set(lambda_q2[l]).at[3, :HEAD_DIM].set(lambda_k2[l])
    gsub = jnp.broadcast_to(subln_g[l].reshape(V_DIM, 1), (V_DIM, TQ)).astype(F32)
    a = _attention(lam_in, qt, k, vt, gsub, batch=batch, seq=seq)

    wr = jnp.zeros((d, 2 * LANES), F32)
    wr = wr.at[:, :N_GROUPS].set(w_group_router[l])
    wr = wr.at[:, LANES:LANES + N_EXPERTS].set(w_expert_router[l].reshape(d, N_EXPERTS)).astype(BF16)
    br = jnp.full((1, 2 * LANES), NEG, F32)
    br = br.at[0, :N_GROUPS].set(b_group_router[l])
    br = br.at[0, LANES:LANES + N_EXPERTS].set(b_expert_router[l].reshape(N_EXPERTS))
    h, xn2, gates, meta, counts = _mix_route(
        x, a, u.reshape(batch, seq, -1), w_pool[l].astype(BF16), pool_scale[l].reshape(1, -1),
        w_out[l].astype(BF16), norm2_g[l].reshape(1, d), wr, br, batch=batch, seq=seq)

    dest, block_e, nvalid, next_e = _route_plan(meta.reshape(t, LANES), counts, t)
    xs = _dispatch(dest, xn2.reshape(t, d // 2),
                   jnp.zeros((block_e.shape[0] * EXP_BLK, d // 2), jnp.uint32))
    ys = _experts(block_e, nvalid, next_e, xs, w_gate[l], w_up[l], w_down[l])
    out = _combine(dest, h.reshape(t, d), gates.reshape(t, LANES), ys, final_g.reshape(1, d))
    return out.reshape(batch, seq, d)
```

```python
import functools
import math

import jax
import jax.numpy as jnp
import numpy as np
from jax import lax
from jax.experimental import pallas as pl
from jax.experimental.pallas import tpu as pltpu

F32 = jnp.float32
BF16 = jnp.bfloat16

N_HEADS = 4
HEAD_DIM = 64
V_DIM = 128
POOL_WINDOWS = (2, 4, 8, 16)
N_GROUPS = 4
E_PER_GROUP = 8
N_EXPERTS = N_GROUPS * E_PER_GROUP
TOP_K = 2
EPS = 1e-6
LAM_INIT = 0.8 - 0.6 * math.exp(-0.3 * 0)
NEG = -1e30
LOG2E = math.log2(math.e)


def _bf16_terms(value, n):
    terms, rest = [], value
    for _ in range(n):
        term = float(np.asarray(rest, dtype=jnp.bfloat16))
        terms.append(term)
        rest -= term
    return tuple(terms)


LOG2E_TERMS = _bf16_terms(LOG2E, 3)
N_TERMS = len(LOG2E_TERMS)
SLOPES = tuple(2.0 ** (-8.0 * (h + 1) / N_HEADS) for h in range(N_HEADS))

LANES = 128
QK_PAD = 128
VT_ROWS = 144
POS_SPLIT = 128

TM_PROJ = 512
TQ = 512
TK = 512
ATTN_UNROLL = 6
ATTN_QSUB = 1
EXP_BLK = 512
TM_COMB = 256
VMEM_LIMIT = 56 * 1024 * 1024


def _nt_dot(a, b):
    return lax.dot_general(a, b, (((1,), (1,)), ((), ())), preferred_element_type=F32)


def _in_proj_kernel(x_ref, g_ref, wqt_ref, wk_ref, wvt_ref, wu_ref, kc_ref,
                    qt_ref, k_ref, vt_ref, u_ref, *, seq):
    tm = x_ref.shape[0]
    i = pl.program_id(0)
    j0 = (i % (seq // tm)) * tm
    x = x_ref[...]
    ms = jnp.mean(x * x, axis=-1, keepdims=True)
    hn = (x * lax.rsqrt(ms + EPS) * g_ref[...]).astype(BF16)

    qt = _nt_dot(wqt_ref[...], hn) * (LOG2E / math.sqrt(HEAD_DIM))
    for g in range(2 * N_HEADS):
        qt_ref[0, g] = qt[g * QK_PAD:(g + 1) * QK_PAD].astype(BF16)

    kk = jnp.dot(hn, wk_ref[...], preferred_element_type=F32)
    pos = j0 + lax.broadcasted_iota(jnp.int32, (tm, QK_PAD), 0)
    lane = lax.broadcasted_iota(jnp.int32, (tm, QK_PAD), 1)
    hi = (pos & ~(POS_SPLIT - 1)).astype(F32)
    lo = (pos & (POS_SPLIT - 1)).astype(F32)
    f_hi, f_lo = HEAD_DIM + 2 * N_TERMS, HEAD_DIM + 3 * N_TERMS
    feat = jnp.where((lane >= f_hi) & (lane < f_lo), hi,
                     jnp.where((lane >= f_lo) & (lane < f_lo + N_TERMS), lo, 0.0))
    for g in range(2 * N_HEADS):
        k_ref[0, g] = (kk[:, g * QK_PAD:(g + 1) * QK_PAD] + (feat + kc_ref[g:g + 1, :])).astype(BF16)

    vt = _nt_dot(wvt_ref[...], hn)
    row = lax.broadcasted_iota(jnp.int32, (VT_ROWS, tm), 0)
    for h in range(N_HEADS):
        blk = vt[h * VT_ROWS:(h + 1) * VT_ROWS]
        vt_ref[0, h, 0] = jnp.where(row == V_DIM, 1.0, blk).astype(BF16)

    u_ref[...] = jnp.dot(hn, wu_ref[...], preferred_element_type=F32)


def _key_feature_consts():
    kc = np.zeros((2 * N_HEADS, QK_PAD), np.float32)
    for g in range(2 * N_HEADS):
        for rep in range(2):
            for n, term in enumerate(LOG2E_TERMS):
                kc[g, HEAD_DIM + rep * N_TERMS + n] = term * SLOPES[g // 2]
    return jnp.asarray(kc)


def _in_proj(x2, g1, wqt, wk, wvt, wu, *, batch, seq):
    t, d = x2.shape
    tm = TM_PROJ
    nblk = seq // tm
    kern = functools.partial(_in_proj_kernel, seq=seq)
    full = lambda shape: pl.BlockSpec(shape, lambda i: (0,) * len(shape))
    kc = _key_feature_consts()
    return pl.pallas_call(
        kern,
        grid=(t // tm,),
        in_specs=[
            pl.BlockSpec((tm, d), lambda i: (i, 0)),
            full(g1.shape), full(wqt.shape), full(wk.shape), full(wvt.shape), full(wu.shape),
            full(kc.shape),
        ],
        out_specs=[
            pl.BlockSpec((1, 2 * N_HEADS, QK_PAD, tm), lambda i: (i // nblk, 0, 0, i % nblk)),
            pl.BlockSpec((1, 2 * N_HEADS, tm, QK_PAD), lambda i: (i // nblk, 0, i % nblk, 0)),
            pl.BlockSpec((1, N_HEADS, 1, VT_ROWS, tm), lambda i: (i // nblk, 0, i % nblk, 0, 0)),
            pl.BlockSpec((tm, wu.shape[1]), lambda i: (i, 0)),
        ],
        out_shape=[
            jax.ShapeDtypeStruct((batch, 2 * N_HEADS, QK_PAD, seq), BF16),
            jax.ShapeDtypeStruct((batch, 2 * N_HEADS, seq, QK_PAD), BF16),
            jax.ShapeDtypeStruct((batch, N_HEADS, nblk, VT_ROWS, tm), BF16),
            jax.ShapeDtypeStruct((t, wu.shape[1]), F32),
        ],
        compiler_params=pltpu.CompilerParams(
            dimension_semantics=("arbitrary",), vmem_limit_bytes=VMEM_LIMIT),
        name="in_proj",
    )(x2, g1, wqt, wk, wvt, wu, kc)


def _attention_kernel(lam_ref, qt_ref, k_ref, vt_ref, g_ref, o_ref,
                      qv_ref, acc_ref, m_ref, sa_ref, sb_ref, mba_ref, mbb_ref):
    tq = o_ref.shape[1] // ATTN_QSUB
    for sub in range(ATTN_QSUB):
        _attention_block(
            pl.program_id(2) * ATTN_QSUB + sub, qt_ref.at[:, :, :, sub * tq:(sub + 1) * tq], k_ref, vt_ref,
            qv_ref.at[sub], acc_ref.at[sub], m_ref.at[sub], sa_ref, sb_ref, mba_ref, mbb_ref)
    for sub in range(ATTN_QSUB):
        _attention_finish(lam_ref, g_ref, o_ref.at[:, sub * tq:(sub + 1) * tq, :], acc_ref.at[sub])


def _attention_block(qi, qt_ref, k_ref, vt_ref, qv_ref, acc_ref, m_ref, sa_ref, sb_ref, mba_ref, mbb_ref):
    h = pl.program_id(1)
    tq = qt_ref.shape[-1]
    nkb, _, tk = vt_ref.shape[2:]
    jdiag = (qi * tq) // tk

    slope = jnp.where(h == 0, SLOPES[0], jnp.where(h == 1, SLOPES[1],
                      jnp.where(h == 2, SLOPES[2], SLOPES[3]))).astype(F32)

    r = lax.broadcasted_iota(jnp.int32, (QK_PAD, tq), 0)
    ipos = qi * tq + lax.broadcasted_iota(jnp.int32, (QK_PAD, tq), 1)
    ihi = (ipos & ~(POS_SPLIT - 1)).astype(F32)
    ilo = (ipos & (POS_SPLIT - 1)).astype(F32)
    f0 = HEAD_DIM
    fi = jnp.where((r >= f0) & (r < f0 + N_TERMS), ihi,
                   jnp.where((r >= f0 + N_TERMS) & (r < f0 + 2 * N_TERMS), ilo, 0.0))
    fj = jnp.zeros((QK_PAD, tq), F32)
    for rep in range(2):
        for n, term in enumerate(LOG2E_TERMS):
            fj = jnp.where(r == f0 + (2 + rep) * N_TERMS + n, term * slope, fj)
    for c in range(2):
        q = qt_ref[0, c].astype(F32)
        qv_ref[0, c] = (q + fi - fj).astype(BF16)
        qv_ref[1, c] = (q - fi + fj).astype(BF16)
        qv_ref[2, c] = qt_ref[0, c]

    acc_ref[...] = jnp.zeros(acc_ref.shape, F32)
    m_ref[...] = jnp.full(m_ref.shape, NEG, F32)

    def score_stage(kb, variant, s_ref, mb_ref, bias=None):
        for c in range(2):
            kblk = k_ref[0, c, pl.ds(pl.multiple_of(kb * tk, tk), tk), :]
            st = jnp.dot(kblk, qv_ref[variant, c], preferred_element_type=F32)
            if bias is not None:
                st = st + bias
            s_ref[c] = st
            mb_ref[c] = jnp.max(st, axis=0, keepdims=True)

    def softmax_stage(kb, s_ref, mb_ref):
        for c in range(2):
            m_old = m_ref[c]
            m_new = jnp.maximum(m_old, mb_ref[c])
            alpha = jnp.exp2(m_old - m_new)
            p = jnp.exp2(s_ref[c] - m_new).astype(BF16)
            pv = jnp.dot(vt_ref[0, 0, kb], p, preferred_element_type=F32)
            acc_ref[c] = acc_ref[c] * alpha + pv
            m_ref[c] = m_new

    def visit_block(n):
        o = n - 1
        return jnp.where(n == 0, jdiag, o + (o >= jdiag).astype(jnp.int32))

    def visit_variant(kb):
        return jnp.where(kb < jdiag, 1, 0)

    jpos = jdiag * tk + lax.broadcasted_iota(jnp.int32, (tk, tq), 0)
    iposd = qi * tq + lax.broadcasted_iota(jnp.int32, (tk, tq), 1)
    bias = (-LOG2E * slope) * jnp.abs(iposd - jpos).astype(F32)
    score_stage(jdiag, 2, sa_ref, mba_ref, bias)

    bufs = ((sa_ref, mba_ref), (sb_ref, mbb_ref))

    def phase(n, parity, with_scores):
        if with_scores:
            kb_next = visit_block(n + 1)
            score_stage(kb_next, visit_variant(kb_next), *bufs[1 - parity])
        softmax_stage(visit_block(n), *bufs[parity])

    def loop_body(t, carry):
        for j in range(ATTN_UNROLL):
            phase(t * ATTN_UNROLL + j, j % 2, True)
        return carry

    trips = (nkb - 1) // ATTN_UNROLL
    lax.fori_loop(0, trips, loop_body, 0)
    for n in range(trips * ATTN_UNROLL, nkb):
        phase(n, n % 2, n + 1 < nkb)


def _attention_finish(lam_ref, g_ref, o_ref, acc_ref):
    lv = lam_ref[...]
    lam = (jnp.exp(jnp.sum(lv[0:1] * lv[1:2], axis=-1, keepdims=True))
           - jnp.exp(jnp.sum(lv[2:3] * lv[3:4], axis=-1, keepdims=True)) + LAM_INIT)
    a0 = acc_ref[0]
    a1 = acc_ref[1]
    o = (a0[:V_DIM] * (1.0 / a0[V_DIM:V_DIM + 1])
         - lam * (a1[:V_DIM] * (1.0 / a1[V_DIM:V_DIM + 1])))
    ms = jnp.mean(o * o, axis=0, keepdims=True)
    y = o * lax.rsqrt(ms + EPS) * g_ref[...] * (1.0 - LAM_INIT)
    o_ref[0] = y.T.astype(BF16)


def _attention(lam_in, qt, k, vt, gsub, *, batch, seq):
    nkb = seq // TK
    return pl.pallas_call(
        _attention_kernel,
        grid=(batch, N_HEADS, seq // (ATTN_QSUB * TQ)),
        in_specs=[
            pl.BlockSpec(lam_in.shape, lambda b, h, i: (0, 0)),
            pl.BlockSpec((1, 2, QK_PAD, ATTN_QSUB * TQ), lambda b, h, i: (b, h, 0, i)),
            pl.BlockSpec((1, 2, seq, QK_PAD), lambda b, h, i: (b, h, 0, 0)),
            pl.BlockSpec((1, 1, nkb, VT_ROWS, TK), lambda b, h, i: (b, h, 0, 0, 0)),
            pl.BlockSpec(gsub.shape, lambda b, h, i: (0, 0)),
        ],
        out_specs=pl.BlockSpec((1, ATTN_QSUB * TQ, V_DIM), lambda b, h, i: (b, i, h)),
        out_shape=jax.ShapeDtypeStruct((batch, seq, N_HEADS * V_DIM), BF16),
        scratch_shapes=[
            pltpu.VMEM((ATTN_QSUB, 3, 2, QK_PAD, TQ), BF16),
            pltpu.VMEM((ATTN_QSUB, 2, VT_ROWS, TQ), F32),
            pltpu.VMEM((ATTN_QSUB, 2, 1, TQ), F32),
            pltpu.VMEM((2, TK, TQ), F32),
            pltpu.VMEM((2, TK, TQ), F32),
            pltpu.VMEM((2, 1, TQ), F32),
            pltpu.VMEM((2, 1, TQ), F32),
        ],
        compiler_params=pltpu.CompilerParams(
            dimension_semantics=("arbitrary", "arbitrary", "arbitrary"),
            vmem_limit_bytes=VMEM_LIMIT),
        name="diff_attention",
    )(lam_in, qt, k, vt, gsub)


def _mix_route_kernel(x_ref, a_ref, u_ref, up_ref, un_ref, wp_ref, ps_ref, wo_ref, g2_ref,
                      wr_ref, br_ref, h_ref, xn_ref, gate_ref, id_ref, count_ref, cnt_ref, *, seq):
    tm = x_ref.shape[1]
    j0 = pl.program_id(1) * tm
    halo = up_ref.shape[1]

    ext = jnp.concatenate([up_ref[0], u_ref[0], un_ref[0]], axis=0)
    epos = j0 - halo + lax.broadcasted_iota(jnp.int32, (tm + 2 * halo, 1), 0)
    ext = jnp.where((epos >= 0) & (epos < seq), ext, 0.0)
    tpos = j0 + lax.broadcasted_iota(jnp.int32, (tm, 1), 0)
    ys = []
    for g, w in enumerate(POOL_WINDOWS):
        e = ext[:, g * LANES:(g + 1) * LANES]
        n = e.shape[0]
        s = e[1:] + e[:-1]
        width, start = 2, 0
        while width < w:
            s = s[:-width] + s[width:]
            width *= 2
        off = halo - w // 2
        win = s[off:off + tm]
        cnt = (jnp.minimum(tpos + w // 2, seq) - jnp.maximum(tpos - w // 2, 0)).astype(F32)
        d = (win / cnt - u_ref[0][:, g * LANES:(g + 1) * LANES]).astype(BF16)
        yg = jnp.dot(d, wp_ref[g], preferred_element_type=F32)
        ys.append(yg * ps_ref[:, g * LANES:(g + 1) * LANES])
        del n, start
    p = jnp.concatenate(ys, axis=-1).astype(BF16)

    mix = jnp.concatenate([a_ref[0], p], axis=-1)
    hcur = x_ref[0] + jnp.dot(mix, wo_ref[...], preferred_element_type=F32)
    h_ref[0] = hcur
    ms = jnp.mean(hcur * hcur, axis=-1, keepdims=True)
    xn = (hcur * lax.rsqrt(ms + EPS) * g2_ref[...]).astype(BF16)
    half = xn.shape[1] // 2
    lo = pltpu.bitcast(xn[:, :half].astype(F32), jnp.uint32) >> 16
    hi = pltpu.bitcast(xn[:, half:].astype(F32), jnp.uint32) & jnp.uint32(0xFFFF0000)
    xn_ref[0] = hi | lo

    logits = jnp.dot(xn, wr_ref[...], preferred_element_type=F32) + br_ref[...]
    gl = logits[:, :LANES]
    el = logits[:, LANES:]
    lane = lax.broadcasted_iota(jnp.int32, (tm, LANES), 1)
    gmax = jnp.max(gl, axis=-1, keepdims=True)
    gsel = jnp.min(jnp.where(gl == gmax, lane, LANES), axis=-1, keepdims=True)
    pg = 1.0 / jnp.sum(jnp.exp(gl - gmax), axis=-1, keepdims=True)
    elm = jnp.where((lane >> 3) == gsel, el, NEG)
    m1 = jnp.max(elm, axis=-1, keepdims=True)
    i1 = jnp.min(jnp.where(elm == m1, lane, LANES), axis=-1, keepdims=True)
    elm2 = jnp.where(lane == i1, NEG, elm)
    m2 = jnp.max(elm2, axis=-1, keepdims=True)
    i2 = jnp.min(jnp.where(elm2 == m2, lane, LANES), axis=-1, keepdims=True)
    e2 = jnp.exp(m2 - m1)
    t1 = 1.0 / (1.0 + e2)
    gate_ref[0] = jnp.where(lane == 0, pg * t1, jnp.where(lane == 1, pg * (e2 * t1), 0.0))

    @pl.when((pl.program_id(0) == 0) & (pl.program_id(1) == 0))
    def _():
        cnt_ref[...] = jnp.zeros(cnt_ref.shape, F32)

    hit1 = lane == i1
    hit2 = lane == i2
    onehot = jnp.where(hit1 | hit2, 1.0, 0.0)
    rr = lax.broadcasted_iota(jnp.int32, (tm, tm), 0)
    cc = lax.broadcasted_iota(jnp.int32, (tm, tm), 1)
    tril = jnp.where(cc < rr, 1.0, 0.0).astype(BF16)
    before = jnp.dot(tril, onehot.astype(BF16), preferred_element_type=F32) + cnt_ref[...]
    r1 = jnp.sum(jnp.where(hit1, before, 0.0), axis=-1, keepdims=True).astype(jnp.int32)
    r2 = jnp.sum(jnp.where(hit2, before, 0.0), axis=-1, keepdims=True).astype(jnp.int32)
    cnt_ref[...] = cnt_ref[...] + jnp.sum(onehot, axis=0, keepdims=True)
    count_ref[...] = cnt_ref[...]
    id_ref[0] = jnp.where(lane == 0, i1, jnp.where(lane == 1, i2,
                          jnp.where(lane == 2, r1, jnp.where(lane == 3, r2, 0))))


def _mix_route(x, a, u, wp, ps, wo, g2, wr, br, *, batch, seq):
    tm = TM_PROJ
    d = x.shape[-1]
    halo = 8
    nh = tm // halo
    kern = functools.partial(_mix_route_kernel, seq=seq)
    full = lambda arr: pl.BlockSpec(arr.shape, lambda b, i: (0,) * arr.ndim)
    tile = lambda w: pl.BlockSpec((1, tm, w), lambda b, i: (b, i, 0))
    return pl.pallas_call(
        kern,
        grid=(batch, seq // tm),
        in_specs=[
            tile(d), tile(a.shape[-1]), tile(u.shape[-1]),
            pl.BlockSpec((1, halo, u.shape[-1]), lambda b, i: (b, jnp.maximum(i * nh - 1, 0), 0)),
            pl.BlockSpec((1, halo, u.shape[-1]),
                         lambda b, i: (b, jnp.minimum((i + 1) * nh, seq // halo - 1), 0)),
            full(wp), full(ps), full(wo), full(g2), full(wr), full(br),
        ],
        out_specs=[tile(d), tile(d // 2), tile(LANES), tile(LANES),
                   pl.BlockSpec((1, LANES), lambda b, i: (0, 0))],
        out_shape=[
            jax.ShapeDtypeStruct((batch, seq, d), F32),
            jax.ShapeDtypeStruct((batch, seq, d // 2), jnp.uint32),
            jax.ShapeDtypeStruct((batch, seq, LANES), F32),
            jax.ShapeDtypeStruct((batch, seq, LANES), jnp.int32),
            jax.ShapeDtypeStruct((1, LANES), F32),
        ],
        scratch_shapes=[pltpu.VMEM((1, LANES), F32)],
        compiler_params=pltpu.CompilerParams(
            dimension_semantics=("arbitrary", "arbitrary"), vmem_limit_bytes=VMEM_LIMIT),
        name="mix_route",
    )(x, a, u, u, u, wp, ps, wo, g2, wr, br)


def _dispatch_kernel(dest_ref, x_ref, xs_in, xs_hbm, sem):
    del xs_in
    tm = x_ref.shape[0]

    for r in range(tm):
        for k in range(TOP_K):
            pltpu.make_async_copy(x_ref.at[r], xs_hbm.at[dest_ref[0, k, r]], sem).start(priority=k)
    for k in range(TOP_K):
        pltpu.make_async_copy(x_ref, xs_hbm.at[pl.ds(0, tm)], sem).wait()


def _dispatch(dest, xn2, xs_zero):
    t, d = xn2.shape
    tm = TM_COMB
    nt = t // tm
    return pl.pallas_call(
        _dispatch_kernel,
        grid=(nt,),
        in_specs=[
            pl.BlockSpec((1, TOP_K, tm), lambda i: (i, 0, 0), memory_space=pltpu.SMEM),
            pl.BlockSpec((tm, d), lambda i: (i, 0)),
            pl.BlockSpec(memory_space=pl.ANY),
        ],
        out_specs=pl.BlockSpec(memory_space=pl.ANY),
        out_shape=jax.ShapeDtypeStruct(xs_zero.shape, xs_zero.dtype),
        input_output_aliases={2: 0},
        scratch_shapes=[pltpu.SemaphoreType.DMA(())],
        compiler_params=pltpu.CompilerParams(
            dimension_semantics=("arbitrary",), vmem_limit_bytes=VMEM_LIMIT),
        name="dispatch",
    )(dest.reshape(nt, tm, TOP_K).transpose(0, 2, 1), xn2, xs_zero)


def _experts_kernel(be_ref, nv_ref, nx_ref, xs_ref, wg_hbm, wu_hbm, wd_hbm, y_ref,
                    wgf, wuf, wdf, wgb, wub, wdb, slot_ref, sem):
    i = pl.program_id(0)
    changed = jnp.logical_or(i == 0, be_ref[i] != be_ref[jnp.maximum(i - 1, 0)])

    def weight_copies(e, slot):
        return (pltpu.make_async_copy(wg_hbm.at[e], wgf.at[slot], sem.at[slot]),
                pltpu.make_async_copy(wu_hbm.at[e], wuf.at[slot], sem.at[slot]),
                pltpu.make_async_copy(wd_hbm.at[e], wdf.at[slot], sem.at[slot]))

    @pl.when(i == 0)
    def _():
        slot_ref[0] = 0
        for cp in weight_copies(be_ref[0], 0):
            cp.start()

    @pl.when(changed)
    def _():
        slot = jnp.where(i == 0, 0, 1 - slot_ref[0])
        slot_ref[0] = slot
        for cp in weight_copies(be_ref[i], slot):
            cp.wait()
        wgb[...] = wgf[slot].astype(BF16)
        wub[...] = wuf[slot].astype(BF16)
        wdb[...] = wdf[slot].astype(BF16)

        @pl.when(nx_ref[i] >= 0)
        def _():
            for cp in weight_copies(nx_ref[i], 1 - slot):
                cp.start()

    @pl.when(nv_ref[i] > 0)
    def _():
        words = xs_ref[...]
        lo = pltpu.bitcast(words << 16, F32)
        hi = pltpu.bitcast(words & jnp.uint32(0xFFFF0000), F32)
        xb = jnp.concatenate([lo, hi], axis=1).astype(BF16)
        gt = jnp.dot(xb, wgb[...], preferred_element_type=F32)
        up = jnp.dot(xb, wub[...], preferred_element_type=F32)
        hid = (gt * (1.0 / (1.0 + jnp.exp(-gt))) * up).astype(BF16)
        y_ref[...] = jnp.dot(hid, wdb[...], preferred_element_type=F32)

    @pl.when(nv_ref[i] == 0)
    def _():
        y_ref[...] = jnp.zeros(y_ref.shape, F32)


def _experts(block_e, nvalid, next_e, xs, w_gate, w_up, w_down):
    nb = block_e.shape[0]
    d, de = w_gate.shape[-2:]
    grid_spec = pltpu.PrefetchScalarGridSpec(
        num_scalar_prefetch=3,
        grid=(nb,),
        in_specs=[
            pl.BlockSpec((EXP_BLK, xs.shape[-1]), lambda i, be, nv, nx: (i, 0)),
            pl.BlockSpec(memory_space=pl.ANY),
            pl.BlockSpec(memory_space=pl.ANY),
            pl.BlockSpec(memory_space=pl.ANY),
        ],
        out_specs=pl.BlockSpec((EXP_BLK, d), lambda i, be, nv, nx: (i, 0)),
        scratch_shapes=[
            pltpu.VMEM((2, d, de), F32), pltpu.VMEM((2, d, de), F32), pltpu.VMEM((2, de, d), F32),
            pltpu.VMEM((d, de), BF16), pltpu.VMEM((d, de), BF16), pltpu.VMEM((de, d), BF16),
            pltpu.SMEM((1,), jnp.int32), pltpu.SemaphoreType.DMA((2,)),
        ],
    )
    return pl.pallas_call(
        _experts_kernel,
        grid_spec=grid_spec,
        out_shape=jax.ShapeDtypeStruct((nb * EXP_BLK, d), F32),
        compiler_params=pltpu.CompilerParams(
            dimension_semantics=("arbitrary",), vmem_limit_bytes=VMEM_LIMIT),
        name="experts",
    )(block_e, nvalid, next_e, xs, w_gate, w_up, w_down)


def _combine_kernel(dest_ref, dnext_ref, h_ref, gate_ref, y_hbm, g_ref, o_ref, ybuf, sem):
    tm = h_ref.shape[0]
    i = pl.program_id(0)
    slot = i % 2

    def start_rows(idx_ref, s):
        for r in range(tm):
            for k in range(TOP_K):
                pltpu.make_async_copy(y_hbm.at[idx_ref[0, k, r]], ybuf.at[s, k, r], sem.at[s]).start(priority=k)

    @pl.when(i == 0)
    def _():
        start_rows(dest_ref, 0)

    @pl.when(i + 1 < pl.num_programs(0))
    def _():
        start_rows(dnext_ref, 1 - slot)

    for k in range(TOP_K):
        pltpu.make_async_copy(y_hbm.at[pl.ds(0, tm)], ybuf.at[slot, k], sem.at[slot]).wait()
    gates = gate_ref[...]
    hcur = h_ref[...] + (gates[:, 0:1] * ybuf[slot, 0] + gates[:, 1:2] * ybuf[slot, 1])
    ms = jnp.mean(hcur * hcur, axis=-1, keepdims=True)
    o_ref[...] = hcur * lax.rsqrt(ms + EPS) * g_ref[...]


def _combine(dest, h2, gates, ys, gfin):
    t, d = h2.shape
    tm = TM_COMB
    nt = t // tm
    dest3 = dest.reshape(nt, tm, TOP_K).transpose(0, 2, 1)
    return pl.pallas_call(
        _combine_kernel,
        grid=(nt,),
        in_specs=[
            pl.BlockSpec((1, TOP_K, tm), lambda i: (i, 0, 0), memory_space=pltpu.SMEM),
            pl.BlockSpec((1, TOP_K, tm), lambda i: (jnp.minimum(i + 1, nt - 1), 0, 0),
                         memory_space=pltpu.SMEM),
            pl.BlockSpec((tm, d), lambda i: (i, 0)),
            pl.BlockSpec((tm, LANES), lambda i: (i, 0)),
            pl.BlockSpec(memory_space=pl.ANY),
            pl.BlockSpec(gfin.shape, lambda i: (0, 0)),
        ],
        out_specs=pl.BlockSpec((tm, d), lambda i: (i, 0)),
        out_shape=jax.ShapeDtypeStruct((t, d), F32),
        scratch_shapes=[pltpu.VMEM((2, TOP_K, tm, d), F32), pltpu.SemaphoreType.DMA((2,))],
        compiler_params=pltpu.CompilerParams(
            dimension_semantics=("arbitrary",), vmem_limit_bytes=VMEM_LIMIT),
        name="combine",
    )(dest3, dest3, h2, gates, ys, gfin)


def _pad_head_cols(w):
    d = w.shape[0]
    w = w.reshape(d, 2 * N_HEADS, HEAD_DIM)
    return jnp.pad(w, ((0, 0), (0, 0), (0, QK_PAD - HEAD_DIM))).reshape(d, 2 * N_HEADS * QK_PAD)


def _route_plan(meta, counts, t):
    ids = meta[:, :TOP_K]
    ranks = meta[:, TOP_K:2 * TOP_K]
    cnt = counts[0, :N_EXPERTS].astype(jnp.int32)
    padded = (cnt + EXP_BLK - 1) // EXP_BLK * EXP_BLK
    pad_ends = jnp.cumsum(padded)
    pad_starts = pad_ends - padded
    experts = jnp.arange(N_EXPERTS, dtype=jnp.int32)
    dest = jnp.sum(jnp.where(ids[..., None] == experts, pad_starts, 0), axis=-1) + ranks
    nb = (t * TOP_K + EXP_BLK - 1) // EXP_BLK + N_EXPERTS
    bstart = jnp.arange(nb, dtype=jnp.int32) * EXP_BLK
    last_e = jnp.max(jnp.where(cnt > 0, experts, 0))
    block_e = jnp.minimum(jnp.sum((pad_ends[None, :] <= bstart[:, None]).astype(jnp.int32), axis=1), last_e)
    row_end = jnp.sum(jnp.where(block_e[:, None] == experts, pad_starts + cnt, 0), axis=-1)
    nvalid = jnp.clip(row_end - bstart, 0, EXP_BLK)
    later = (experts[None, :] > experts[:, None]) & (cnt[None, :] > 0)
    next_of = jnp.min(jnp.where(later, experts[None, :], N_EXPERTS), axis=1)
    next_of = jnp.where(next_of == N_EXPERTS, -1, next_of)
    next_e = jnp.sum(jnp.where(block_e[:, None] == experts, next_of, 0), axis=-1)
    return (dest.astype(jnp.int32), block_e.astype(jnp.int32), nvalid.astype(jnp.int32),
            next_e.astype(jnp.int32))


def kernel(x, norm1_g, w_in, lambda_q1, lambda_k1, lambda_q2, lambda_k2, subln_g, w_pool, pool_scale,
           w_out, norm2_g, w_group_router, b_group_router, w_expert_router, b_expert_router,
           w_gate, w_up, w_down, final_g):
    batch, seq, d = x.shape
    t = batch * seq
    qk_w = 2 * N_HEADS * HEAD_DIM
    av_w = N_HEADS * V_DIM
    l = 0
    assert seq % (2 * TK) == 0 and seq % TM_PROJ == 0 and t % TM_COMB == 0 and TK == TM_PROJ
    assert seq % (ATTN_QSUB * TQ) == 0 and ATTN_UNROLL % 2 == 0

    wi = w_in[l]
    wqt = _pad_head_cols(wi[:, :qk_w]).T.astype(BF16)
    wk = _pad_head_cols(wi[:, qk_w:2 * qk_w]).astype(BF16)
    wv = wi[:, 2 * qk_w:2 * qk_w + av_w].reshape(d, N_HEADS, V_DIM)
    wvt = jnp.pad(wv, ((0, 0), (0, 0), (0, VT_ROWS - V_DIM))).reshape(d, N_HEADS * VT_ROWS).T.astype(BF16)
    wu = wi[:, 2 * qk_w + av_w:].astype(BF16)

    qt, k, vt, u = _in_proj(x.reshape(t, d), norm1_g[l].reshape(1, d), wqt, wk, wvt, wu,
                            batch=batch, seq=seq)

    lam_in = jnp.zeros((8, LANES), F32)
    lam_in = lam_in.at[0, :HEAD_DIM].set(lambda_q1[l]).at[1, :HEAD_DIM].set(lambda_k1[l])
    lam_in = lam_in.at[2, :HEAD_DIM].set(lambda_q2[l]).at[3, :HEAD_DIM].set(lambda_k2[l])
    gsub = jnp.broadcast_to(subln_g[l].reshape(V_DIM, 1), (V_DIM, TQ)).astype(F32)
    a = _attention(lam_in, qt, k, vt, gsub, batch=batch, seq=seq)

    wr = jnp.zeros((d, 2 * LANES), F32)
    wr = wr.at[:, :N_GROUPS].set(w_group_router[l])
    wr = wr.at[:, LANES:LANES + N_EXPERTS].set(w_expert_router[l].reshape(d, N_EXPERTS)).astype(BF16)
    br = jnp.full((1, 2 * LANES), NEG, F32)
    br = br.at[0, :N_GROUPS].set(b_group_router[l])
    br = br.at[0, LANES:LANES + N_EXPERTS].set(b_expert_router[l].reshape(N_EXPERTS))
    h, xn2, gates, meta, counts = _mix_route(
        x, a, u.reshape(batch, seq, -1), w_pool[l].astype(BF16), pool_scale[l].reshape(1, -1),
        w_out[l].astype(BF16), norm2_g[l].reshape(1, d), wr, br, batch=batch, seq=seq)

    dest, block_e, nvalid, next_e = _route_plan(meta.reshape(t, LANES), counts, t)
    xs = _dispatch(dest, xn2.reshape(t, d // 2),
                   jnp.zeros((block_e.shape[0] * EXP_BLK, d // 2), jnp.uint32))
    ys = _experts(block_e, nvalid, next_e, xs, w_gate[l], w_up[l], w_down[l])
    out = _combine(dest, h.reshape(t, d), gates.reshape(t, LANES), ys, final_g.reshape(1, d))
    return out.reshape(batch, seq, d)
```

```python
import functools
import math

import jax
import jax.numpy as jnp
import numpy as np
from jax import lax
from jax.experimental import pallas as pl
from jax.experimental.pallas import tpu as pltpu

F32 = jnp.float32
BF16 = jnp.bfloat16

N_HEADS = 4
HEAD_DIM = 64
V_DIM = 128
POOL_WINDOWS = (2, 4, 8, 16)
N_GROUPS = 4
E_PER_GROUP = 8
N_EXPERTS = N_GROUPS * E_PER_GROUP
TOP_K = 2
EPS = 1e-6
LAM_INIT = 0.8 - 0.6 * math.exp(-0.3 * 0)
NEG = -1e30
LOG2E = math.log2(math.e)


def _bf16_terms(value, n):
    terms, rest = [], value
    for _ in range(n):
        term = float(np.asarray(rest, dtype=jnp.bfloat16))
        terms.append(term)
        rest -= term
    return tuple(terms)


LOG2E_TERMS = _bf16_terms(LOG2E, 3)
N_TERMS = len(LOG2E_TERMS)
SLOPES = tuple(2.0 ** (-8.0 * (h + 1) / N_HEADS) for h in range(N_HEADS))

LANES = 128
QK_PAD = 128
VT_ROWS = 144
POS_SPLIT = 128

TM_PROJ = 512
TQ = 512
TK = 512
ATTN_UNROLL = 6
ATTN_QSUB = 1
EXP_BLK = 512
TM_COMB = 256
VMEM_LIMIT = 56 * 1024 * 1024


def _nt_dot(a, b):
    return lax.dot_general(a, b, (((1,), (1,)), ((), ())), preferred_element_type=F32)


def _in_proj_kernel(x_ref, g_ref, wqt_ref, wk_ref, wvt_ref, wu_ref, kc_ref,
                    qt_ref, k_ref, vt_ref, u_ref, *, seq):
    tm = x_ref.shape[0]
    i = pl.program_id(0)
    j0 = (i % (seq // tm)) * tm
    x = x_ref[...]
    ms = jnp.mean(x * x, axis=-1, keepdims=True)
    hn = (x * lax.rsqrt(ms + EPS) * g_ref[...]).astype(BF16)

    qt = _nt_dot(wqt_ref[...], hn) * (LOG2E / math.sqrt(HEAD_DIM))
    zero_rows = jnp.zeros((HEAD_DIM, tm), BF16)
    for g in range(2 * N_HEADS):
        data = qt[g * HEAD_DIM:(g + 1) * HEAD_DIM].astype(BF16)
        lo_rows, hi_rows = (data, zero_rows) if g % 2 == 0 else (zero_rows, data)
        qt_ref[0, g, 0:HEAD_DIM, :] = lo_rows
        qt_ref[0, g, HEAD_DIM:QK_PAD, :] = hi_rows

    kk = jnp.dot(hn, wk_ref[...], preferred_element_type=F32)
    pos = j0 + lax.broadcasted_iota(jnp.int32, (tm, QK_PAD), 0)
    lane = lax.broadcasted_iota(jnp.int32, (tm, QK_PAD), 1)
    hi = (pos & ~(POS_SPLIT - 1)).astype(F32)
    lo = (pos & (POS_SPLIT - 1)).astype(F32)
    for c in range(2):
        f0 = _feature_base(c)
        f_hi, f_lo = f0 + 2 * N_TERMS, f0 + 3 * N_TERMS
        feat = jnp.where((lane >= f_hi) & (lane < f_lo), hi,
                         jnp.where((lane >= f_lo) & (lane < f_lo + N_TERMS), lo, 0.0))
        own = (lane < HEAD_DIM) if c == 0 else (lane >= HEAD_DIM)
        for h in range(N_HEADS):
            g = 2 * h + c
            dims = jnp.where(own, kk[:, h * QK_PAD:(h + 1) * QK_PAD], 0.0)
            k_ref[0, g] = (dims + (feat + kc_ref[g:g + 1, :])).astype(BF16)

    vt = _nt_dot(wvt_ref[...], hn)
    row = lax.broadcasted_iota(jnp.int32, (VT_ROWS, tm), 0)
    for h in range(N_HEADS):
        blk = vt[h * VT_ROWS:(h + 1) * VT_ROWS]
        vt_ref[0, h, 0] = jnp.where(row == V_DIM, 1.0, blk).astype(BF16)

    u_ref[...] = jnp.dot(hn, wu_ref[...], preferred_element_type=F32)


def _feature_base(c):
    return HEAD_DIM if c == 0 else 0


def _key_feature_consts():
    kc = np.zeros((2 * N_HEADS, QK_PAD), np.float32)
    for g in range(2 * N_HEADS):
        for rep in range(2):
            for n, term in enumerate(LOG2E_TERMS):
                kc[g, _feature_base(g % 2) + rep * N_TERMS + n] = term * SLOPES[g // 2]
    return jnp.asarray(kc)


def _in_proj(x2, g1, wqt, wk, wvt, wu, *, batch, seq):
    t, d = x2.shape
    tm = TM_PROJ
    nblk = seq // tm
    kern = functools.partial(_in_proj_kernel, seq=seq)
    full = lambda shape: pl.BlockSpec(shape, lambda i: (0,) * len(shape))
    kc = _key_feature_consts()
    return pl.pallas_call(
        kern,
        grid=(t // tm,),
        in_specs=[
            pl.BlockSpec((tm, d), lambda i: (i, 0)),
            full(g1.shape), full(wqt.shape), full(wk.shape), full(wvt.shape), full(wu.shape),
            full(kc.shape),
        ],
        out_specs=[
            pl.BlockSpec((1, 2 * N_HEADS, QK_PAD, tm), lambda i: (i // nblk, 0, 0, i % nblk)),
            pl.BlockSpec((1, 2 * N_HEADS, tm, QK_PAD), lambda i: (i // nblk, 0, i % nblk, 0)),
            pl.BlockSpec((1, N_HEADS, 1, VT_ROWS, tm), lambda i: (i // nblk, 0, i % nblk, 0, 0)),
            pl.BlockSpec((tm, wu.shape[1]), lambda i: (i, 0)),
        ],
        out_shape=[
            jax.ShapeDtypeStruct((batch, 2 * N_HEADS, QK_PAD, seq), BF16),
            jax.ShapeDtypeStruct((batch, 2 * N_HEADS, seq, QK_PAD), BF16),
            jax.ShapeDtypeStruct((batch, N_HEADS, nblk, VT_ROWS, tm), BF16),
            jax.ShapeDtypeStruct((t, wu.shape[1]), F32),
        ],
        compiler_params=pltpu.CompilerParams(
            dimension_semantics=("arbitrary",), vmem_limit_bytes=VMEM_LIMIT),
        name="in_proj",
    )(x2, g1, wqt, wk, wvt, wu, kc)


def _attention_kernel(lam_ref, qt_ref, k_ref, vt_ref, g_ref, o_ref,
                      qv_ref, acc_ref, m_ref, sa_ref, sb_ref, mba_ref, mbb_ref):
    tq = o_ref.shape[1] // ATTN_QSUB
    for sub in range(ATTN_QSUB):
        _attention_block(
            pl.program_id(2) * ATTN_QSUB + sub, qt_ref.at[:, :, :, sub * tq:(sub + 1) * tq], k_ref, vt_ref,
            qv_ref.at[sub], acc_ref.at[sub], m_ref.at[sub], sa_ref, sb_ref, mba_ref, mbb_ref)
    for sub in range(ATTN_QSUB):
        _attention_finish(lam_ref, g_ref, o_ref.at[:, sub * tq:(sub + 1) * tq, :], acc_ref.at[sub])


def _attention_block(qi, qt_ref, k_ref, vt_ref, qv_ref, acc_ref, m_ref, sa_ref, sb_ref, mba_ref, mbb_ref):
    h = pl.program_id(1)
    tq = qt_ref.shape[-1]
    nkb, _, tk = vt_ref.shape[2:]
    jdiag = (qi * tq) // tk

    slope = jnp.where(h == 0, SLOPES[0], jnp.where(h == 1, SLOPES[1],
                      jnp.where(h == 2, SLOPES[2], SLOPES[3]))).astype(F32)

    r = lax.broadcasted_iota(jnp.int32, (QK_PAD, tq), 0)
    ipos = qi * tq + lax.broadcasted_iota(jnp.int32, (QK_PAD, tq), 1)
    ihi = (ipos & ~(POS_SPLIT - 1)).astype(F32)
    ilo = (ipos & (POS_SPLIT - 1)).astype(F32)
    for c in range(2):
        f0 = _feature_base(c)
        fi = jnp.where((r >= f0) & (r < f0 + N_TERMS), ihi,
                       jnp.where((r >= f0 + N_TERMS) & (r < f0 + 2 * N_TERMS), ilo, 0.0))
        fj = jnp.zeros((QK_PAD, tq), F32)
        for rep in range(2):
            for n, term in enumerate(LOG2E_TERMS):
                fj = jnp.where(r == f0 + (2 + rep) * N_TERMS + n, term * slope, fj)
        q = qt_ref[0, c].astype(F32)
        qv_ref[0, c] = (q + fi - fj).astype(BF16)
        qv_ref[1, c] = (q - fi + fj).astype(BF16)

    acc_ref[...] = jnp.zeros(acc_ref.shape, F32)
    m_ref[...] = jnp.full(m_ref.shape, NEG, F32)

    def score_stage(kb, variant, s_ref, mb_ref, bias=None):
        for c in range(2):
            kblk = k_ref[0, c, pl.ds(pl.multiple_of(kb * tk, tk), tk), :]
            qmat = qt_ref[0, c] if variant is None else qv_ref[variant, c]
            st = jnp.dot(kblk, qmat, preferred_element_type=F32)
            if bias is not None:
                st = st + bias
            s_ref[c] = st
            mb_ref[c] = jnp.max(st, axis=0, keepdims=True)

    def softmax_stage(kb, s_ref, mb_ref):
        for c in range(2):
            m_old = m_ref[c]
            m_new = jnp.maximum(m_old, mb_ref[c])
            alpha = jnp.exp2(m_old - m_new)
            p = jnp.exp2(s_ref[c] - m_new).astype(BF16)
            pv = jnp.dot(vt_ref[0, 0, kb], p, preferred_element_type=F32)
            acc_ref[c] = acc_ref[c] * alpha + pv
            m_ref[c] = m_new

    def visit_block(n):
        o = n - 1
        return jnp.where(n == 0, jdiag, o + (o >= jdiag).astype(jnp.int32))

    def visit_variant(kb):
        return jnp.where(kb < jdiag, 1, 0)

    jpos = jdiag * tk + lax.broadcasted_iota(jnp.int32, (tk, tq), 0)
    iposd = qi * tq + lax.broadcasted_iota(jnp.int32, (tk, tq), 1)
    bias = (-LOG2E * slope) * jnp.abs(iposd - jpos).astype(F32)
    score_stage(jdiag, None, sa_ref, mba_ref, bias)

    bufs = ((sa_ref, mba_ref), (sb_ref, mbb_ref))

    def phase(n, parity, with_scores):
        if with_scores:
            kb_next = visit_block(n + 1)
            score_stage(kb_next, visit_variant(kb_next), *bufs[1 - parity])
        softmax_stage(visit_block(n), *bufs[parity])

    def loop_body(t, carry):
        for j in range(ATTN_UNROLL):
            phase(t * ATTN_UNROLL + j, j % 2, True)
        return carry

    trips = (nkb - 1) // ATTN_UNROLL
    lax.fori_loop(0, trips, loop_body, 0)
    for n in range(trips * ATTN_UNROLL, nkb):
        phase(n, n % 2, n + 1 < nkb)


def _attention_finish(lam_ref, g_ref, o_ref, acc_ref):
    lv = lam_ref[...]
    lam = (jnp.exp(jnp.sum(lv[0:1] * lv[1:2], axis=-1, keepdims=True))
           - jnp.exp(jnp.sum(lv[2:3] * lv[3:4], axis=-1, keepdims=True)) + LAM_INIT)
    a0 = acc_ref[0]
    a1 = acc_ref[1]
    o = (a0[:V_DIM] * (1.0 / a0[V_DIM:V_DIM + 1])
         - lam * (a1[:V_DIM] * (1.0 / a1[V_DIM:V_DIM + 1])))
    ms = jnp.mean(o * o, axis=0, keepdims=True)
    y = o * lax.rsqrt(ms + EPS) * g_ref[...] * (1.0 - LAM_INIT)
    o_ref[0] = y.T.astype(BF16)


def _attention(lam_in, qt, k, vt, gsub, *, batch, seq):
    nkb = seq // TK
    return pl.pallas_call(
        _attention_kernel,
        grid=(batch, N_HEADS, seq // (ATTN_QSUB * TQ)),
        in_specs=[
            pl.BlockSpec(lam_in.shape, lambda b, h, i: (0, 0)),
            pl.BlockSpec((1, 2, QK_PAD, ATTN_QSUB * TQ), lambda b, h, i: (b, h, 0, i)),
            pl.BlockSpec((1, 2, seq, QK_PAD), lambda b, h, i: (b, h, 0, 0)),
            pl.BlockSpec((1, 1, nkb, VT_ROWS, TK), lambda b, h, i: (b, h, 0, 0, 0)),
            pl.BlockSpec(gsub.shape, lambda b, h, i: (0, 0)),
        ],
        out_specs=pl.BlockSpec((1, ATTN_QSUB * TQ, V_DIM), lambda b, h, i: (b, i, h)),
        out_shape=jax.ShapeDtypeStruct((batch, seq, N_HEADS * V_DIM), BF16),
        scratch_shapes=[
            pltpu.VMEM((ATTN_QSUB, 2, 2, QK_PAD, TQ), BF16),
            pltpu.VMEM((ATTN_QSUB, 2, VT_ROWS, TQ), F32),
            pltpu.VMEM((ATTN_QSUB, 2, 1, TQ), F32),
            pltpu.VMEM((2, TK, TQ), F32),
            pltpu.VMEM((2, TK, TQ), F32),
            pltpu.VMEM((2, 1, TQ), F32),
            pltpu.VMEM((2, 1, TQ), F32),
        ],
        compiler_params=pltpu.CompilerParams(
            dimension_semantics=("arbitrary", "arbitrary", "arbitrary"),
            vmem_limit_bytes=VMEM_LIMIT),
        name="diff_attention",
    )(lam_in, qt, k, vt, gsub)


def _mix_route_kernel(x_ref, a_ref, u_ref, up_ref, un_ref, wp_ref, ps_ref, wo_ref, g2_ref,
                      wr_ref, br_ref, h_ref, xn_ref, gate_ref, id_ref, count_ref, cnt_ref, *, seq):
    tm = x_ref.shape[1]
    j0 = pl.program_id(1) * tm
    halo = up_ref.shape[1]

    ext = jnp.concatenate([up_ref[0], u_ref[0], un_ref[0]], axis=0)
    epos = j0 - halo + lax.broadcasted_iota(jnp.int32, (tm + 2 * halo, 1), 0)
    ext = jnp.where((epos >= 0) & (epos < seq), ext, 0.0)
    tpos = j0 + lax.broadcasted_iota(jnp.int32, (tm, 1), 0)
    ys = []
    for g, w in enumerate(POOL_WINDOWS):
        e = ext[:, g * LANES:(g + 1) * LANES]
        n = e.shape[0]
        s = e[1:] + e[:-1]
        width, start = 2, 0
        while width < w:
            s = s[:-width] + s[width:]
            width *= 2
        off = halo - w // 2
        win = s[off:off + tm]
        cnt = (jnp.minimum(tpos + w // 2, seq) - jnp.maximum(tpos - w // 2, 0)).astype(F32)
        d = (win / cnt - u_ref[0][:, g * LANES:(g + 1) * LANES]).astype(BF16)
        yg = jnp.dot(d, wp_ref[g], preferred_element_type=F32)
        ys.append(yg * ps_ref[:, g * LANES:(g + 1) * LANES])
        del n, start
    p = jnp.concatenate(ys, axis=-1).astype(BF16)

    mix = jnp.concatenate([a_ref[0], p], axis=-1)
    hcur = x_ref[0] + jnp.dot(mix, wo_ref[...], preferred_element_type=F32)
    h_ref[0] = hcur
    ms = jnp.mean(hcur * hcur, axis=-1, keepdims=True)
    xn = (hcur * lax.rsqrt(ms + EPS) * g2_ref[...]).astype(BF16)
    half = xn.shape[1] // 2
    lo = pltpu.bitcast(xn[:, :half].astype(F32), jnp.uint32) >> 16
    hi = pltpu.bitcast(xn[:, half:].astype(F32), jnp.uint32) & jnp.uint32(0xFFFF0000)
    xn_ref[0] = hi | lo

    logits = jnp.dot(xn, wr_ref[...], preferred_element_type=F32) + br_ref[...]
    gl = logits[:, :LANES]
    el = logits[:, LANES:]
    lane = lax.broadcasted_iota(jnp.int32, (tm, LANES), 1)
    gmax = jnp.max(gl, axis=-1, keepdims=True)
    gsel = jnp.min(jnp.where(gl == gmax, lane, LANES), axis=-1, keepdims=True)
    pg = 1.0 / jnp.sum(jnp.exp(gl - gmax), axis=-1, keepdims=True)
    elm = jnp.where((lane >> 3) == gsel, el, NEG)
    m1 = jnp.max(elm, axis=-1, keepdims=True)
    i1 = jnp.min(jnp.where(elm == m1, lane, LANES), axis=-1, keepdims=True)
    elm2 = jnp.where(lane == i1, NEG, elm)
    m2 = jnp.max(elm2, axis=-1, keepdims=True)
    i2 = jnp.min(jnp.where(elm2 == m2, lane, LANES), axis=-1, keepdims=True)
    e2 = jnp.exp(m2 - m1)
    t1 = 1.0 / (1.0 + e2)
    gate_ref[0] = jnp.where(lane == 0, pg * t1, jnp.where(lane == 1, pg * (e2 * t1), 0.0))

    @pl.when((pl.program_id(0) == 0) & (pl.program_id(1) == 0))
    def _():
        cnt_ref[...] = jnp.zeros(cnt_ref.shape, F32)

    hit1 = lane == i1
    hit2 = lane == i2
    onehot = jnp.where(hit1 | hit2, 1.0, 0.0)
    rr = lax.broadcasted_iota(jnp.int32, (tm, tm), 0)
    cc = lax.broadcasted_iota(jnp.int32, (tm, tm), 1)
    tril = jnp.where(cc < rr, 1.0, 0.0).astype(BF16)
    before = jnp.dot(tril, onehot.astype(BF16), preferred_element_type=F32) + cnt_ref[...]
    r1 = jnp.sum(jnp.where(hit1, before, 0.0), axis=-1, keepdims=True).astype(jnp.int32)
    r2 = jnp.sum(jnp.where(hit2, before, 0.0), axis=-1, keepdims=True).astype(jnp.int32)
    cnt_ref[...] = cnt_ref[...] + jnp.sum(onehot, axis=0, keepdims=True)
    count_ref[...] = cnt_ref[...]
    id_ref[0] = jnp.where(lane == 0, i1, jnp.where(lane == 1, i2,
                          jnp.where(lane == 2, r1, jnp.where(lane == 3, r2, 0))))


def _mix_route(x, a, u, wp, ps, wo, g2, wr, br, *, batch, seq):
    tm = TM_PROJ
    d = x.shape[-1]
    halo = 8
    nh = tm // halo
    kern = functools.partial(_mix_route_kernel, seq=seq)
    full = lambda arr: pl.BlockSpec(arr.shape, lambda b, i: (0,) * arr.ndim)
    tile = lambda w: pl.BlockSpec((1, tm, w), lambda b, i: (b, i, 0))
    return pl.pallas_call(
        kern,
        grid=(batch, seq // tm),
        in_specs=[
            tile(d), tile(a.shape[-1]), tile(u.shape[-1]),
            pl.BlockSpec((1, halo, u.shape[-1]), lambda b, i: (b, jnp.maximum(i * nh - 1, 0), 0)),
            pl.BlockSpec((1, halo, u.shape[-1]),
                         lambda b, i: (b, jnp.minimum((i + 1) * nh, seq // halo - 1), 0)),
            full(wp), full(ps), full(wo), full(g2), full(wr), full(br),
        ],
        out_specs=[tile(d), tile(d // 2), tile(LANES), tile(LANES),
                   pl.BlockSpec((1, LANES), lambda b, i: (0, 0))],
        out_shape=[
            jax.ShapeDtypeStruct((batch, seq, d), F32),
            jax.ShapeDtypeStruct((batch, seq, d // 2), jnp.uint32),
            jax.ShapeDtypeStruct((batch, seq, LANES), F32),
            jax.ShapeDtypeStruct((batch, seq, LANES), jnp.int32),
            jax.ShapeDtypeStruct((1, LANES), F32),
        ],
        scratch_shapes=[pltpu.VMEM((1, LANES), F32)],
        compiler_params=pltpu.CompilerParams(
            dimension_semantics=("arbitrary", "arbitrary"), vmem_limit_bytes=VMEM_LIMIT),
        name="mix_route",
    )(x, a, u, u, u, wp, ps, wo, g2, wr, br)


def _dispatch_kernel(dest_ref, x_ref, xs_in, xs_hbm, sem):
    del xs_in
    tm = x_ref.shape[0]

    for r in range(tm):
        for k in range(TOP_K):
            pltpu.make_async_copy(x_ref.at[r], xs_hbm.at[dest_ref[0, k, r]], sem).start(priority=k)
    for k in range(TOP_K):
        pltpu.make_async_copy(x_ref, xs_hbm.at[pl.ds(0, tm)], sem).wait()


def _dispatch(dest, xn2, xs_zero):
    t, d = xn2.shape
    tm = TM_COMB
    nt = t // tm
    return pl.pallas_call(
        _dispatch_kernel,
        grid=(nt,),
        in_specs=[
            pl.BlockSpec((1, TOP_K, tm), lambda i: (i, 0, 0), memory_space=pltpu.SMEM),
            pl.BlockSpec((tm, d), lambda i: (i, 0)),
            pl.BlockSpec(memory_space=pl.ANY),
        ],
        out_specs=pl.BlockSpec(memory_space=pl.ANY),
        out_shape=jax.ShapeDtypeStruct(xs_zero.shape, xs_zero.dtype),
        input_output_aliases={2: 0},
        scratch_shapes=[pltpu.SemaphoreType.DMA(())],
        compiler_params=pltpu.CompilerParams(
            dimension_semantics=("arbitrary",), vmem_limit_bytes=VMEM_LIMIT),
        name="dispatch",
    )(dest.reshape(nt, tm, TOP_K).transpose(0, 2, 1), xn2, xs_zero)


def _experts_kernel(be_ref, nv_ref, nx_ref, xs_ref, wg_hbm, wu_hbm, wd_hbm, y_ref,
                    wgf, wuf, wdf, wgb, wub, wdb, slot_ref, sem):
    i = pl.program_id(0)
    changed = jnp.logical_or(i == 0, be_ref[i] != be_ref[jnp.maximum(i - 1, 0)])

    def weight_copies(e, slot):
        return (pltpu.make_async_copy(wg_hbm.at[e], wgf.at[slot], sem.at[slot]),
                pltpu.make_async_copy(wu_hbm.at[e], wuf.at[slot], sem.at[slot]),
                pltpu.make_async_copy(wd_hbm.at[e], wdf.at[slot], sem.at[slot]))

    @pl.when(i == 0)
    def _():
        slot_ref[0] = 0
        for cp in weight_copies(be_ref[0], 0):
            cp.start()

    @pl.when(changed)
    def _():
        slot = jnp.where(i == 0, 0, 1 - slot_ref[0])
        slot_ref[0] = slot
        for cp in weight_copies(be_ref[i], slot):
            cp.wait()
        wgb[...] = wgf[slot].astype(BF16)
        wub[...] = wuf[slot].astype(BF16)
        wdb[...] = wdf[slot].astype(BF16)

        @pl.when(nx_ref[i] >= 0)
        def _():
            for cp in weight_copies(nx_ref[i], 1 - slot):
                cp.start()

    blk = xs_ref.shape[0]
    half = blk // 2
    nv = nv_ref[i]

    def mlp(rows):
        words = xs_ref[0:rows, :]
        lo = pltpu.bitcast(words << 16, F32)
        hi = pltpu.bitcast(words & jnp.uint32(0xFFFF0000), F32)
        xb = jnp.concatenate([lo, hi], axis=1).astype(BF16)
        gt = jnp.dot(xb, wgb[...], preferred_element_type=F32)
        up = jnp.dot(xb, wub[...], preferred_element_type=F32)
        hid = (gt * (1.0 / (1.0 + jnp.exp(-gt))) * up).astype(BF16)
        y_ref[0:rows, :] = jnp.dot(hid, wdb[...], preferred_element_type=F32)

    @pl.when(nv > half)
    def _():
        mlp(blk)

    @pl.when(jnp.logical_and(nv > 0, nv <= half))
    def _():
        mlp(half)
        y_ref[half:, :] = jnp.zeros((blk - half, y_ref.shape[1]), F32)

    @pl.when(nv == 0)
    def _():
        y_ref[...] = jnp.zeros(y_ref.shape, F32)


def _experts(block_e, nvalid, next_e, xs, w_gate, w_up, w_down):
    nb = block_e.shape[0]
    d, de = w_gate.shape[-2:]
    grid_spec = pltpu.PrefetchScalarGridSpec(
        num_scalar_prefetch=3,
        grid=(nb,),
        in_specs=[
            pl.BlockSpec((EXP_BLK, xs.shape[-1]), lambda i, be, nv, nx: (i, 0)),
            pl.BlockSpec(memory_space=pl.ANY),
            pl.BlockSpec(memory_space=pl.ANY),
            pl.BlockSpec(memory_space=pl.ANY),
        ],
        out_specs=pl.BlockSpec((EXP_BLK, d), lambda i, be, nv, nx: (i, 0)),
        scratch_shapes=[
            pltpu.VMEM((2, d, de), F32), pltpu.VMEM((2, d, de), F32), pltpu.VMEM((2, de, d), F32),
            pltpu.VMEM((d, de), BF16), pltpu.VMEM((d, de), BF16), pltpu.VMEM((de, d), BF16),
            pltpu.SMEM((1,), jnp.int32), pltpu.SemaphoreType.DMA((2,)),
        ],
    )
    return pl.pallas_call(
        _experts_kernel,
        grid_spec=grid_spec,
        out_shape=jax.ShapeDtypeStruct((nb * EXP_BLK, d), F32),
        compiler_params=pltpu.CompilerParams(
            dimension_semantics=("arbitrary",), vmem_limit_bytes=VMEM_LIMIT),
        name="experts",
    )(block_e, nvalid, next_e, xs, w_gate, w_up, w_down)


def _combine_kernel(dest_ref, dnext_ref, h_ref, gate_ref, y_hbm, g_ref, o_ref, ybuf, sem):
    tm = h_ref.shape[0]
    i = pl.program_id(0)
    slot = i % 2

    def start_rows(idx_ref, s):
        for r in range(tm):
            for k in range(TOP_K):
                pltpu.make_async_copy(y_hbm.at[idx_ref[0, k, r]], ybuf.at[s, k, r], sem.at[s]).start(priority=k)

    @pl.when(i == 0)
    def _():
        start_rows(dest_ref, 0)

    @pl.when(i + 1 < pl.num_programs(0))
    def _():
        start_rows(dnext_ref, 1 - slot)

    for k in range(TOP_K):
        pltpu.make_async_copy(y_hbm.at[pl.ds(0, tm)], ybuf.at[slot, k], sem.at[slot]).wait()
    gates = gate_ref[...]
    hcur = h_ref[...] + (gates[:, 0:1] * ybuf[slot, 0] + gates[:, 1:2] * ybuf[slot, 1])
    ms = jnp.mean(hcur * hcur, axis=-1, keepdims=True)
    o_ref[...] = hcur * lax.rsqrt(ms + EPS) * g_ref[...]


def _combine(dest, h2, gates, ys, gfin):
    t, d = h2.shape
    tm = TM_COMB
    nt = t // tm
    dest3 = dest.reshape(nt, tm, TOP_K).transpose(0, 2, 1)
    return pl.pallas_call(
        _combine_kernel,
        grid=(nt,),
        in_specs=[
            pl.BlockSpec((1, TOP_K, tm), lambda i: (i, 0, 0), memory_space=pltpu.SMEM),
            pl.BlockSpec((1, TOP_K, tm), lambda i: (jnp.minimum(i + 1, nt - 1), 0, 0),
                         memory_space=pltpu.SMEM),
            pl.BlockSpec((tm, d), lambda i: (i, 0)),
            pl.BlockSpec((tm, LANES), lambda i: (i, 0)),
            pl.BlockSpec(memory_space=pl.ANY),
            pl.BlockSpec(gfin.shape, lambda i: (0, 0)),
        ],
        out_specs=pl.BlockSpec((tm, d), lambda i: (i, 0)),
        out_shape=jax.ShapeDtypeStruct((t, d), F32),
        scratch_shapes=[pltpu.VMEM((2, TOP_K, tm, d), F32), pltpu.SemaphoreType.DMA((2,))],
        compiler_params=pltpu.CompilerParams(
            dimension_semantics=("arbitrary",), vmem_limit_bytes=VMEM_LIMIT),
        name="combine",
    )(dest3, dest3, h2, gates, ys, gfin)


def _route_plan(meta, counts, t):
    ids = meta[:, :TOP_K]
    ranks = meta[:, TOP_K:2 * TOP_K]
    cnt = counts[0, :N_EXPERTS].astype(jnp.int32)
    padded = (cnt + EXP_BLK - 1) // EXP_BLK * EXP_BLK
    pad_ends = jnp.cumsum(padded)
    pad_starts = pad_ends - padded
    experts = jnp.arange(N_EXPERTS, dtype=jnp.int32)
    dest = jnp.sum(jnp.where(ids[..., None] == experts, pad_starts, 0), axis=-1) + ranks
    nb = (t * TOP_K + EXP_BLK - 1) // EXP_BLK + N_EXPERTS
    bstart = jnp.arange(nb, dtype=jnp.int32) * EXP_BLK
    last_e = jnp.max(jnp.where(cnt > 0, experts, 0))
    block_e = jnp.minimum(jnp.sum((pad_ends[None, :] <= bstart[:, None]).astype(jnp.int32), axis=1), last_e)
    row_end = jnp.sum(jnp.where(block_e[:, None] == experts, pad_starts + cnt, 0), axis=-1)
    nvalid = jnp.clip(row_end - bstart, 0, EXP_BLK)
    later = (experts[None, :] > experts[:, None]) & (cnt[None, :] > 0)
    next_of = jnp.min(jnp.where(later, experts[None, :], N_EXPERTS), axis=1)
    next_of = jnp.where(next_of == N_EXPERTS, -1, next_of)
    next_e = jnp.sum(jnp.where(block_e[:, None] == experts, next_of, 0), axis=-1)
    return (dest.astype(jnp.int32), block_e.astype(jnp.int32), nvalid.astype(jnp.int32),
            next_e.astype(jnp.int32))


def kernel(x, norm1_g, w_in, lambda_q1, lambda_k1, lambda_q2, lambda_k2, subln_g, w_pool, pool_scale,
           w_out, norm2_g, w_group_router, b_group_router, w_expert_router, b_expert_router,
           w_gate, w_up, w_down, final_g):
    batch, seq, d = x.shape
    t = batch * seq
    qk_w = 2 * N_HEADS * HEAD_DIM
    av_w = N_HEADS * V_DIM
    l = 0
    assert seq % (2 * TK) == 0 and seq % TM_PROJ == 0 and t % TM_COMB == 0 and TK == TM_PROJ
    assert seq % (ATTN_QSUB * TQ) == 0 and ATTN_UNROLL % 2 == 0

    wi = w_in[l]
    wqt = wi[:, :qk_w].T.astype(BF16)
    wk = wi[:, qk_w:2 * qk_w].astype(BF16)
    wv = wi[:, 2 * qk_w:2 * qk_w + av_w].reshape(d, N_HEADS, V_DIM)
    wvt = jnp.pad(wv, ((0, 0), (0, 0), (0, VT_ROWS - V_DIM))).reshape(d, N_HEADS * VT_ROWS).T.astype(BF16)
    wu = wi[:, 2 * qk_w + av_w:].astype(BF16)

    qt, k, vt, u = _in_proj(x.reshape(t, d), norm1_g[l].reshape(1, d), wqt, wk, wvt, wu,
                            batch=batch, seq=seq)

    lam_in = jnp.zeros((8, LANES), F32)
    lam_in = lam_in.at[0, :HEAD_DIM].set(lambda_q1[l]).at[1, :HEAD_DIM].set(lambda_k1[l])
    lam_in = lam_in.at[2, :HEAD_DIM].set(lambda_q2[l]).at[3, :HEAD_DIM].set(lambda_k2[l])
    gsub = jnp.broadcast_to(subln_g[l].reshape(V_DIM, 1), (V_DIM, TQ)).astype(F32)
    a = _attention(lam_in, qt, k, vt, gsub, batch=batch, seq=seq)

    wr = jnp.zeros((d, 2 * LANES), F32)
    wr = wr.at[:, :N_GROUPS].set(w_group_router[l])
    wr = wr.at[:, LANES:LANES + N_EXPERTS].set(w_expert_router[l].reshape(d, N_EXPERTS)).astype(BF16)
    br = jnp.full((1, 2 * LANES), NEG, F32)
    br = br.at[0, :N_GROUPS].set(b_group_router[l])
    br = br.at[0, LANES:LANES + N_EXPERTS].set(b_expert_router[l].reshape(N_EXPERTS))
    h, xn2, gates, meta, counts = _mix_route(
        x, a, u.reshape(batch, seq, -1), w_pool[l].astype(BF16), pool_scale[l].reshape(1, -1),
        w_out[l].astype(BF16), norm2_g[l].reshape(1, d), wr, br, batch=batch, seq=seq)

    dest, block_e, nvalid, next_e = _route_plan(meta.reshape(t, LANES), counts, t)
    xs = _dispatch(dest, xn2.reshape(t, d // 2),
                   jnp.zeros((block_e.shape[0] * EXP_BLK, d // 2), jnp.uint32))
    ys = _experts(block_e, nvalid, next_e, xs, w_gate[l], w_up[l], w_down[l])
    out = _combine(dest, h.reshape(t, d), gates.reshape(t, LANES), ys, final_g.reshape(1, d))
    return out.reshape(batch, seq, d)
```

```python
import functools
import math

import jax
import jax.numpy as jnp
import numpy as np
from jax import lax
from jax.experimental import pallas as pl
from jax.experimental.pallas import tpu as pltpu

F32 = jnp.float32
BF16 = jnp.bfloat16

N_HEADS = 4
HEAD_DIM = 64
V_DIM = 128
POOL_WINDOWS = (2, 4, 8, 16)
N_GROUPS = 4
E_PER_GROUP = 8
N_EXPERTS = N_GROUPS * E_PER_GROUP
TOP_K = 2
EPS = 1e-6
LAM_INIT = 0.8 - 0.6 * math.exp(-0.3 * 0)
NEG = -1e30
LOG2E = math.log2(math.e)


def _bf16_terms(value, n):
    terms, rest = [], value
    for _ in range(n):
        term = float(np.asarray(rest, dtype=jnp.bfloat16))
        terms.append(term)
        rest -= term
    return tuple(terms)


LOG2E_TERMS = _bf16_terms(LOG2E, 3)
N_TERMS = len(LOG2E_TERMS)
SLOPES = tuple(2.0 ** (-8.0 * (h + 1) / N_HEADS) for h in range(N_HEADS))

LANES = 128
QK_PAD = 128
VT_ROWS = 144
POS_SPLIT = 128

TM_PROJ = 512
TQ = 512
TK = 512
ATTN_UNROLL = 6
ATTN_QSUB = 1
EXP_BLK = 512
TM_DISP = 1024
TM_COMB = 512
VMEM_LIMIT = 56 * 1024 * 1024


def _nt_dot(a, b):
    return lax.dot_general(a, b, (((1,), (1,)), ((), ())), preferred_element_type=F32)


def _in_proj_kernel(x_ref, g_ref, wqt_ref, wk_ref, wvt_ref, wu_ref, kc_ref,
                    qt_ref, k_ref, vt_ref, u_ref, *, seq):
    tm = x_ref.shape[0]
    i = pl.program_id(0)
    j0 = (i % (seq // tm)) * tm
    x = x_ref[...]
    ms = jnp.mean(x * x, axis=-1, keepdims=True)
    hn = (x * lax.rsqrt(ms + EPS) * g_ref[...]).astype(BF16)

    qt = _nt_dot(wqt_ref[...], hn) * (LOG2E / math.sqrt(HEAD_DIM))
    zero_rows = jnp.zeros((HEAD_DIM, tm), BF16)
    for g in range(2 * N_HEADS):
        data = qt[g * HEAD_DIM:(g + 1) * HEAD_DIM].astype(BF16)
        lo_rows, hi_rows = (data, zero_rows) if g % 2 == 0 else (zero_rows, data)
        qt_ref[0, g, 0:HEAD_DIM, :] = lo_rows
        qt_ref[0, g, HEAD_DIM:QK_PAD, :] = hi_rows

    kk = jnp.dot(hn, wk_ref[...], preferred_element_type=F32)
    pos = j0 + lax.broadcasted_iota(jnp.int32, (tm, QK_PAD), 0)
    lane = lax.broadcasted_iota(jnp.int32, (tm, QK_PAD), 1)
    hi = (pos & ~(POS_SPLIT - 1)).astype(F32)
    lo = (pos & (POS_SPLIT - 1)).astype(F32)
    for c in range(2):
        f0 = _feature_base(c)
        f_hi, f_lo = f0 + 2 * N_TERMS, f0 + 3 * N_TERMS
        feat = jnp.where((lane >= f_hi) & (lane < f_lo), hi,
                         jnp.where((lane >= f_lo) & (lane < f_lo + N_TERMS), lo, 0.0))
        own = (lane < HEAD_DIM) if c == 0 else (lane >= HEAD_DIM)
        for h in range(N_HEADS):
            g = 2 * h + c
            dims = jnp.where(own, kk[:, h * QK_PAD:(h + 1) * QK_PAD], 0.0)
            k_ref[0, g] = (dims + (feat + kc_ref[g:g + 1, :])).astype(BF16)

    vt = _nt_dot(wvt_ref[...], hn)
    row = lax.broadcasted_iota(jnp.int32, (VT_ROWS, tm), 0)
    for h in range(N_HEADS):
        blk = vt[h * VT_ROWS:(h + 1) * VT_ROWS]
        vt_ref[0, h, 0] = jnp.where(row == V_DIM, 1.0, blk).astype(BF16)

    u_ref[...] = jnp.dot(hn, wu_ref[...], preferred_element_type=F32)


def _feature_base(c):
    return HEAD_DIM if c == 0 else 0


def _key_feature_consts():
    kc = np.zeros((2 * N_HEADS, QK_PAD), np.float32)
    for g in range(2 * N_HEADS):
        for rep in range(2):
            for n, term in enumerate(LOG2E_TERMS):
                kc[g, _feature_base(g % 2) + rep * N_TERMS + n] = term * SLOPES[g // 2]
    return jnp.asarray(kc)


def _in_proj(x2, g1, wqt, wk, wvt, wu, *, batch, seq):
    t, d = x2.shape
    tm = TM_PROJ
    nblk = seq // tm
    kern = functools.partial(_in_proj_kernel, seq=seq)
    full = lambda shape: pl.BlockSpec(shape, lambda i: (0,) * len(shape))
    kc = _key_feature_consts()
    return pl.pallas_call(
        kern,
        grid=(t // tm,),
        in_specs=[
            pl.BlockSpec((tm, d), lambda i: (i, 0)),
            full(g1.shape), full(wqt.shape), full(wk.shape), full(wvt.shape), full(wu.shape),
            full(kc.shape),
        ],
        out_specs=[
            pl.BlockSpec((1, 2 * N_HEADS, QK_PAD, tm), lambda i: (i // nblk, 0, 0, i % nblk)),
            pl.BlockSpec((1, 2 * N_HEADS, tm, QK_PAD), lambda i: (i // nblk, 0, i % nblk, 0)),
            pl.BlockSpec((1, N_HEADS, 1, VT_ROWS, tm), lambda i: (i // nblk, 0, i % nblk, 0, 0)),
            pl.BlockSpec((tm, wu.shape[1]), lambda i: (i, 0)),
        ],
        out_shape=[
            jax.ShapeDtypeStruct((batch, 2 * N_HEADS, QK_PAD, seq), BF16),
            jax.ShapeDtypeStruct((batch, 2 * N_HEADS, seq, QK_PAD), BF16),
            jax.ShapeDtypeStruct((batch, N_HEADS, nblk, VT_ROWS, tm), BF16),
            jax.ShapeDtypeStruct((t, wu.shape[1]), F32),
        ],
        compiler_params=pltpu.CompilerParams(
            dimension_semantics=("arbitrary",), vmem_limit_bytes=VMEM_LIMIT),
        name="in_proj",
    )(x2, g1, wqt, wk, wvt, wu, kc)


def _attention_kernel(lam_ref, qt_ref, k_ref, vt_ref, g_ref, o_ref,
                      qv_ref, acc_ref, m_ref, sa_ref, sb_ref, mba_ref, mbb_ref):
    tq = o_ref.shape[1] // ATTN_QSUB
    for sub in range(ATTN_QSUB):
        _attention_block(
            pl.program_id(2) * ATTN_QSUB + sub, qt_ref.at[:, :, :, sub * tq:(sub + 1) * tq], k_ref, vt_ref,
            qv_ref.at[sub], acc_ref.at[sub], m_ref.at[sub], sa_ref, sb_ref, mba_ref, mbb_ref)
    for sub in range(ATTN_QSUB):
        _attention_finish(lam_ref, g_ref, o_ref.at[:, sub * tq:(sub + 1) * tq, :], acc_ref.at[sub])


def _attention_block(qi, qt_ref, k_ref, vt_ref, qv_ref, acc_ref, m_ref, sa_ref, sb_ref, mba_ref, mbb_ref):
    h = pl.program_id(1)
    tq = qt_ref.shape[-1]
    nkb, _, tk = vt_ref.shape[2:]
    jdiag = (qi * tq) // tk

    slope = jnp.where(h == 0, SLOPES[0], jnp.where(h == 1, SLOPES[1],
                      jnp.where(h == 2, SLOPES[2], SLOPES[3]))).astype(F32)

    r = lax.broadcasted_iota(jnp.int32, (QK_PAD, tq), 0)
    ipos = qi * tq + lax.broadcasted_iota(jnp.int32, (QK_PAD, tq), 1)
    ihi = (ipos & ~(POS_SPLIT - 1)).astype(F32)
    ilo = (ipos & (POS_SPLIT - 1)).astype(F32)
    for c in range(2):
        f0 = _feature_base(c)
        fi = jnp.where((r >= f0) & (r < f0 + N_TERMS), ihi,
                       jnp.where((r >= f0 + N_TERMS) & (r < f0 + 2 * N_TERMS), ilo, 0.0))
        fj = jnp.zeros((QK_PAD, tq), F32)
        for rep in range(2):
            for n, term in enumerate(LOG2E_TERMS):
                fj = jnp.where(r == f0 + (2 + rep) * N_TERMS + n, term * slope, fj)
        q = qt_ref[0, c].astype(F32)
        qv_ref[0, c] = (q + fi - fj).astype(BF16)
        qv_ref[1, c] = (q - fi + fj).astype(BF16)

    acc_ref[...] = jnp.zeros(acc_ref.shape, F32)
    m_ref[...] = jnp.full(m_ref.shape, NEG, F32)

    def score_stage(kb, variant, s_ref, mb_ref, bias=None):
        for c in range(2):
            kblk = k_ref[0, c, pl.ds(pl.multiple_of(kb * tk, tk), tk), :]
            qmat = qt_ref[0, c] if variant is None else qv_ref[variant, c]
            st = jnp.dot(kblk, qmat, preferred_element_type=F32)
            if bias is not None:
                st = st + bias
            s_ref[c] = st
            mb_ref[c] = jnp.max(st, axis=0, keepdims=True)

    def softmax_stage(kb, s_ref, mb_ref):
        for c in range(2):
            m_old = m_ref[c]
            m_new = jnp.maximum(m_old, mb_ref[c])
            alpha = jnp.exp2(m_old - m_new)
            p = jnp.exp2(s_ref[c] - m_new).astype(BF16)
            pv = jnp.dot(vt_ref[0, 0, kb], p, preferred_element_type=F32)
            acc_ref[c] = acc_ref[c] * alpha + pv
            m_ref[c] = m_new

    def visit_block(n):
        o = n - 1
        return jnp.where(n == 0, jdiag, o + (o >= jdiag).astype(jnp.int32))

    def visit_variant(kb):
        return jnp.where(kb < jdiag, 1, 0)

    jpos = jdiag * tk + lax.broadcasted_iota(jnp.int32, (tk, tq), 0)
    iposd = qi * tq + lax.broadcasted_iota(jnp.int32, (tk, tq), 1)
    bias = (-LOG2E * slope) * jnp.abs(iposd - jpos).astype(F32)
    score_stage(jdiag, None, sa_ref, mba_ref, bias)

    bufs = ((sa_ref, mba_ref), (sb_ref, mbb_ref))

    def phase(n, parity, with_scores):
        if with_scores:
            kb_next = visit_block(n + 1)
            score_stage(kb_next, visit_variant(kb_next), *bufs[1 - parity])
        softmax_stage(visit_block(n), *bufs[parity])

    def loop_body(t, carry):
        for j in range(ATTN_UNROLL):
            phase(t * ATTN_UNROLL + j, j % 2, True)
        return carry

    trips = (nkb - 1) // ATTN_UNROLL
    lax.fori_loop(0, trips, loop_body, 0)
    for n in range(trips * ATTN_UNROLL, nkb):
        phase(n, n % 2, n + 1 < nkb)


def _attention_finish(lam_ref, g_ref, o_ref, acc_ref):
    lv = lam_ref[...]
    lam = (jnp.exp(jnp.sum(lv[0:1] * lv[1:2], axis=-1, keepdims=True))
           - jnp.exp(jnp.sum(lv[2:3] * lv[3:4], axis=-1, keepdims=True)) + LAM_INIT)
    a0 = acc_ref[0]
    a1 = acc_ref[1]
    o = (a0[:V_DIM] * (1.0 / a0[V_DIM:V_DIM + 1])
         - lam * (a1[:V_DIM] * (1.0 / a1[V_DIM:V_DIM + 1])))
    ms = jnp.mean(o * o, axis=0, keepdims=True)
    y = o * lax.rsqrt(ms + EPS) * g_ref[...] * (1.0 - LAM_INIT)
    o_ref[0] = y.T.astype(BF16)


def _attention(lam_in, qt, k, vt, gsub, *, batch, seq):
    nkb = seq // TK
    return pl.pallas_call(
        _attention_kernel,
        grid=(batch, N_HEADS, seq // (ATTN_QSUB * TQ)),
        in_specs=[
            pl.BlockSpec(lam_in.shape, lambda b, h, i: (0, 0)),
            pl.BlockSpec((1, 2, QK_PAD, ATTN_QSUB * TQ), lambda b, h, i: (b, h, 0, i)),
            pl.BlockSpec((1, 2, seq, QK_PAD), lambda b, h, i: (b, h, 0, 0)),
            pl.BlockSpec((1, 1, nkb, VT_ROWS, TK), lambda b, h, i: (b, h, 0, 0, 0)),
            pl.BlockSpec(gsub.shape, lambda b, h, i: (0, 0)),
        ],
        out_specs=pl.BlockSpec((1, ATTN_QSUB * TQ, V_DIM), lambda b, h, i: (b, i, h)),
        out_shape=jax.ShapeDtypeStruct((batch, seq, N_HEADS * V_DIM), BF16),
        scratch_shapes=[
            pltpu.VMEM((ATTN_QSUB, 2, 2, QK_PAD, TQ), BF16),
            pltpu.VMEM((ATTN_QSUB, 2, VT_ROWS, TQ), F32),
            pltpu.VMEM((ATTN_QSUB, 2, 1, TQ), F32),
            pltpu.VMEM((2, TK, TQ), F32),
            pltpu.VMEM((2, TK, TQ), F32),
            pltpu.VMEM((2, 1, TQ), F32),
            pltpu.VMEM((2, 1, TQ), F32),
        ],
        compiler_params=pltpu.CompilerParams(
            dimension_semantics=("arbitrary", "arbitrary", "arbitrary"),
            vmem_limit_bytes=VMEM_LIMIT),
        name="diff_attention",
    )(lam_in, qt, k, vt, gsub)


def _mix_route_kernel(x_ref, a_ref, u_ref, up_ref, un_ref, wp_ref, ps_ref, wo_ref, g2_ref,
                      wr_ref, br_ref, h_ref, xn_ref, gate_ref, id_ref, count_ref, cnt_ref, tril_ref,
                      *, seq):
    tm = x_ref.shape[1]
    j0 = pl.program_id(1) * tm
    halo = up_ref.shape[1]

    @pl.when((pl.program_id(0) == 0) & (pl.program_id(1) == 0))
    def _():
        cnt_ref[...] = jnp.zeros(cnt_ref.shape, F32)
        rr = lax.broadcasted_iota(jnp.int32, (tm, tm), 0)
        cc = lax.broadcasted_iota(jnp.int32, (tm, tm), 1)
        tril_ref[...] = jnp.where(cc < rr, 1.0, 0.0).astype(BF16)

    n = tm + 2 * halo
    ext = jnp.concatenate([jnp.where(j0 > 0, up_ref[0], 0.0), u_ref[0],
                           jnp.where(j0 + tm < seq, un_ref[0], 0.0)], axis=0)
    edge = lax.broadcasted_iota(jnp.int32, (halo, LANES), 0)
    tpos_top = j0 + edge
    tpos_bot = j0 + (tm - halo) + edge
    ys = []
    for g, w in enumerate(POOL_WINDOWS):
        fwd = ext[:, g * LANES:(g + 1) * LANES]
        length = 1
        while 2 * length < w:
            fwd = fwd + pltpu.roll(fwd, n - length, 0)
            length *= 2
        win = (fwd + pltpu.roll(fwd, w // 2, 0))[halo:halo + tm]

        def clipped(tpos, w=w):
            return (jnp.minimum(tpos + w // 2, seq) - jnp.maximum(tpos - w // 2, 0)).astype(F32)

        mean = jnp.concatenate([win[:halo] / clipped(tpos_top), win[halo:tm - halo] * (1.0 / w),
                                win[tm - halo:] / clipped(tpos_bot)], axis=0)
        d = (mean - u_ref[0][:, g * LANES:(g + 1) * LANES]).astype(BF16)
        yg = jnp.dot(d, wp_ref[g], preferred_element_type=F32)
        ys.append(yg * ps_ref[:, g * LANES:(g + 1) * LANES])
    p = jnp.concatenate(ys, axis=-1).astype(BF16)

    mix = jnp.concatenate([a_ref[0], p], axis=-1)
    hcur = x_ref[0] + jnp.dot(mix, wo_ref[...], preferred_element_type=F32)
    h_ref[0] = hcur
    ms = jnp.mean(hcur * hcur, axis=-1, keepdims=True)
    xn = (hcur * lax.rsqrt(ms + EPS) * g2_ref[...]).astype(BF16)
    half = xn.shape[1] // 2
    lo = pltpu.bitcast(xn[:, :half].astype(F32), jnp.uint32) >> 16
    hi = pltpu.bitcast(xn[:, half:].astype(F32), jnp.uint32) & jnp.uint32(0xFFFF0000)
    xn_ref[0] = hi | lo

    logits = jnp.dot(xn, wr_ref[...], preferred_element_type=F32) + br_ref[...]
    gl = logits[:, :LANES]
    el = logits[:, LANES:]
    lane = lax.broadcasted_iota(jnp.int32, (tm, LANES), 1)
    gmax = jnp.max(gl, axis=-1, keepdims=True)
    gsel = jnp.min(jnp.where(gl == gmax, lane, LANES), axis=-1, keepdims=True)
    pg = 1.0 / jnp.sum(jnp.exp(gl - gmax), axis=-1, keepdims=True)
    elm = jnp.where((lane >> 3) == gsel, el, NEG)
    m1 = jnp.max(elm, axis=-1, keepdims=True)
    i1 = jnp.min(jnp.where(elm == m1, lane, LANES), axis=-1, keepdims=True)
    elm2 = jnp.where(lane == i1, NEG, elm)
    m2 = jnp.max(elm2, axis=-1, keepdims=True)
    i2 = jnp.min(jnp.where(elm2 == m2, lane, LANES), axis=-1, keepdims=True)
    e2 = jnp.exp(m2 - m1)
    t1 = 1.0 / (1.0 + e2)
    gate_ref[0] = jnp.where(lane == 0, pg * t1, jnp.where(lane == 1, pg * (e2 * t1), 0.0))

    hit1 = lane == i1
    hit2 = lane == i2
    onehot = jnp.where(hit1 | hit2, 1.0, 0.0)
    before = jnp.dot(tril_ref[...], onehot.astype(BF16), preferred_element_type=F32) + cnt_ref[...]
    r1 = jnp.sum(jnp.where(hit1, before, 0.0), axis=-1, keepdims=True).astype(jnp.int32)
    r2 = jnp.sum(jnp.where(hit2, before, 0.0), axis=-1, keepdims=True).astype(jnp.int32)
    cnt_ref[...] = cnt_ref[...] + jnp.sum(onehot, axis=0, keepdims=True)
    count_ref[...] = cnt_ref[...]
    id_ref[0] = jnp.where(lane == 0, i1, jnp.where(lane == 1, i2,
                          jnp.where(lane == 2, r1, jnp.where(lane == 3, r2, 0))))


def _mix_route(x, a, u, wp, ps, wo, g2, wr, br, *, batch, seq):
    tm = TM_PROJ
    d = x.shape[-1]
    halo = 8
    nh = tm // halo
    kern = functools.partial(_mix_route_kernel, seq=seq)
    full = lambda arr: pl.BlockSpec(arr.shape, lambda b, i: (0,) * arr.ndim)
    tile = lambda w: pl.BlockSpec((1, tm, w), lambda b, i: (b, i, 0))
    return pl.pallas_call(
        kern,
        grid=(batch, seq // tm),
        in_specs=[
            tile(d), tile(a.shape[-1]), tile(u.shape[-1]),
            pl.BlockSpec((1, halo, u.shape[-1]), lambda b, i: (b, jnp.maximum(i * nh - 1, 0), 0)),
            pl.BlockSpec((1, halo, u.shape[-1]),
                         lambda b, i: (b, jnp.minimum((i + 1) * nh, seq // halo - 1), 0)),
            full(wp), full(ps), full(wo), full(g2), full(wr), full(br),
        ],
        out_specs=[tile(d), tile(d // 2), tile(LANES), tile(LANES),
                   pl.BlockSpec((1, LANES), lambda b, i: (0, 0))],
        out_shape=[
            jax.ShapeDtypeStruct((batch, seq, d), F32),
            jax.ShapeDtypeStruct((batch, seq, d // 2), jnp.uint32),
            jax.ShapeDtypeStruct((batch, seq, LANES), F32),
            jax.ShapeDtypeStruct((batch, seq, LANES), jnp.int32),
            jax.ShapeDtypeStruct((1, LANES), F32),
        ],
        scratch_shapes=[pltpu.VMEM((1, LANES), F32), pltpu.VMEM((tm, tm), BF16)],
        compiler_params=pltpu.CompilerParams(
            dimension_semantics=("arbitrary", "arbitrary"), vmem_limit_bytes=VMEM_LIMIT),
        name="mix_route",
    )(x, a, u, u, u, wp, ps, wo, g2, wr, br)


def _dispatch_kernel(dest_ref, x_ref, xs_in, xs_hbm, sem):
    del xs_in
    tm = x_ref.shape[0]

    for r in range(tm):
        for k in range(TOP_K):
            pltpu.make_async_copy(x_ref.at[r], xs_hbm.at[dest_ref[0, k, r]], sem).start(priority=k)
    for k in range(TOP_K):
        pltpu.make_async_copy(x_ref, xs_hbm.at[pl.ds(0, tm)], sem).wait()


def _dispatch(dest, xn2, xs_zero):
    t, d = xn2.shape
    tm = TM_DISP
    nt = t // tm
    return pl.pallas_call(
        _dispatch_kernel,
        grid=(nt,),
        in_specs=[
            pl.BlockSpec((1, TOP_K, tm), lambda i: (i, 0, 0), memory_space=pltpu.SMEM),
            pl.BlockSpec((tm, d), lambda i: (i, 0)),
            pl.BlockSpec(memory_space=pl.ANY),
        ],
        out_specs=pl.BlockSpec(memory_space=pl.ANY),
        out_shape=jax.ShapeDtypeStruct(xs_zero.shape, xs_zero.dtype),
        input_output_aliases={2: 0},
        scratch_shapes=[pltpu.SemaphoreType.DMA(())],
        compiler_params=pltpu.CompilerParams(
            dimension_semantics=("arbitrary",), vmem_limit_bytes=VMEM_LIMIT),
        name="dispatch",
    )(dest.reshape(nt, tm, TOP_K).transpose(0, 2, 1), xn2, xs_zero)


def _experts_kernel(be_ref, nv_ref, nx_ref, xs_ref, wg_hbm, wu_hbm, wd_hbm, y_ref,
                    wgf, wuf, wdf, wgb, wub, wdb, slot_ref, sem):
    i = pl.program_id(0)
    changed = jnp.logical_or(i == 0, be_ref[i] != be_ref[jnp.maximum(i - 1, 0)])

    def weight_copies(e, slot):
        return (pltpu.make_async_copy(wg_hbm.at[e], wgf.at[slot], sem.at[slot]),
                pltpu.make_async_copy(wu_hbm.at[e], wuf.at[slot], sem.at[slot]),
                pltpu.make_async_copy(wd_hbm.at[e], wdf.at[slot], sem.at[slot]))

    @pl.when(i == 0)
    def _():
        slot_ref[0] = 0
        for cp in weight_copies(be_ref[0], 0):
            cp.start()

    @pl.when(changed)
    def _():
        slot = jnp.where(i == 0, 0, 1 - slot_ref[0])
        slot_ref[0] = slot
        for cp in weight_copies(be_ref[i], slot):
            cp.wait()
        wgb[...] = wgf[slot].astype(BF16)
        wub[...] = wuf[slot].astype(BF16)
        wdb[...] = wdf[slot].astype(BF16)

        @pl.when(nx_ref[i] >= 0)
        def _():
            for cp in weight_copies(nx_ref[i], 1 - slot):
                cp.start()

    blk = xs_ref.shape[0]
    half = blk // 2
    nv = nv_ref[i]

    def mlp(rows):
        words = xs_ref[0:rows, :]
        lo = pltpu.bitcast(words << 16, F32)
        hi = pltpu.bitcast(words & jnp.uint32(0xFFFF0000), F32)
        xb = jnp.concatenate([lo, hi], axis=1).astype(BF16)
        gt = jnp.dot(xb, wgb[...], preferred_element_type=F32)
        up = jnp.dot(xb, wub[...], preferred_element_type=F32)
        hid = (gt * (1.0 / (1.0 + jnp.exp(-gt))) * up).astype(BF16)
        y_ref[0:rows, :] = jnp.dot(hid, wdb[...], preferred_element_type=F32)

    @pl.when(nv > half)
    def _():
        mlp(blk)

    @pl.when(jnp.logical_and(nv > 0, nv <= half))
    def _():
        mlp(half)
        y_ref[half:, :] = jnp.zeros((blk - half, y_ref.shape[1]), F32)

    @pl.when(nv == 0)
    def _():
        y_ref[...] = jnp.zeros(y_ref.shape, F32)


def _experts(block_e, nvalid, next_e, xs, w_gate, w_up, w_down):
    nb = block_e.shape[0]
    d, de = w_gate.shape[-2:]
    grid_spec = pltpu.PrefetchScalarGridSpec(
        num_scalar_prefetch=3,
        grid=(nb,),
        in_specs=[
            pl.BlockSpec((EXP_BLK, xs.shape[-1]), lambda i, be, nv, nx: (i, 0)),
            pl.BlockSpec(memory_space=pl.ANY),
            pl.BlockSpec(memory_space=pl.ANY),
            pl.BlockSpec(memory_space=pl.ANY),
        ],
        out_specs=pl.BlockSpec((EXP_BLK, d), lambda i, be, nv, nx: (i, 0)),
        scratch_shapes=[
            pltpu.VMEM((2, d, de), F32), pltpu.VMEM((2, d, de), F32), pltpu.VMEM((2, de, d), F32),
            pltpu.VMEM((d, de), BF16), pltpu.VMEM((d, de), BF16), pltpu.VMEM((de, d), BF16),
            pltpu.SMEM((1,), jnp.int32), pltpu.SemaphoreType.DMA((2,)),
        ],
    )
    return pl.pallas_call(
        _experts_kernel,
        grid_spec=grid_spec,
        out_shape=jax.ShapeDtypeStruct((nb * EXP_BLK, d), F32),
        compiler_params=pltpu.CompilerParams(
            dimension_semantics=("arbitrary",), vmem_limit_bytes=VMEM_LIMIT),
        name="experts",
    )(block_e, nvalid, next_e, xs, w_gate, w_up, w_down)


def _combine_kernel(dest_ref, dnext_ref, h_ref, gate_ref, y_hbm, g_ref, o_ref, ybuf, sem):
    tm = h_ref.shape[0]
    i = pl.program_id(0)
    slot = i % 2

    def start_rows(idx_ref, s):
        for r in range(tm):
            for k in range(TOP_K):
                pltpu.make_async_copy(y_hbm.at[idx_ref[0, k, r]], ybuf.at[s, k, r], sem.at[s]).start(priority=k)

    @pl.when(i == 0)
    def _():
        start_rows(dest_ref, 0)

    @pl.when(i + 1 < pl.num_programs(0))
    def _():
        start_rows(dnext_ref, 1 - slot)

    for k in range(TOP_K):
        pltpu.make_async_copy(y_hbm.at[pl.ds(0, tm)], ybuf.at[slot, k], sem.at[slot]).wait()
    gates = gate_ref[...]
    hcur = h_ref[...] + (gates[:, 0:1] * ybuf[slot, 0] + gates[:, 1:2] * ybuf[slot, 1])
    ms = jnp.mean(hcur * hcur, axis=-1, keepdims=True)
    o_ref[...] = hcur * lax.rsqrt(ms + EPS) * g_ref[...]


def _combine(dest, h2, gates, ys, gfin):
    t, d = h2.shape
    tm = TM_COMB
    nt = t // tm
    dest3 = dest.reshape(nt, tm, TOP_K).transpose(0, 2, 1)
    return pl.pallas_call(
        _combine_kernel,
        grid=(nt,),
        in_specs=[
            pl.BlockSpec((1, TOP_K, tm), lambda i: (i, 0, 0), memory_space=pltpu.SMEM),
            pl.BlockSpec((1, TOP_K, tm), lambda i: (jnp.minimum(i + 1, nt - 1), 0, 0),
                         memory_space=pltpu.SMEM),
            pl.BlockSpec((tm, d), lambda i: (i, 0)),
            pl.BlockSpec((tm, LANES), lambda i: (i, 0)),
            pl.BlockSpec(memory_space=pl.ANY),
            pl.BlockSpec(gfin.shape, lambda i: (0, 0)),
        ],
        out_specs=pl.BlockSpec((tm, d), lambda i: (i, 0)),
        out_shape=jax.ShapeDtypeStruct((t, d), F32),
        scratch_shapes=[pltpu.VMEM((2, TOP_K, tm, d), F32), pltpu.SemaphoreType.DMA((2,))],
        compiler_params=pltpu.CompilerParams(
            dimension_semantics=("arbitrary",), vmem_limit_bytes=VMEM_LIMIT),
        name="combine",
    )(dest3, dest3, h2, gates, ys, gfin)


def _route_plan(meta, counts, t):
    ids = meta[:, :TOP_K]
    ranks = meta[:, TOP_K:2 * TOP_K]
    cnt = counts[0, :N_EXPERTS].astype(jnp.int32)
    padded = (cnt + EXP_BLK - 1) // EXP_BLK * EXP_BLK
    pad_ends = jnp.cumsum(padded)
    pad_starts = pad_ends - padded
    experts = jnp.arange(N_EXPERTS, dtype=jnp.int32)
    dest = jnp.sum(jnp.where(ids[..., None] == experts, pad_starts, 0), axis=-1) + ranks
    nb = (t * TOP_K + EXP_BLK - 1) // EXP_BLK + N_EXPERTS
    bstart = jnp.arange(nb, dtype=jnp.int32) * EXP_BLK
    last_e = jnp.max(jnp.where(cnt > 0, experts, 0))
    block_e = jnp.minimum(jnp.sum((pad_ends[None, :] <= bstart[:, None]).astype(jnp.int32), axis=1), last_e)
    row_end = jnp.sum(jnp.where(block_e[:, None] == experts, pad_starts + cnt, 0), axis=-1)
    nvalid = jnp.clip(row_end - bstart, 0, EXP_BLK)
    later = (experts[None, :] > experts[:, None]) & (cnt[None, :] > 0)
    next_of = jnp.min(jnp.where(later, experts[None, :], N_EXPERTS), axis=1)
    next_of = jnp.where(next_of == N_EXPERTS, -1, next_of)
    next_e = jnp.sum(jnp.where(block_e[:, None] == experts, next_of, 0), axis=-1)
    return (dest.astype(jnp.int32), block_e.astype(jnp.int32), nvalid.astype(jnp.int32),
            next_e.astype(jnp.int32))


def kernel(x, norm1_g, w_in, lambda_q1, lambda_k1, lambda_q2, lambda_k2, subln_g, w_pool, pool_scale,
           w_out, norm2_g, w_group_router, b_group_router, w_expert_router, b_expert_router,
           w_gate, w_up, w_down, final_g):
    batch, seq, d = x.shape
    t = batch * seq
    qk_w = 2 * N_HEADS * HEAD_DIM
    av_w = N_HEADS * V_DIM
    l = 0
    assert seq % (2 * TK) == 0 and seq % TM_PROJ == 0 and TK == TM_PROJ
    assert t % TM_COMB == 0 and t % TM_DISP == 0
    assert seq % (ATTN_QSUB * TQ) == 0 and ATTN_UNROLL % 2 == 0

    wi = w_in[l]
    wqt = wi[:, :qk_w].T.astype(BF16)
    wk = wi[:, qk_w:2 * qk_w].astype(BF16)
    wv = wi[:, 2 * qk_w:2 * qk_w + av_w].reshape(d, N_HEADS, V_DIM)
    wvt = jnp.pad(wv, ((0, 0), (0, 0), (0, VT_ROWS - V_DIM))).reshape(d, N_HEADS * VT_ROWS).T.astype(BF16)
    wu = wi[:, 2 * qk_w + av_w:].astype(BF16)

    qt, k, vt, u = _in_proj(x.reshape(t, d), norm1_g[l].reshape(1, d), wqt, wk, wvt, wu,
                            batch=batch, seq=seq)

    lam_in = jnp.zeros((8, LANES), F32)
    lam_in = lam_in.at[0, :HEAD_DIM].set(lambda_q1[l]).at[1, :HEAD_DIM].set(lambda_k1[l])
    lam_in = lam_in.at[2, :HEAD_DIM].set(lambda_q2[l]).at[3, :HEAD_DIM].set(lambda_k2[l])
    gsub = jnp.broadcast_to(subln_g[l].reshape(V_DIM, 1), (V_DIM, TQ)).astype(F32)
    a = _attention(lam_in, qt, k, vt, gsub, batch=batch, seq=seq)

    wr = jnp.zeros((d, 2 * LANES), F32)
    wr = wr.at[:, :N_GROUPS].set(w_group_router[l])
    wr = wr.at[:, LANES:LANES + N_EXPERTS].set(w_expert_router[l].reshape(d, N_EXPERTS)).astype(BF16)
    br = jnp.full((1, 2 * LANES), NEG, F32)
    br = br.at[0, :N_GROUPS].set(b_group_router[l])
    br = br.at[0, LANES:LANES + N_EXPERTS].set(b_expert_router[l].reshape(N_EXPERTS))
    h, xn2, gates, meta, counts = _mix_route(
        x, a, u.reshape(batch, seq, -1), w_pool[l].astype(BF16), pool_scale[l].reshape(1, -1),
        w_out[l].astype(BF16), norm2_g[l].reshape(1, d), wr, br, batch=batch, seq=seq)

    dest, block_e, nvalid, next_e = _route_plan(meta.reshape(t, LANES), counts, t)
    xs = _dispatch(dest, xn2.reshape(t, d // 2),
                   jnp.zeros((block_e.shape[0] * EXP_BLK, d // 2), jnp.uint32))
    ys = _experts(block_e, nvalid, next_e, xs, w_gate[l], w_up[l], w_down[l])
    out = _combine(dest, h.reshape(t, d), gates.reshape(t, LANES), ys, final_g.reshape(1, d))
    return out.reshape(batch, seq, d)
```

```python
import functools
import math

import jax
import jax.numpy as jnp
import numpy as np
from jax import lax
from jax.experimental import pallas as pl
from jax.experimental.pallas import tpu as pltpu

F32 = jnp.float32
BF16 = jnp.bfloat16

N_HEADS = 4
HEAD_DIM = 64
V_DIM = 128
POOL_WINDOWS = (2, 4, 8, 16)
N_GROUPS = 4
E_PER_GROUP = 8
N_EXPERTS = N_GROUPS * E_PER_GROUP
TOP_K = 2
EPS = 1e-6
LAM_INIT = 0.8 - 0.6 * math.exp(-0.3 * 0)
NEG = -1e30
LOG2E = math.log2(math.e)


def _bf16_terms(value, n):
    terms, rest = [], value
    for _ in range(n):
        term = float(np.asarray(rest, dtype=jnp.bfloat16))
        terms.append(term)
        rest -= term
    return tuple(terms)


LOG2E_TERMS = _bf16_terms(LOG2E, 3)
N_TERMS = len(LOG2E_TERMS)
SLOPES = tuple(2.0 ** (-8.0 * (h + 1) / N_HEADS) for h in range(N_HEADS))

LANES = 128
QK_PAD = 128
VT_ROWS = 144
POS_SPLIT = 128

TM_PROJ = 512
TQ = 512
TK = 512
ATTN_UNROLL = 6
ATTN_QSUB = 1
EXP_BLK = 512
TM_DISP = 1024
TM_COMB = 512
VMEM_LIMIT = 56 * 1024 * 1024


def _nt_dot(a, b):
    return lax.dot_general(a, b, (((1,), (1,)), ((), ())), preferred_element_type=F32)


def _in_proj_kernel(x_ref, g_ref, wqt_ref, wk_ref, wvt_ref, wu_ref, kc_ref,
                    qt_ref, k_ref, vt_ref, u_ref, *, seq):
    tm = x_ref.shape[0]
    i = pl.program_id(0)
    j0 = (i % (seq // tm)) * tm
    x = x_ref[...]
    ms = jnp.mean(x * x, axis=-1, keepdims=True)
    hn = (x * lax.rsqrt(ms + EPS) * g_ref[...]).astype(BF16)

    qt = _nt_dot(wqt_ref[...], hn) * (LOG2E / math.sqrt(HEAD_DIM))
    zero_rows = jnp.zeros((HEAD_DIM, tm), BF16)
    for g in range(2 * N_HEADS):
        data = qt[g * HEAD_DIM:(g + 1) * HEAD_DIM].astype(BF16)
        lo_rows, hi_rows = (data, zero_rows) if g % 2 == 0 else (zero_rows, data)
        qt_ref[0, g, 0:HEAD_DIM, :] = lo_rows
        qt_ref[0, g, HEAD_DIM:QK_PAD, :] = hi_rows

    kk = jnp.dot(hn, wk_ref[...], preferred_element_type=F32)
    pos = j0 + lax.broadcasted_iota(jnp.int32, (tm, QK_PAD), 0)
    lane = lax.broadcasted_iota(jnp.int32, (tm, QK_PAD), 1)
    hi = (pos & ~(POS_SPLIT - 1)).astype(F32)
    lo = (pos & (POS_SPLIT - 1)).astype(F32)
    for c in range(2):
        f0 = _feature_base(c)
        f_hi, f_lo = f0 + 2 * N_TERMS, f0 + 3 * N_TERMS
        feat = jnp.where((lane >= f_hi) & (lane < f_lo), hi,
                         jnp.where((lane >= f_lo) & (lane < f_lo + N_TERMS), lo, 0.0))
        own = (lane < HEAD_DIM) if c == 0 else (lane >= HEAD_DIM)
        for h in range(N_HEADS):
            g = 2 * h + c
            dims = jnp.where(own, kk[:, h * QK_PAD:(h + 1) * QK_PAD], 0.0)
            k_ref[0, g] = (dims + (feat + kc_ref[g:g + 1, :])).astype(BF16)

    vt = _nt_dot(wvt_ref[...], hn)
    row = lax.broadcasted_iota(jnp.int32, (VT_ROWS, tm), 0)
    for h in range(N_HEADS):
        blk = vt[h * VT_ROWS:(h + 1) * VT_ROWS]
        vt_ref[0, h, 0] = jnp.where(row == V_DIM, 1.0, blk).astype(BF16)

    u_ref[...] = jnp.dot(hn, wu_ref[...], preferred_element_type=F32)


def _feature_base(c):
    return HEAD_DIM if c == 0 else 0


def _key_feature_consts():
    kc = np.zeros((2 * N_HEADS, QK_PAD), np.float32)
    for g in range(2 * N_HEADS):
        for rep in range(2):
            for n, term in enumerate(LOG2E_TERMS):
                kc[g, _feature_base(g % 2) + rep * N_TERMS + n] = term * SLOPES[g // 2]
    return jnp.asarray(kc)


def _in_proj(x2, g1, wqt, wk, wvt, wu, *, batch, seq):
    t, d = x2.shape
    tm = TM_PROJ
    nblk = seq // tm
    kern = functools.partial(_in_proj_kernel, seq=seq)
    full = lambda shape: pl.BlockSpec(shape, lambda i: (0,) * len(shape))
    kc = _key_feature_consts()
    return pl.pallas_call(
        kern,
        grid=(t // tm,),
        in_specs=[
            pl.BlockSpec((tm, d), lambda i: (i, 0)),
            full(g1.shape), full(wqt.shape), full(wk.shape), full(wvt.shape), full(wu.shape),
            full(kc.shape),
        ],
        out_specs=[
            pl.BlockSpec((1, 2 * N_HEADS, QK_PAD, tm), lambda i: (i // nblk, 0, 0, i % nblk)),
            pl.BlockSpec((1, 2 * N_HEADS, tm, QK_PAD), lambda i: (i // nblk, 0, i % nblk, 0)),
            pl.BlockSpec((1, N_HEADS, 1, VT_ROWS, tm), lambda i: (i // nblk, 0, i % nblk, 0, 0)),
            pl.BlockSpec((tm, wu.shape[1]), lambda i: (i, 0)),
        ],
        out_shape=[
            jax.ShapeDtypeStruct((batch, 2 * N_HEADS, QK_PAD, seq), BF16),
            jax.ShapeDtypeStruct((batch, 2 * N_HEADS, seq, QK_PAD), BF16),
            jax.ShapeDtypeStruct((batch, N_HEADS, nblk, VT_ROWS, tm), BF16),
            jax.ShapeDtypeStruct((t, wu.shape[1]), F32),
        ],
        compiler_params=pltpu.CompilerParams(
            dimension_semantics=("arbitrary",), vmem_limit_bytes=VMEM_LIMIT),
        name="in_proj",
    )(x2, g1, wqt, wk, wvt, wu, kc)


def _attention_kernel(lam_ref, qt_ref, k_ref, vt_ref, g_ref, o_ref, zero_hbm,
                      qv_ref, acc_ref, m_ref, sa_ref, sb_ref, mba_ref, mbb_ref, zbuf, zsem):
    step = ((pl.program_id(0) * pl.num_programs(1) + pl.program_id(1)) * pl.num_programs(2)
            + pl.program_id(2))
    zrows = zbuf.shape[0]

    @pl.when(step == 0)
    def _():
        zbuf[...] = jnp.zeros(zbuf.shape, zbuf.dtype)

    zero_copy = pltpu.make_async_copy(zbuf, zero_hbm.at[pl.ds(step * zrows, zrows)], zsem)
    zero_copy.start()

    tq = o_ref.shape[1] // ATTN_QSUB
    for sub in range(ATTN_QSUB):
        _attention_block(
            pl.program_id(2) * ATTN_QSUB + sub, qt_ref.at[:, :, :, sub * tq:(sub + 1) * tq], k_ref, vt_ref,
            qv_ref.at[sub], acc_ref.at[sub], m_ref.at[sub], sa_ref, sb_ref, mba_ref, mbb_ref)
    for sub in range(ATTN_QSUB):
        _attention_finish(lam_ref, g_ref, o_ref.at[:, sub * tq:(sub + 1) * tq, :], acc_ref.at[sub])
    zero_copy.wait()


def _attention_block(qi, qt_ref, k_ref, vt_ref, qv_ref, acc_ref, m_ref, sa_ref, sb_ref, mba_ref, mbb_ref):
    h = pl.program_id(1)
    tq = qt_ref.shape[-1]
    nkb, _, tk = vt_ref.shape[2:]
    jdiag = (qi * tq) // tk

    slope = jnp.where(h == 0, SLOPES[0], jnp.where(h == 1, SLOPES[1],
                      jnp.where(h == 2, SLOPES[2], SLOPES[3]))).astype(F32)

    r = lax.broadcasted_iota(jnp.int32, (QK_PAD, tq), 0)
    ipos = qi * tq + lax.broadcasted_iota(jnp.int32, (QK_PAD, tq), 1)
    ihi = (ipos & ~(POS_SPLIT - 1)).astype(F32)
    ilo = (ipos & (POS_SPLIT - 1)).astype(F32)
    for c in range(2):
        f0 = _feature_base(c)
        fi = jnp.where((r >= f0) & (r < f0 + N_TERMS), ihi,
                       jnp.where((r >= f0 + N_TERMS) & (r < f0 + 2 * N_TERMS), ilo, 0.0))
        fj = jnp.zeros((QK_PAD, tq), F32)
        for rep in range(2):
            for n, term in enumerate(LOG2E_TERMS):
                fj = jnp.where(r == f0 + (2 + rep) * N_TERMS + n, term * slope, fj)
        q = qt_ref[0, c].astype(F32)
        qv_ref[0, c] = (q + fi - fj).astype(BF16)
        qv_ref[1, c] = (q - fi + fj).astype(BF16)

    acc_ref[...] = jnp.zeros(acc_ref.shape, F32)
    m_ref[...] = jnp.full(m_ref.shape, NEG, F32)

    def score_stage(kb, variant, s_ref, mb_ref, bias=None):
        for c in range(2):
            kblk = k_ref[0, c, pl.ds(pl.multiple_of(kb * tk, tk), tk), :]
            qmat = qt_ref[0, c] if variant is None else qv_ref[variant, c]
            st = jnp.dot(kblk, qmat, preferred_element_type=F32)
            if bias is not None:
                st = st + bias
            s_ref[c] = st
            mb_ref[c] = jnp.max(st, axis=0, keepdims=True)

    def softmax_stage(kb, s_ref, mb_ref):
        for c in range(2):
            m_old = m_ref[c]
            m_new = jnp.maximum(m_old, mb_ref[c])
            alpha = jnp.exp2(m_old - m_new)
            p = jnp.exp2(s_ref[c] - m_new).astype(BF16)
            pv = jnp.dot(vt_ref[0, 0, kb], p, preferred_element_type=F32)
            acc_ref[c] = acc_ref[c] * alpha + pv
            m_ref[c] = m_new

    def visit_block(n):
        o = n - 1
        return jnp.where(n == 0, jdiag, o + (o >= jdiag).astype(jnp.int32))

    def visit_variant(kb):
        return jnp.where(kb < jdiag, 1, 0)

    jpos = jdiag * tk + lax.broadcasted_iota(jnp.int32, (tk, tq), 0)
    iposd = qi * tq + lax.broadcasted_iota(jnp.int32, (tk, tq), 1)
    bias = (-LOG2E * slope) * jnp.abs(iposd - jpos).astype(F32)
    score_stage(jdiag, None, sa_ref, mba_ref, bias)

    bufs = ((sa_ref, mba_ref), (sb_ref, mbb_ref))

    def phase(n, parity, with_scores):
        if with_scores:
            kb_next = visit_block(n + 1)
            score_stage(kb_next, visit_variant(kb_next), *bufs[1 - parity])
        softmax_stage(visit_block(n), *bufs[parity])

    def loop_body(t, carry):
        for j in range(ATTN_UNROLL):
            phase(t * ATTN_UNROLL + j, j % 2, True)
        return carry

    trips = (nkb - 1) // ATTN_UNROLL
    lax.fori_loop(0, trips, loop_body, 0)
    for n in range(trips * ATTN_UNROLL, nkb):
        phase(n, n % 2, n + 1 < nkb)


def _attention_finish(lam_ref, g_ref, o_ref, acc_ref):
    lv = lam_ref[...]
    lam = (jnp.exp(jnp.sum(lv[0:1] * lv[1:2], axis=-1, keepdims=True))
           - jnp.exp(jnp.sum(lv[2:3] * lv[3:4], axis=-1, keepdims=True)) + LAM_INIT)
    a0 = acc_ref[0]
    a1 = acc_ref[1]
    o = (a0[:V_DIM] * (1.0 / a0[V_DIM:V_DIM + 1])
         - lam * (a1[:V_DIM] * (1.0 / a1[V_DIM:V_DIM + 1])))
    ms = jnp.mean(o * o, axis=0, keepdims=True)
    y = o * lax.rsqrt(ms + EPS) * g_ref[...] * (1.0 - LAM_INIT)
    o_ref[0] = y.T.astype(BF16)


def _attention(lam_in, qt, k, vt, gsub, zero_shape, *, batch, seq):
    nkb = seq // TK
    steps = batch * N_HEADS * (seq // (ATTN_QSUB * TQ))
    assert zero_shape[0] % steps == 0 and (zero_shape[0] // steps) % 8 == 0
    zrows = zero_shape[0] // steps
    return pl.pallas_call(
        _attention_kernel,
        grid=(batch, N_HEADS, seq // (ATTN_QSUB * TQ)),
        in_specs=[
            pl.BlockSpec(lam_in.shape, lambda b, h, i: (0, 0)),
            pl.BlockSpec((1, 2, QK_PAD, ATTN_QSUB * TQ), lambda b, h, i: (b, h, 0, i)),
            pl.BlockSpec((1, 2, seq, QK_PAD), lambda b, h, i: (b, h, 0, 0)),
            pl.BlockSpec((1, 1, nkb, VT_ROWS, TK), lambda b, h, i: (b, h, 0, 0, 0)),
            pl.BlockSpec(gsub.shape, lambda b, h, i: (0, 0)),
        ],
        out_specs=[pl.BlockSpec((1, ATTN_QSUB * TQ, V_DIM), lambda b, h, i: (b, i, h)),
                   pl.BlockSpec(memory_space=pl.ANY)],
        out_shape=[jax.ShapeDtypeStruct((batch, seq, N_HEADS * V_DIM), BF16),
                   jax.ShapeDtypeStruct(zero_shape, jnp.uint32)],
        scratch_shapes=[
            pltpu.VMEM((ATTN_QSUB, 2, 2, QK_PAD, TQ), BF16),
            pltpu.VMEM((ATTN_QSUB, 2, VT_ROWS, TQ), F32),
            pltpu.VMEM((ATTN_QSUB, 2, 1, TQ), F32),
            pltpu.VMEM((2, TK, TQ), F32),
            pltpu.VMEM((2, TK, TQ), F32),
            pltpu.VMEM((2, 1, TQ), F32),
            pltpu.VMEM((2, 1, TQ), F32),
            pltpu.VMEM((zrows, zero_shape[1]), jnp.uint32),
            pltpu.SemaphoreType.DMA(()),
        ],
        compiler_params=pltpu.CompilerParams(
            dimension_semantics=("arbitrary", "arbitrary", "arbitrary"),
            vmem_limit_bytes=VMEM_LIMIT),
        name="diff_attention",
    )(lam_in, qt, k, vt, gsub)


def _mix_route_kernel(x_ref, a_ref, u_ref, up_ref, un_ref, wp_ref, ps_ref, wo_ref, g2_ref,
                      wr_ref, br_ref, h_ref, xn_ref, gate_ref, id_ref, count_ref, cnt_ref, tril_ref,
                      *, seq):
    tm = x_ref.shape[1]
    j0 = pl.program_id(1) * tm
    halo = up_ref.shape[1]

    @pl.when((pl.program_id(0) == 0) & (pl.program_id(1) == 0))
    def _():
        cnt_ref[...] = jnp.zeros(cnt_ref.shape, F32)
        rr = lax.broadcasted_iota(jnp.int32, (tm, tm), 0)
        cc = lax.broadcasted_iota(jnp.int32, (tm, tm), 1)
        tril_ref[...] = jnp.where(cc < rr, 1.0, 0.0).astype(BF16)

    n = tm + 2 * halo
    ext = jnp.concatenate([jnp.where(j0 > 0, up_ref[0], 0.0), u_ref[0],
                           jnp.where(j0 + tm < seq, un_ref[0], 0.0)], axis=0)
    edge = lax.broadcasted_iota(jnp.int32, (halo, LANES), 0)
    tpos_top = j0 + edge
    tpos_bot = j0 + (tm - halo) + edge
    ys = []
    for g, w in enumerate(POOL_WINDOWS):
        fwd = ext[:, g * LANES:(g + 1) * LANES]
        length = 1
        while 2 * length < w:
            fwd = fwd + pltpu.roll(fwd, n - length, 0)
            length *= 2
        win = (fwd + pltpu.roll(fwd, w // 2, 0))[halo:halo + tm]

        def clipped(tpos, w=w):
            return (jnp.minimum(tpos + w // 2, seq) - jnp.maximum(tpos - w // 2, 0)).astype(F32)

        mean = jnp.concatenate([win[:halo] / clipped(tpos_top), win[halo:tm - halo] * (1.0 / w),
                                win[tm - halo:] / clipped(tpos_bot)], axis=0)
        d = (mean - u_ref[0][:, g * LANES:(g + 1) * LANES]).astype(BF16)
        yg = jnp.dot(d, wp_ref[g], preferred_element_type=F32)
        ys.append(yg * ps_ref[:, g * LANES:(g + 1) * LANES])
    p = jnp.concatenate(ys, axis=-1).astype(BF16)

    mix = jnp.concatenate([a_ref[0], p], axis=-1)
    hcur = x_ref[0] + jnp.dot(mix, wo_ref[...], preferred_element_type=F32)
    h_ref[0] = hcur
    ms = jnp.mean(hcur * hcur, axis=-1, keepdims=True)
    xn = (hcur * lax.rsqrt(ms + EPS) * g2_ref[...]).astype(BF16)
    half = xn.shape[1] // 2
    lo = pltpu.bitcast(xn[:, :half].astype(F32), jnp.uint32) >> 16
    hi = pltpu.bitcast(xn[:, half:].astype(F32), jnp.uint32) & jnp.uint32(0xFFFF0000)
    xn_ref[0] = hi | lo

    logits = jnp.dot(xn, wr_ref[...], preferred_element_type=F32) + br_ref[...]
    gl = logits[:, :LANES]
    el = logits[:, LANES:]
    lane = lax.broadcasted_iota(jnp.int32, (tm, LANES), 1)
    gmax = jnp.max(gl, axis=-1, keepdims=True)
    gsel = jnp.min(jnp.where(gl == gmax, lane, LANES), axis=-1, keepdims=True)
    pg = 1.0 / jnp.sum(jnp.exp(gl - gmax), axis=-1, keepdims=True)
    elm = jnp.where((lane >> 3) == gsel, el, NEG)
    m1 = jnp.max(elm, axis=-1, keepdims=True)
    i1 = jnp.min(jnp.where(elm == m1, lane, LANES), axis=-1, keepdims=True)
    elm2 = jnp.where(lane == i1, NEG, elm)
    m2 = jnp.max(elm2, axis=-1, keepdims=True)
    i2 = jnp.min(jnp.where(elm2 == m2, lane, LANES), axis=-1, keepdims=True)
    e2 = jnp.exp(m2 - m1)
    t1 = 1.0 / (1.0 + e2)
    gate_ref[0] = jnp.where(lane == 0, pg * t1, jnp.where(lane == 1, pg * (e2 * t1), 0.0))

    hit1 = lane == i1
    hit2 = lane == i2
    onehot = jnp.where(hit1 | hit2, 1.0, 0.0)
    before = jnp.dot(tril_ref[...], onehot.astype(BF16), preferred_element_type=F32) + cnt_ref[...]
    r1 = jnp.sum(jnp.where(hit1, before, 0.0), axis=-1, keepdims=True).astype(jnp.int32)
    r2 = jnp.sum(jnp.where(hit2, before, 0.0), axis=-1, keepdims=True).astype(jnp.int32)
    cnt_ref[...] = cnt_ref[...] + jnp.sum(onehot, axis=0, keepdims=True)
    count_ref[...] = cnt_ref[...]
    id_ref[0] = jnp.where(lane == 0, i1, jnp.where(lane == 1, i2,
                          jnp.where(lane == 2, r1, jnp.where(lane == 3, r2, 0))))


def _mix_route(x, a, u, wp, ps, wo, g2, wr, br, *, batch, seq):
    tm = TM_PROJ
    d = x.shape[-1]
    halo = 8
    nh = tm // halo
    kern = functools.partial(_mix_route_kernel, seq=seq)
    full = lambda arr: pl.BlockSpec(arr.shape, lambda b, i: (0,) * arr.ndim)
    tile = lambda w: pl.BlockSpec((1, tm, w), lambda b, i: (b, i, 0))
    return pl.pallas_call(
        kern,
        grid=(batch, seq // tm),
        in_specs=[
            tile(d), tile(a.shape[-1]), tile(u.shape[-1]),
            pl.BlockSpec((1, halo, u.shape[-1]), lambda b, i: (b, jnp.maximum(i * nh - 1, 0), 0)),
            pl.BlockSpec((1, halo, u.shape[-1]),
                         lambda b, i: (b, jnp.minimum((i + 1) * nh, seq // halo - 1), 0)),
            full(wp), full(ps), full(wo), full(g2), full(wr), full(br),
        ],
        out_specs=[tile(d), tile(d // 2), tile(LANES), tile(LANES),
                   pl.BlockSpec((1, LANES), lambda b, i: (0, 0))],
        out_shape=[
            jax.ShapeDtypeStruct((batch, seq, d), F32),
            jax.ShapeDtypeStruct((batch, seq, d // 2), jnp.uint32),
            jax.ShapeDtypeStruct((batch, seq, LANES), F32),
            jax.ShapeDtypeStruct((batch, seq, LANES), jnp.int32),
            jax.ShapeDtypeStruct((1, LANES), F32),
        ],
        scratch_shapes=[pltpu.VMEM((1, LANES), F32), pltpu.VMEM((tm, tm), BF16)],
        compiler_params=pltpu.CompilerParams(
            dimension_semantics=("arbitrary", "arbitrary"), vmem_limit_bytes=VMEM_LIMIT),
        name="mix_route",
    )(x, a, u, u, u, wp, ps, wo, g2, wr, br)


def _dispatch_kernel(dest_ref, x_ref, xs_in, xs_hbm, sem):
    del xs_in
    tm = x_ref.shape[0]

    for r in range(tm):
        for k in range(TOP_K):
            pltpu.make_async_copy(x_ref.at[r], xs_hbm.at[dest_ref[0, k, r]], sem).start(priority=k)
    for k in range(TOP_K):
        pltpu.make_async_copy(x_ref, xs_hbm.at[pl.ds(0, tm)], sem).wait()


def _dispatch(dest, xn2, xs_zero):
    t, d = xn2.shape
    tm = TM_DISP
    nt = t // tm
    return pl.pallas_call(
        _dispatch_kernel,
        grid=(nt,),
        in_specs=[
            pl.BlockSpec((1, TOP_K, tm), lambda i: (i, 0, 0), memory_space=pltpu.SMEM),
            pl.BlockSpec((tm, d), lambda i: (i, 0)),
            pl.BlockSpec(memory_space=pl.ANY),
        ],
        out_specs=pl.BlockSpec(memory_space=pl.ANY),
        out_shape=jax.ShapeDtypeStruct(xs_zero.shape, xs_zero.dtype),
        input_output_aliases={2: 0},
        scratch_shapes=[pltpu.SemaphoreType.DMA(())],
        compiler_params=pltpu.CompilerParams(
            dimension_semantics=("arbitrary",), vmem_limit_bytes=VMEM_LIMIT),
        name="dispatch",
    )(dest.reshape(nt, tm, TOP_K).transpose(0, 2, 1), xn2, xs_zero)


def _experts_kernel(be_ref, nv_ref, nx_ref, xs_ref, wg_hbm, wu_hbm, wd_hbm, y_ref,
                    wgf, wuf, wdf, wgb, wub, wdb, slot_ref, sem):
    i = pl.program_id(0)
    changed = jnp.logical_or(i == 0, be_ref[i] != be_ref[jnp.maximum(i - 1, 0)])

    def weight_copies(e, slot):
        return (pltpu.make_async_copy(wg_hbm.at[e], wgf.at[slot], sem.at[slot]),
                pltpu.make_async_copy(wu_hbm.at[e], wuf.at[slot], sem.at[slot]),
                pltpu.make_async_copy(wd_hbm.at[e], wdf.at[slot], sem.at[slot]))

    @pl.when(i == 0)
    def _():
        slot_ref[0] = 0
        for cp in weight_copies(be_ref[0], 0):
            cp.start()

    @pl.when(changed)
    def _():
        slot = jnp.where(i == 0, 0, 1 - slot_ref[0])
        slot_ref[0] = slot
        for cp in weight_copies(be_ref[i], slot):
            cp.wait()
        wgb[...] = wgf[slot].astype(BF16)
        wub[...] = wuf[slot].astype(BF16)
        wdb[...] = wdf[slot].astype(BF16)

        @pl.when(nx_ref[i] >= 0)
        def _():
            for cp in weight_copies(nx_ref[i], 1 - slot):
                cp.start()

    blk = xs_ref.shape[0]
    half = blk // 2
    nv = nv_ref[i]

    def mlp(rows):
        words = xs_ref[0:rows, :]
        lo = pltpu.bitcast(words << 16, F32)
        hi = pltpu.bitcast(words & jnp.uint32(0xFFFF0000), F32)
        xb = jnp.concatenate([lo, hi], axis=1).astype(BF16)
        gt = jnp.dot(xb, wgb[...], preferred_element_type=F32)
        up = jnp.dot(xb, wub[...], preferred_element_type=F32)
        hid = (gt * (1.0 / (1.0 + jnp.exp(-gt))) * up).astype(BF16)
        y_ref[0:rows, :] = jnp.dot(hid, wdb[...], preferred_element_type=F32)

    @pl.when(nv > half)
    def _():
        mlp(blk)

    @pl.when(jnp.logical_and(nv > 0, nv <= half))
    def _():
        mlp(half)
        y_ref[half:, :] = jnp.zeros((blk - half, y_ref.shape[1]), F32)

    @pl.when(nv == 0)
    def _():
        y_ref[...] = jnp.zeros(y_ref.shape, F32)


def _experts(block_e, nvalid, next_e, xs, w_gate, w_up, w_down):
    nb = block_e.shape[0]
    d, de = w_gate.shape[-2:]
    grid_spec = pltpu.PrefetchScalarGridSpec(
        num_scalar_prefetch=3,
        grid=(nb,),
        in_specs=[
            pl.BlockSpec((EXP_BLK, xs.shape[-1]), lambda i, be, nv, nx: (i, 0)),
            pl.BlockSpec(memory_space=pl.ANY),
            pl.BlockSpec(memory_space=pl.ANY),
            pl.BlockSpec(memory_space=pl.ANY),
        ],
        out_specs=pl.BlockSpec((EXP_BLK, d), lambda i, be, nv, nx: (i, 0)),
        scratch_shapes=[
            pltpu.VMEM((2, d, de), F32), pltpu.VMEM((2, d, de), F32), pltpu.VMEM((2, de, d), F32),
            pltpu.VMEM((d, de), BF16), pltpu.VMEM((d, de), BF16), pltpu.VMEM((de, d), BF16),
            pltpu.SMEM((1,), jnp.int32), pltpu.SemaphoreType.DMA((2,)),
        ],
    )
    return pl.pallas_call(
        _experts_kernel,
        grid_spec=grid_spec,
        out_shape=jax.ShapeDtypeStruct((nb * EXP_BLK, d), F32),
        compiler_params=pltpu.CompilerParams(
            dimension_semantics=("arbitrary",), vmem_limit_bytes=VMEM_LIMIT),
        name="experts",
    )(block_e, nvalid, next_e, xs, w_gate, w_up, w_down)


def _combine_kernel(dest_ref, dnext_ref, h_ref, gate_ref, y_hbm, g_ref, o_ref, ybuf, sem):
    tm = h_ref.shape[0]
    i = pl.program_id(0)
    slot = i % 2

    def start_rows(idx_ref, s):
        for r in range(tm):
            for k in range(TOP_K):
                pltpu.make_async_copy(y_hbm.at[idx_ref[0, k, r]], ybuf.at[s, k, r], sem.at[s]).start(priority=k)

    @pl.when(i == 0)
    def _():
        start_rows(dest_ref, 0)

    @pl.when(i + 1 < pl.num_programs(0))
    def _():
        start_rows(dnext_ref, 1 - slot)

    for k in range(TOP_K):
        pltpu.make_async_copy(y_hbm.at[pl.ds(0, tm)], ybuf.at[slot, k], sem.at[slot]).wait()
    gates = gate_ref[...]
    hcur = h_ref[...] + (gates[:, 0:1] * ybuf[slot, 0] + gates[:, 1:2] * ybuf[slot, 1])
    ms = jnp.mean(hcur * hcur, axis=-1, keepdims=True)
    o_ref[...] = hcur * lax.rsqrt(ms + EPS) * g_ref[...]


def _combine(dest, h2, gates, ys, gfin):
    t, d = h2.shape
    tm = TM_COMB
    nt = t // tm
    dest3 = dest.reshape(nt, tm, TOP_K).transpose(0, 2, 1)
    return pl.pallas_call(
        _combine_kernel,
        grid=(nt,),
        in_specs=[
            pl.BlockSpec((1, TOP_K, tm), lambda i: (i, 0, 0), memory_space=pltpu.SMEM),
            pl.BlockSpec((1, TOP_K, tm), lambda i: (jnp.minimum(i + 1, nt - 1), 0, 0),
                         memory_space=pltpu.SMEM),
            pl.BlockSpec((tm, d), lambda i: (i, 0)),
            pl.BlockSpec((tm, LANES), lambda i: (i, 0)),
            pl.BlockSpec(memory_space=pl.ANY),
            pl.BlockSpec(gfin.shape, lambda i: (0, 0)),
        ],
        out_specs=pl.BlockSpec((tm, d), lambda i: (i, 0)),
        out_shape=jax.ShapeDtypeStruct((t, d), F32),
        scratch_shapes=[pltpu.VMEM((2, TOP_K, tm, d), F32), pltpu.SemaphoreType.DMA((2,))],
        compiler_params=pltpu.CompilerParams(
            dimension_semantics=("arbitrary",), vmem_limit_bytes=VMEM_LIMIT),
        name="combine",
    )(dest3, dest3, h2, gates, ys, gfin)


def _route_plan(meta, counts, t):
    ids = meta[:, :TOP_K]
    ranks = meta[:, TOP_K:2 * TOP_K]
    cnt = counts[0, :N_EXPERTS].astype(jnp.int32)
    padded = (cnt + EXP_BLK - 1) // EXP_BLK * EXP_BLK
    pad_ends = jnp.cumsum(padded)
    pad_starts = pad_ends - padded
    experts = jnp.arange(N_EXPERTS, dtype=jnp.int32)
    dest = jnp.sum(jnp.where(ids[..., None] == experts, pad_starts, 0), axis=-1) + ranks
    nb = (t * TOP_K + EXP_BLK - 1) // EXP_BLK + N_EXPERTS
    bstart = jnp.arange(nb, dtype=jnp.int32) * EXP_BLK
    last_e = jnp.max(jnp.where(cnt > 0, experts, 0))
    block_e = jnp.minimum(jnp.sum((pad_ends[None, :] <= bstart[:, None]).astype(jnp.int32), axis=1), last_e)
    row_end = jnp.sum(jnp.where(block_e[:, None] == experts, pad_starts + cnt, 0), axis=-1)
    nvalid = jnp.clip(row_end - bstart, 0, EXP_BLK)
    later = (experts[None, :] > experts[:, None]) & (cnt[None, :] > 0)
    next_of = jnp.min(jnp.where(later, experts[None, :], N_EXPERTS), axis=1)
    next_of = jnp.where(next_of == N_EXPERTS, -1, next_of)
    next_e = jnp.sum(jnp.where(block_e[:, None] == experts, next_of, 0), axis=-1)
    return (dest.astype(jnp.int32), block_e.astype(jnp.int32), nvalid.astype(jnp.int32),
            next_e.astype(jnp.int32))


def kernel(x, norm1_g, w_in, lambda_q1, lambda_k1, lambda_q2, lambda_k2, subln_g, w_pool, pool_scale,
           w_out, norm2_g, w_group_router, b_group_router, w_expert_router, b_expert_router,
           w_gate, w_up, w_down, final_g):
    batch, seq, d = x.shape
    t = batch * seq
    qk_w = 2 * N_HEADS * HEAD_DIM
    av_w = N_HEADS * V_DIM
    l = 0
    assert seq % (2 * TK) == 0 and seq % TM_PROJ == 0 and TK == TM_PROJ
    assert t % TM_COMB == 0 and t % TM_DISP == 0
    assert seq % (ATTN_QSUB * TQ) == 0 and ATTN_UNROLL % 2 == 0

    wi = w_in[l]
    wqt = wi[:, :qk_w].T.astype(BF16)
    wk = wi[:, qk_w:2 * qk_w].astype(BF16)
    wv = wi[:, 2 * qk_w:2 * qk_w + av_w].reshape(d, N_HEADS, V_DIM)
    wvt = jnp.pad(wv, ((0, 0), (0, 0), (0, VT_ROWS - V_DIM))).reshape(d, N_HEADS * VT_ROWS).T.astype(BF16)
    wu = wi[:, 2 * qk_w + av_w:].astype(BF16)

    qt, k, vt, u = _in_proj(x.reshape(t, d), norm1_g[l].reshape(1, d), wqt, wk, wvt, wu,
                            batch=batch, seq=seq)

    lam_in = jnp.zeros((8, LANES), F32)
    lam_in = lam_in.at[0, :HEAD_DIM].set(lambda_q1[l]).at[1, :HEAD_DIM].set(lambda_k1[l])
    lam_in = lam_in.at[2, :HEAD_DIM].set(lambda_q2[l]).at[3, :HEAD_DIM].set(lambda_k2[l])
    gsub = jnp.broadcast_to(subln_g[l].reshape(V_DIM, 1), (V_DIM, TQ)).astype(F32)
    n_blocks = (t * TOP_K + EXP_BLK - 1) // EXP_BLK + N_EXPERTS
    a, xs_zero = _attention(lam_in, qt, k, vt, gsub, (n_blocks * EXP_BLK, d // 2),
                            batch=batch, seq=seq)

    wr = jnp.zeros((d, 2 * LANES), F32)
    wr = wr.at[:, :N_GROUPS].set(w_group_router[l])
    wr = wr.at[:, LANES:LANES + N_EXPERTS].set(w_expert_router[l].reshape(d, N_EXPERTS)).astype(BF16)
    br = jnp.full((1, 2 * LANES), NEG, F32)
    br = br.at[0, :N_GROUPS].set(b_group_router[l])
    br = br.at[0, LANES:LANES + N_EXPERTS].set(b_expert_router[l].reshape(N_EXPERTS))
    h, xn2, gates, meta, counts = _mix_route(
        x, a, u.reshape(batch, seq, -1), w_pool[l].astype(BF16), pool_scale[l].reshape(1, -1),
        w_out[l].astype(BF16), norm2_g[l].reshape(1, d), wr, br, batch=batch, seq=seq)

    dest, block_e, nvalid, next_e = _route_plan(meta.reshape(t, LANES), counts, t)
    assert block_e.shape[0] == n_blocks
    xs = _dispatch(dest, xn2.reshape(t, d // 2), xs_zero)
    ys = _experts(block_e, nvalid, next_e, xs, w_gate[l], w_up[l], w_down[l])
    out = _combine(dest, h.reshape(t, d), gates.reshape(t, LANES), ys, final_g.reshape(1, d))
    return out.reshape(batch, seq, d)
```

```python
import functools
import math

import jax
import jax.numpy as jnp
import numpy as np
from jax import lax
from jax.experimental import pallas as pl
from jax.experimental.pallas import tpu as pltpu

F32 = jnp.float32
BF16 = jnp.bfloat16

N_HEADS = 4
HEAD_DIM = 64
V_DIM = 128
POOL_WINDOWS = (2, 4, 8, 16)
N_GROUPS = 4
E_PER_GROUP = 8
N_EXPERTS = N_GROUPS * E_PER_GROUP
TOP_K = 2
EPS = 1e-6
LAM_INIT = 0.8 - 0.6 * math.exp(-0.3 * 0)
NEG = -1e30
LOG2E = math.log2(math.e)


def _bf16_terms(value, n):
    terms, rest = [], value
    for _ in range(n):
        term = float(np.asarray(rest, dtype=jnp.bfloat16))
        terms.append(term)
        rest -= term
    return tuple(terms)


LOG2E_TERMS = _bf16_terms(LOG2E, 3)
N_TERMS = len(LOG2E_TERMS)
SLOPES = tuple(2.0 ** (-8.0 * (h + 1) / N_HEADS) for h in range(N_HEADS))

LANES = 128
QK_PAD = 128
VT_ROWS = 144
POS_SPLIT = 128

TM_PROJ = 512
TQ = 512
TK = 512
ATTN_UNROLL = 6
ATTN_QSUB = 1
EXP_BLK = 512
TM_DISP = 1024
TM_COMB = 512
VMEM_LIMIT = 56 * 1024 * 1024


def _nt_dot(a, b):
    return lax.dot_general(a, b, (((1,), (1,)), ((), ())), preferred_element_type=F32)


def _in_proj_kernel(x_ref, g_ref, wqt_ref, wk_ref, wvt_ref, wu_ref, kc_ref,
                    qt_ref, k_ref, vt_ref, u_ref, *, seq):
    tm = x_ref.shape[0]
    i = pl.program_id(0)
    j0 = (i % (seq // tm)) * tm
    x = x_ref[...]
    ms = jnp.mean(x * x, axis=-1, keepdims=True)
    hn = (x * lax.rsqrt(ms + EPS) * g_ref[...]).astype(BF16)

    qt = _nt_dot(wqt_ref[...], hn) * (LOG2E / math.sqrt(HEAD_DIM))
    zero_rows = jnp.zeros((HEAD_DIM, tm), BF16)
    for g in range(2 * N_HEADS):
        data = qt[g * HEAD_DIM:(g + 1) * HEAD_DIM].astype(BF16)
        lo_rows, hi_rows = (data, zero_rows) if g % 2 == 0 else (zero_rows, data)
        qt_ref[0, g, 0:HEAD_DIM, :] = lo_rows
        qt_ref[0, g, HEAD_DIM:QK_PAD, :] = hi_rows

    kk = jnp.dot(hn, wk_ref[...], preferred_element_type=F32)
    pos = j0 + lax.broadcasted_iota(jnp.int32, (tm, QK_PAD), 0)
    lane = lax.broadcasted_iota(jnp.int32, (tm, QK_PAD), 1)
    hi = (pos & ~(POS_SPLIT - 1)).astype(F32)
    lo = (pos & (POS_SPLIT - 1)).astype(F32)
    for c in range(2):
        f0 = _feature_base(c)
        f_hi, f_lo = f0 + 2 * N_TERMS, f0 + 3 * N_TERMS
        feat = jnp.where((lane >= f_hi) & (lane < f_lo), hi,
                         jnp.where((lane >= f_lo) & (lane < f_lo + N_TERMS), lo, 0.0))
        own = (lane < HEAD_DIM) if c == 0 else (lane >= HEAD_DIM)
        for h in range(N_HEADS):
            g = 2 * h + c
            dims = jnp.where(own, kk[:, h * QK_PAD:(h + 1) * QK_PAD], 0.0)
            k_ref[0, g] = (dims + (feat + kc_ref[g:g + 1, :])).astype(BF16)

    vt = _nt_dot(wvt_ref[...], hn)
    row = lax.broadcasted_iota(jnp.int32, (VT_ROWS, tm), 0)
    for h in range(N_HEADS):
        blk = vt[h * VT_ROWS:(h + 1) * VT_ROWS]
        vt_ref[0, h, 0] = jnp.where(row == V_DIM, 1.0, blk).astype(BF16)

    u_ref[...] = jnp.dot(hn, wu_ref[...], preferred_element_type=F32)


def _feature_base(c):
    return HEAD_DIM if c == 0 else 0


def _key_feature_consts():
    kc = np.zeros((2 * N_HEADS, QK_PAD), np.float32)
    for g in range(2 * N_HEADS):
        for rep in range(2):
            for n, term in enumerate(LOG2E_TERMS):
                kc[g, _feature_base(g % 2) + rep * N_TERMS + n] = term * SLOPES[g // 2]
    return jnp.asarray(kc)


def _in_proj(x2, g1, wqt, wk, wvt, wu, *, batch, seq):
    t, d = x2.shape
    tm = TM_PROJ
    nblk = seq // tm
    kern = functools.partial(_in_proj_kernel, seq=seq)
    full = lambda shape: pl.BlockSpec(shape, lambda i: (0,) * len(shape))
    kc = _key_feature_consts()
    return pl.pallas_call(
        kern,
        grid=(t // tm,),
        in_specs=[
            pl.BlockSpec((tm, d), lambda i: (i, 0)),
            full(g1.shape), full(wqt.shape), full(wk.shape), full(wvt.shape), full(wu.shape),
            full(kc.shape),
        ],
        out_specs=[
            pl.BlockSpec((1, 2 * N_HEADS, QK_PAD, tm), lambda i: (i // nblk, 0, 0, i % nblk)),
            pl.BlockSpec((1, 2 * N_HEADS, tm, QK_PAD), lambda i: (i // nblk, 0, i % nblk, 0)),
            pl.BlockSpec((1, N_HEADS, 1, VT_ROWS, tm), lambda i: (i // nblk, 0, i % nblk, 0, 0)),
            pl.BlockSpec((tm, wu.shape[1]), lambda i: (i, 0)),
        ],
        out_shape=[
            jax.ShapeDtypeStruct((batch, 2 * N_HEADS, QK_PAD, seq), BF16),
            jax.ShapeDtypeStruct((batch, 2 * N_HEADS, seq, QK_PAD), BF16),
            jax.ShapeDtypeStruct((batch, N_HEADS, nblk, VT_ROWS, tm), BF16),
            jax.ShapeDtypeStruct((t, wu.shape[1]), F32),
        ],
        compiler_params=pltpu.CompilerParams(
            dimension_semantics=("arbitrary",), vmem_limit_bytes=VMEM_LIMIT),
        name="in_proj",
    )(x2, g1, wqt, wk, wvt, wu, kc)


def _attention_kernel(lam_ref, qt_ref, k_ref, vt_ref, g_ref, o_ref, zero_hbm,
                      qv_ref, acc_ref, m_ref, sa_ref, sb_ref, mba_ref, mbb_ref, zbuf, zsem):
    step = ((pl.program_id(0) * pl.num_programs(1) + pl.program_id(1)) * pl.num_programs(2)
            + pl.program_id(2))
    zrows = zbuf.shape[0]

    @pl.when(step == 0)
    def _():
        zbuf[...] = jnp.zeros(zbuf.shape, zbuf.dtype)

    zero_copy = pltpu.make_async_copy(zbuf, zero_hbm.at[pl.ds(step * zrows, zrows)], zsem)
    zero_copy.start()

    tq = o_ref.shape[1] // ATTN_QSUB
    for sub in range(ATTN_QSUB):
        _attention_block(
            pl.program_id(2) * ATTN_QSUB + sub, qt_ref.at[:, :, :, sub * tq:(sub + 1) * tq], k_ref, vt_ref,
            qv_ref.at[sub], acc_ref.at[sub], m_ref.at[sub], sa_ref, sb_ref, mba_ref, mbb_ref)
    for sub in range(ATTN_QSUB):
        _attention_finish(lam_ref, g_ref, o_ref.at[:, sub * tq:(sub + 1) * tq, :], acc_ref.at[sub])
    zero_copy.wait()


def _attention_block(qi, qt_ref, k_ref, vt_ref, qv_ref, acc_ref, m_ref, sa_ref, sb_ref, mba_ref, mbb_ref):
    h = pl.program_id(1)
    tq = qt_ref.shape[-1]
    nkb, _, tk = vt_ref.shape[2:]
    jdiag = (qi * tq) // tk

    slope = jnp.where(h == 0, SLOPES[0], jnp.where(h == 1, SLOPES[1],
                      jnp.where(h == 2, SLOPES[2], SLOPES[3]))).astype(F32)

    r = lax.broadcasted_iota(jnp.int32, (QK_PAD, tq), 0)
    ipos = qi * tq + lax.broadcasted_iota(jnp.int32, (QK_PAD, tq), 1)
    ihi = (ipos & ~(POS_SPLIT - 1)).astype(F32)
    ilo = (ipos & (POS_SPLIT - 1)).astype(F32)
    for c in range(2):
        f0 = _feature_base(c)
        fi = jnp.where((r >= f0) & (r < f0 + N_TERMS), ihi,
                       jnp.where((r >= f0 + N_TERMS) & (r < f0 + 2 * N_TERMS), ilo, 0.0))
        fj = jnp.zeros((QK_PAD, tq), F32)
        for rep in range(2):
            for n, term in enumerate(LOG2E_TERMS):
                fj = jnp.where(r == f0 + (2 + rep) * N_TERMS + n, term * slope, fj)
        q = qt_ref[0, c].astype(F32)
        qv_ref[0, c] = (q + fi - fj).astype(BF16)
        qv_ref[1, c] = (q - fi + fj).astype(BF16)

    acc_ref[...] = jnp.zeros(acc_ref.shape, F32)
    m_ref[...] = jnp.full(m_ref.shape, NEG, F32)

    def score_stage(kb, variant, s_ref, mb_ref, bias=None):
        for c in range(2):
            kblk = k_ref[0, c, pl.ds(pl.multiple_of(kb * tk, tk), tk), :]
            qmat = qt_ref[0, c] if variant is None else qv_ref[variant, c]
            st = jnp.dot(kblk, qmat, preferred_element_type=F32)
            if bias is not None:
                st = st + bias
            s_ref[c] = st
            mb_ref[c] = jnp.max(st, axis=0, keepdims=True)

    def softmax_stage(kb, s_ref, mb_ref):
        for c in range(2):
            m_old = m_ref[c]
            m_new = jnp.maximum(m_old, mb_ref[c])
            alpha = jnp.exp2(m_old - m_new)
            p = jnp.exp2(s_ref[c] - m_new).astype(BF16)
            pv = jnp.dot(vt_ref[0, 0, kb], p, preferred_element_type=F32)
            acc_ref[c] = acc_ref[c] * alpha + pv
            m_ref[c] = m_new

    def visit_block(n):
        o = n - 1
        return jnp.where(n == 0, jdiag, o + (o >= jdiag).astype(jnp.int32))

    def visit_variant(kb):
        return jnp.where(kb < jdiag, 1, 0)

    jpos = jdiag * tk + lax.broadcasted_iota(jnp.int32, (tk, tq), 0)
    iposd = qi * tq + lax.broadcasted_iota(jnp.int32, (tk, tq), 1)
    bias = (-LOG2E * slope) * jnp.abs(iposd - jpos).astype(F32)
    score_stage(jdiag, None, sa_ref, mba_ref, bias)

    bufs = ((sa_ref, mba_ref), (sb_ref, mbb_ref))

    def phase(n, parity, with_scores):
        if with_scores:
            kb_next = visit_block(n + 1)
            score_stage(kb_next, visit_variant(kb_next), *bufs[1 - parity])
        softmax_stage(visit_block(n), *bufs[parity])

    def loop_body(t, carry):
        for j in range(ATTN_UNROLL):
            phase(t * ATTN_UNROLL + j, j % 2, True)
        return carry

    trips = (nkb - 1) // ATTN_UNROLL
    lax.fori_loop(0, trips, loop_body, 0)
    for n in range(trips * ATTN_UNROLL, nkb):
        phase(n, n % 2, n + 1 < nkb)


def _attention_finish(lam_ref, g_ref, o_ref, acc_ref):
    lv = lam_ref[...]
    lam = (jnp.exp(jnp.sum(lv[0:1] * lv[1:2], axis=-1, keepdims=True))
           - jnp.exp(jnp.sum(lv[2:3] * lv[3:4], axis=-1, keepdims=True)) + LAM_INIT)
    a0 = acc_ref[0]
    a1 = acc_ref[1]
    o = (a0[:V_DIM] * (1.0 / a0[V_DIM:V_DIM + 1])
         - lam * (a1[:V_DIM] * (1.0 / a1[V_DIM:V_DIM + 1])))
    ms = jnp.mean(o * o, axis=0, keepdims=True)
    y = o * lax.rsqrt(ms + EPS) * g_ref[...] * (1.0 - LAM_INIT)
    o_ref[0] = y.T.astype(BF16)


def _attention(lam_in, qt, k, vt, gsub, zero_shape, *, batch, seq):
    nkb = seq // TK
    steps = batch * N_HEADS * (seq // (ATTN_QSUB * TQ))
    assert zero_shape[0] % steps == 0 and (zero_shape[0] // steps) % 8 == 0
    zrows = zero_shape[0] // steps
    return pl.pallas_call(
        _attention_kernel,
        grid=(batch, N_HEADS, seq // (ATTN_QSUB * TQ)),
        in_specs=[
            pl.BlockSpec(lam_in.shape, lambda b, h, i: (0, 0)),
            pl.BlockSpec((1, 2, QK_PAD, ATTN_QSUB * TQ), lambda b, h, i: (b, h, 0, i)),
            pl.BlockSpec((1, 2, seq, QK_PAD), lambda b, h, i: (b, h, 0, 0)),
            pl.BlockSpec((1, 1, nkb, VT_ROWS, TK), lambda b, h, i: (b, h, 0, 0, 0)),
            pl.BlockSpec(gsub.shape, lambda b, h, i: (0, 0)),
        ],
        out_specs=[pl.BlockSpec((1, ATTN_QSUB * TQ, V_DIM), lambda b, h, i: (b, i, h)),
                   pl.BlockSpec(memory_space=pl.ANY)],
        out_shape=[jax.ShapeDtypeStruct((batch, seq, N_HEADS * V_DIM), BF16),
                   jax.ShapeDtypeStruct(zero_shape, jnp.uint32)],
        scratch_shapes=[
            pltpu.VMEM((ATTN_QSUB, 2, 2, QK_PAD, TQ), BF16),
            pltpu.VMEM((ATTN_QSUB, 2, VT_ROWS, TQ), F32),
            pltpu.VMEM((ATTN_QSUB, 2, 1, TQ), F32),
            pltpu.VMEM((2, TK, TQ), F32),
            pltpu.VMEM((2, TK, TQ), F32),
            pltpu.VMEM((2, 1, TQ), F32),
            pltpu.VMEM((2, 1, TQ), F32),
            pltpu.VMEM((zrows, zero_shape[1]), jnp.uint32),
            pltpu.SemaphoreType.DMA(()),
        ],
        compiler_params=pltpu.CompilerParams(
            dimension_semantics=("arbitrary", "arbitrary", "arbitrary"),
            vmem_limit_bytes=VMEM_LIMIT),
        name="diff_attention",
    )(lam_in, qt, k, vt, gsub)


def _mix_route_kernel(x_ref, a_ref, u_ref, up_ref, un_ref, wp_ref, ps_ref, wo_ref, g2_ref,
                      wr_ref, br_ref, h_ref, xn_ref, gate_ref, id_ref, count_ref, cnt_ref, triu_ref,
                      *, seq):
    tm = x_ref.shape[1]
    j0 = pl.program_id(1) * tm
    halo = up_ref.shape[1]

    @pl.when((pl.program_id(0) == 0) & (pl.program_id(1) == 0))
    def _():
        cnt_ref[...] = jnp.zeros(cnt_ref.shape, F32)
        rr = lax.broadcasted_iota(jnp.int32, (tm, tm), 0)
        cc = lax.broadcasted_iota(jnp.int32, (tm, tm), 1)
        triu_ref[...] = jnp.where(rr < cc, 1.0, 0.0).astype(BF16)

    n = tm + 2 * halo
    ext = jnp.concatenate([jnp.where(j0 > 0, up_ref[0], 0.0), u_ref[0],
                           jnp.where(j0 + tm < seq, un_ref[0], 0.0)], axis=0)
    edge = lax.broadcasted_iota(jnp.int32, (halo, LANES), 0)
    tpos_top = j0 + edge
    tpos_bot = j0 + (tm - halo) + edge
    ys = []
    for g, w in enumerate(POOL_WINDOWS):
        fwd = ext[:, g * LANES:(g + 1) * LANES]
        length = 1
        while 2 * length < w:
            fwd = fwd + pltpu.roll(fwd, n - length, 0)
            length *= 2
        win = (fwd + pltpu.roll(fwd, w // 2, 0))[halo:halo + tm]

        def clipped(tpos, w=w):
            return (jnp.minimum(tpos + w // 2, seq) - jnp.maximum(tpos - w // 2, 0)).astype(F32)

        mean = jnp.concatenate([win[:halo] / clipped(tpos_top), win[halo:tm - halo] * (1.0 / w),
                                win[tm - halo:] / clipped(tpos_bot)], axis=0)
        d = (mean - u_ref[0][:, g * LANES:(g + 1) * LANES]).astype(BF16)
        yg = jnp.dot(d, wp_ref[g], preferred_element_type=F32)
        ys.append(yg * ps_ref[:, g * LANES:(g + 1) * LANES])
    p = jnp.concatenate(ys, axis=-1).astype(BF16)

    mix = jnp.concatenate([a_ref[0], p], axis=-1)
    hcur = x_ref[0] + jnp.dot(mix, wo_ref[...], preferred_element_type=F32)
    h_ref[0] = hcur
    ms = jnp.mean(hcur * hcur, axis=-1, keepdims=True)
    xn = (hcur * lax.rsqrt(ms + EPS) * g2_ref[...]).astype(BF16)
    half = xn.shape[1] // 2
    lo = pltpu.bitcast(xn[:, :half].astype(F32), jnp.uint32) >> 16
    hi = pltpu.bitcast(xn[:, half:].astype(F32), jnp.uint32) & jnp.uint32(0xFFFF0000)
    xn_ref[0] = hi | lo

    logits = _nt_dot(wr_ref[...], xn) + br_ref[...]
    row8 = lax.broadcasted_iota(jnp.int32, (E_PER_GROUP, tm), 0)
    gl = logits[0:E_PER_GROUP]
    gmax = jnp.max(gl, axis=0, keepdims=True)
    gsel = jnp.min(jnp.where(gl == gmax, row8, E_PER_GROUP), axis=0, keepdims=True)
    pg = 1.0 / jnp.sum(jnp.exp(gl - gmax), axis=0, keepdims=True)
    elm = logits[E_PER_GROUP:2 * E_PER_GROUP]
    for g in range(1, N_GROUPS):
        elm = jnp.where(gsel == g, logits[(g + 1) * E_PER_GROUP:(g + 2) * E_PER_GROUP], elm)
    m1 = jnp.max(elm, axis=0, keepdims=True)
    j1 = jnp.min(jnp.where(elm == m1, row8, E_PER_GROUP), axis=0, keepdims=True)
    elm2 = jnp.where(row8 == j1, NEG, elm)
    m2 = jnp.max(elm2, axis=0, keepdims=True)
    j2 = jnp.min(jnp.where(elm2 == m2, row8, E_PER_GROUP), axis=0, keepdims=True)
    i1 = gsel * E_PER_GROUP + j1
    i2 = gsel * E_PER_GROUP + j2
    e2 = jnp.exp(m2 - m1)
    t1 = 1.0 / (1.0 + e2)
    rows = lax.broadcasted_iota(jnp.int32, (LANES, tm), 0)
    gates_t = jnp.where(rows == 0, pg * t1, jnp.where(rows == 1, pg * (e2 * t1), 0.0))
    gate_ref[0] = gates_t.T

    hit1 = rows == i1
    hit2 = rows == i2
    onehot = jnp.where(hit1 | hit2, 1.0, 0.0)
    before = jnp.dot(onehot.astype(BF16), triu_ref[...], preferred_element_type=F32) + cnt_ref[...]
    r1 = jnp.sum(jnp.where(hit1, before, 0.0), axis=0, keepdims=True).astype(jnp.int32)
    r2 = jnp.sum(jnp.where(hit2, before, 0.0), axis=0, keepdims=True).astype(jnp.int32)
    cnt_ref[...] = cnt_ref[...] + jnp.sum(onehot, axis=1, keepdims=True)
    count_ref[...] = cnt_ref[...]
    id_ref[...] = jnp.where(row8 == 0, i1, jnp.where(row8 == 1, i2,
                            jnp.where(row8 == 2, r1, jnp.where(row8 == 3, r2, 0))))


def _mix_route(x, a, u, wp, ps, wo, g2, wr, br, *, batch, seq):
    tm = TM_PROJ
    d = x.shape[-1]
    halo = 8
    nh = tm // halo
    kern = functools.partial(_mix_route_kernel, seq=seq)
    full = lambda arr: pl.BlockSpec(arr.shape, lambda b, i: (0,) * arr.ndim)
    tile = lambda w: pl.BlockSpec((1, tm, w), lambda b, i: (b, i, 0))
    return pl.pallas_call(
        kern,
        grid=(batch, seq // tm),
        in_specs=[
            tile(d), tile(a.shape[-1]), tile(u.shape[-1]),
            pl.BlockSpec((1, halo, u.shape[-1]), lambda b, i: (b, jnp.maximum(i * nh - 1, 0), 0)),
            pl.BlockSpec((1, halo, u.shape[-1]),
                         lambda b, i: (b, jnp.minimum((i + 1) * nh, seq // halo - 1), 0)),
            full(wp), full(ps), full(wo), full(g2), full(wr), full(br),
        ],
        out_specs=[tile(d), tile(d // 2), tile(LANES),
                   pl.BlockSpec((E_PER_GROUP, tm), lambda b, i: (0, b * (seq // tm) + i)),
                   pl.BlockSpec((LANES, 1), lambda b, i: (0, 0))],
        out_shape=[
            jax.ShapeDtypeStruct((batch, seq, d), F32),
            jax.ShapeDtypeStruct((batch, seq, d // 2), jnp.uint32),
            jax.ShapeDtypeStruct((batch, seq, LANES), F32),
            jax.ShapeDtypeStruct((E_PER_GROUP, batch * seq), jnp.int32),
            jax.ShapeDtypeStruct((LANES, 1), F32),
        ],
        scratch_shapes=[pltpu.VMEM((LANES, 1), F32), pltpu.VMEM((tm, tm), BF16)],
        compiler_params=pltpu.CompilerParams(
            dimension_semantics=("arbitrary", "arbitrary"), vmem_limit_bytes=VMEM_LIMIT),
        name="mix_route",
    )(x, a, u, u, u, wp, ps, wo, g2, wr, br)


def _dispatch_kernel(dest_ref, x_ref, xs_in, xs_hbm, sem):
    del xs_in
    tm = x_ref.shape[0]

    for r in range(tm):
        for k in range(TOP_K):
            pltpu.make_async_copy(x_ref.at[r], xs_hbm.at[dest_ref[0, k, r]], sem).start(priority=k)
    for k in range(TOP_K):
        pltpu.make_async_copy(x_ref, xs_hbm.at[pl.ds(0, tm)], sem).wait()


def _dispatch(dest, xn2, xs_zero):
    t, d = xn2.shape
    tm = TM_DISP
    nt = t // tm
    return pl.pallas_call(
        _dispatch_kernel,
        grid=(nt,),
        in_specs=[
            pl.BlockSpec((1, TOP_K, tm), lambda i: (i, 0, 0), memory_space=pltpu.SMEM),
            pl.BlockSpec((tm, d), lambda i: (i, 0)),
            pl.BlockSpec(memory_space=pl.ANY),
        ],
        out_specs=pl.BlockSpec(memory_space=pl.ANY),
        out_shape=jax.ShapeDtypeStruct(xs_zero.shape, xs_zero.dtype),
        input_output_aliases={2: 0},
        scratch_shapes=[pltpu.SemaphoreType.DMA(())],
        compiler_params=pltpu.CompilerParams(
            dimension_semantics=("arbitrary",), vmem_limit_bytes=VMEM_LIMIT),
        name="dispatch",
    )(dest.reshape(TOP_K, nt, tm).transpose(1, 0, 2), xn2, xs_zero)


def _experts_kernel(be_ref, nv_ref, nx_ref, xs_ref, wg_hbm, wu_hbm, wd_hbm, y_ref,
                    wgf, wuf, wdf, wgb, wub, wdb, slot_ref, sem):
    i = pl.program_id(0)
    changed = jnp.logical_or(i == 0, be_ref[i] != be_ref[jnp.maximum(i - 1, 0)])

    def weight_copies(e, slot):
        return (pltpu.make_async_copy(wg_hbm.at[e], wgf.at[slot], sem.at[slot]),
                pltpu.make_async_copy(wu_hbm.at[e], wuf.at[slot], sem.at[slot]),
                pltpu.make_async_copy(wd_hbm.at[e], wdf.at[slot], sem.at[slot]))

    @pl.when(i == 0)
    def _():
        slot_ref[0] = 0
        for cp in weight_copies(be_ref[0], 0):
            cp.start()

    @pl.when(changed)
    def _():
        slot = jnp.where(i == 0, 0, 1 - slot_ref[0])
        slot_ref[0] = slot
        for cp in weight_copies(be_ref[i], slot):
            cp.wait()
        wgb[...] = wgf[slot].astype(BF16)
        wub[...] = wuf[slot].astype(BF16)
        wdb[...] = wdf[slot].astype(BF16)

        @pl.when(nx_ref[i] >= 0)
        def _():
            for cp in weight_copies(nx_ref[i], 1 - slot):
                cp.start()

    blk = xs_ref.shape[0]
    half = blk // 2
    nv = nv_ref[i]

    def mlp(rows):
        words = xs_ref[0:rows, :]
        lo = pltpu.bitcast(words << 16, F32)
        hi = pltpu.bitcast(words & jnp.uint32(0xFFFF0000), F32)
        xb = jnp.concatenate([lo, hi], axis=1).astype(BF16)
        gt = jnp.dot(xb, wgb[...], preferred_element_type=F32)
        up = jnp.dot(xb, wub[...], preferred_element_type=F32)
        hid = (gt * (1.0 / (1.0 + jnp.exp(-gt))) * up).astype(BF16)
        y_ref[0:rows, :] = jnp.dot(hid, wdb[...], preferred_element_type=F32)

    @pl.when(nv > half)
    def _():
        mlp(blk)

    @pl.when(jnp.logical_and(nv > 0, nv <= half))
    def _():
        mlp(half)
        y_ref[half:, :] = jnp.zeros((blk - half, y_ref.shape[1]), F32)

    @pl.when(nv == 0)
    def _():
        y_ref[...] = jnp.zeros(y_ref.shape, F32)


def _experts(block_e, nvalid, next_e, xs, w_gate, w_up, w_down):
    nb = block_e.shape[0]
    d, de = w_gate.shape[-2:]
    grid_spec = pltpu.PrefetchScalarGridSpec(
        num_scalar_prefetch=3,
        grid=(nb,),
        in_specs=[
            pl.BlockSpec((EXP_BLK, xs.shape[-1]), lambda i, be, nv, nx: (i, 0)),
            pl.BlockSpec(memory_space=pl.ANY),
            pl.BlockSpec(memory_space=pl.ANY),
            pl.BlockSpec(memory_space=pl.ANY),
        ],
        out_specs=pl.BlockSpec((EXP_BLK, d), lambda i, be, nv, nx: (i, 0)),
        scratch_shapes=[
            pltpu.VMEM((2, d, de), F32), pltpu.VMEM((2, d, de), F32), pltpu.VMEM((2, de, d), F32),
            pltpu.VMEM((d, de), BF16), pltpu.VMEM((d, de), BF16), pltpu.VMEM((de, d), BF16),
            pltpu.SMEM((1,), jnp.int32), pltpu.SemaphoreType.DMA((2,)),
        ],
    )
    return pl.pallas_call(
        _experts_kernel,
        grid_spec=grid_spec,
        out_shape=jax.ShapeDtypeStruct((nb * EXP_BLK, d), F32),
        compiler_params=pltpu.CompilerParams(
            dimension_semantics=("arbitrary",), vmem_limit_bytes=VMEM_LIMIT),
        name="experts",
    )(block_e, nvalid, next_e, xs, w_gate, w_up, w_down)


def _combine_kernel(dest_ref, dnext_ref, h_ref, gate_ref, y_hbm, g_ref, o_ref, ybuf, sem):
    tm = h_ref.shape[0]
    i = pl.program_id(0)
    slot = i % 2

    def start_rows(idx_ref, s):
        for r in range(tm):
            for k in range(TOP_K):
                pltpu.make_async_copy(y_hbm.at[idx_ref[0, k, r]], ybuf.at[s, k, r], sem.at[s]).start(priority=k)

    @pl.when(i == 0)
    def _():
        start_rows(dest_ref, 0)

    @pl.when(i + 1 < pl.num_programs(0))
    def _():
        start_rows(dnext_ref, 1 - slot)

    for k in range(TOP_K):
        pltpu.make_async_copy(y_hbm.at[pl.ds(0, tm)], ybuf.at[slot, k], sem.at[slot]).wait()
    gates = gate_ref[...]
    hcur = h_ref[...] + (gates[:, 0:1] * ybuf[slot, 0] + gates[:, 1:2] * ybuf[slot, 1])
    ms = jnp.mean(hcur * hcur, axis=-1, keepdims=True)
    o_ref[...] = hcur * lax.rsqrt(ms + EPS) * g_ref[...]


def _combine(dest, h2, gates, ys, gfin):
    t, d = h2.shape
    tm = TM_COMB
    nt = t // tm
    dest3 = dest.reshape(TOP_K, nt, tm).transpose(1, 0, 2)
    return pl.pallas_call(
        _combine_kernel,
        grid=(nt,),
        in_specs=[
            pl.BlockSpec((1, TOP_K, tm), lambda i: (i, 0, 0), memory_space=pltpu.SMEM),
            pl.BlockSpec((1, TOP_K, tm), lambda i: (jnp.minimum(i + 1, nt - 1), 0, 0),
                         memory_space=pltpu.SMEM),
            pl.BlockSpec((tm, d), lambda i: (i, 0)),
            pl.BlockSpec((tm, LANES), lambda i: (i, 0)),
            pl.BlockSpec(memory_space=pl.ANY),
            pl.BlockSpec(gfin.shape, lambda i: (0, 0)),
        ],
        out_specs=pl.BlockSpec((tm, d), lambda i: (i, 0)),
        out_shape=jax.ShapeDtypeStruct((t, d), F32),
        scratch_shapes=[pltpu.VMEM((2, TOP_K, tm, d), F32), pltpu.SemaphoreType.DMA((2,))],
        compiler_params=pltpu.CompilerParams(
            dimension_semantics=("arbitrary",), vmem_limit_bytes=VMEM_LIMIT),
        name="combine",
    )(dest3, dest3, h2, gates, ys, gfin)


def _route_plan(meta, counts, t):
    ids = meta[:TOP_K]
    ranks = meta[TOP_K:2 * TOP_K]
    cnt = counts[:N_EXPERTS, 0].astype(jnp.int32)
    padded = (cnt + EXP_BLK - 1) // EXP_BLK * EXP_BLK
    pad_ends = jnp.cumsum(padded)
    pad_starts = pad_ends - padded
    experts = jnp.arange(N_EXPERTS, dtype=jnp.int32)
    dest = ranks
    for e in range(N_EXPERTS):
        dest = dest + jnp.where(ids == e, pad_starts[e], 0)
    nb = (t * TOP_K + EXP_BLK - 1) // EXP_BLK + N_EXPERTS
    bstart = jnp.arange(nb, dtype=jnp.int32) * EXP_BLK
    last_e = jnp.max(jnp.where(cnt > 0, experts, 0))
    block_e = jnp.minimum(jnp.sum((pad_ends[None, :] <= bstart[:, None]).astype(jnp.int32), axis=1), last_e)
    row_end = jnp.sum(jnp.where(block_e[:, None] == experts, pad_starts + cnt, 0), axis=-1)
    nvalid = jnp.clip(row_end - bstart, 0, EXP_BLK)
    later = (experts[None, :] > experts[:, None]) & (cnt[None, :] > 0)
    next_of = jnp.min(jnp.where(later, experts[None, :], N_EXPERTS), axis=1)
    next_of = jnp.where(next_of == N_EXPERTS, -1, next_of)
    next_e = jnp.sum(jnp.where(block_e[:, None] == experts, next_of, 0), axis=-1)
    return (dest.astype(jnp.int32), block_e.astype(jnp.int32), nvalid.astype(jnp.int32),
            next_e.astype(jnp.int32))


def kernel(x, norm1_g, w_in, lambda_q1, lambda_k1, lambda_q2, lambda_k2, subln_g, w_pool, pool_scale,
           w_out, norm2_g, w_group_router, b_group_router, w_expert_router, b_expert_router,
           w_gate, w_up, w_down, final_g):
    batch, seq, d = x.shape
    t = batch * seq
    qk_w = 2 * N_HEADS * HEAD_DIM
    av_w = N_HEADS * V_DIM
    l = 0
    assert seq % (2 * TK) == 0 and seq % TM_PROJ == 0 and TK == TM_PROJ
    assert t % TM_COMB == 0 and t % TM_DISP == 0
    assert seq % (ATTN_QSUB * TQ) == 0 and ATTN_UNROLL % 2 == 0

    wi = w_in[l]
    wqt = wi[:, :qk_w].T.astype(BF16)
    wk = wi[:, qk_w:2 * qk_w].astype(BF16)
    wv = wi[:, 2 * qk_w:2 * qk_w + av_w].reshape(d, N_HEADS, V_DIM)
    wvt = jnp.pad(wv, ((0, 0), (0, 0), (0, VT_ROWS - V_DIM))).reshape(d, N_HEADS * VT_ROWS).T.astype(BF16)
    wu = wi[:, 2 * qk_w + av_w:].astype(BF16)

    qt, k, vt, u = _in_proj(x.reshape(t, d), norm1_g[l].reshape(1, d), wqt, wk, wvt, wu,
                            batch=batch, seq=seq)

    lam_in = jnp.zeros((8, LANES), F32)
    lam_in = lam_in.at[0, :HEAD_DIM].set(lambda_q1[l]).at[1, :HEAD_DIM].set(lambda_k1[l])
    lam_in = lam_in.at[2, :HEAD_DIM].set(lambda_q2[l]).at[3, :HEAD_DIM].set(lambda_k2[l])
    gsub = jnp.broadcast_to(subln_g[l].reshape(V_DIM, 1), (V_DIM, TQ)).astype(F32)
    n_blocks = (t * TOP_K + EXP_BLK - 1) // EXP_BLK + N_EXPERTS
    a, xs_zero = _attention(lam_in, qt, k, vt, gsub, (n_blocks * EXP_BLK, d // 2),
                            batch=batch, seq=seq)

    n_router_rows = E_PER_GROUP + N_EXPERTS + E_PER_GROUP
    wr = jnp.zeros((n_router_rows, d), F32)
    wr = wr.at[:N_GROUPS].set(w_group_router[l].T)
    wr = wr.at[E_PER_GROUP:E_PER_GROUP + N_EXPERTS].set(w_expert_router[l].reshape(d, N_EXPERTS).T).astype(BF16)
    br = jnp.full((n_router_rows, 1), NEG, F32)
    br = br.at[:N_GROUPS, 0].set(b_group_router[l])
    br = br.at[E_PER_GROUP:E_PER_GROUP + N_EXPERTS, 0].set(b_expert_router[l].reshape(N_EXPERTS))
    h, xn2, gates, meta, counts = _mix_route(
        x, a, u.reshape(batch, seq, -1), w_pool[l].astype(BF16), pool_scale[l].reshape(1, -1),
        w_out[l].astype(BF16), norm2_g[l].reshape(1, d), wr, br, batch=batch, seq=seq)

    dest, block_e, nvalid, next_e = _route_plan(meta, counts, t)
    assert block_e.shape[0] == n_blocks
    xs = _dispatch(dest, xn2.reshape(t, d // 2), xs_zero)
    ys = _experts(block_e, nvalid, next_e, xs, w_gate[l], w_up[l], w_down[l])
    out = _combine(dest, h.reshape(t, d), gates.reshape(t, LANES), ys, final_g.reshape(1, d))
    return out.reshape(batch, seq, d)
```

```python
import functools
import math

import jax
import jax.numpy as jnp
import numpy as np
from jax import lax
from jax.experimental import pallas as pl
from jax.experimental.pallas import tpu as pltpu

F32 = jnp.float32
BF16 = jnp.bfloat16

N_HEADS = 4
HEAD_DIM = 64
V_DIM = 128
POOL_WINDOWS = (2, 4, 8, 16)
N_GROUPS = 4
E_PER_GROUP = 8
N_EXPERTS = N_GROUPS * E_PER_GROUP
TOP_K = 2
EPS = 1e-6
LAM_INIT = 0.8 - 0.6 * math.exp(-0.3 * 0)
NEG = -1e30
LOG2E = math.log2(math.e)


def _bf16_terms(value, n):
    terms, rest = [], value
    for _ in range(n):
        term = float(np.asarray(rest, dtype=jnp.bfloat16))
        terms.append(term)
        rest -= term
    return tuple(terms)


LOG2E_TERMS = _bf16_terms(LOG2E, 3)
N_TERMS = len(LOG2E_TERMS)
SLOPES = tuple(2.0 ** (-8.0 * (h + 1) / N_HEADS) for h in range(N_HEADS))

LANES = 128
QK_PAD = 128
VT_ROWS = 144
POS_SPLIT = 128

TM_PROJ = 512
TQ = 512
TK = 512
ATTN_UNROLL = 6
ATTN_QSUB = 1
EXP_BLK = 512
EXP_GRAIN = 256
TM_DISP = 1024
TM_COMB = 512
VMEM_LIMIT = 56 * 1024 * 1024


def _nt_dot(a, b):
    return lax.dot_general(a, b, (((1,), (1,)), ((), ())), preferred_element_type=F32)


def _in_proj_kernel(x_ref, g_ref, wqt_ref, wk_ref, wvt_ref, wu_ref, kc_ref,
                    qt_ref, k_ref, vt_ref, u_ref, *, seq):
    tm = x_ref.shape[0]
    i = pl.program_id(0)
    j0 = (i % (seq // tm)) * tm
    x = x_ref[...]
    ms = jnp.mean(x * x, axis=-1, keepdims=True)
    hn = (x * lax.rsqrt(ms + EPS) * g_ref[...]).astype(BF16)

    qt = _nt_dot(wqt_ref[...], hn) * (LOG2E / math.sqrt(HEAD_DIM))
    zero_rows = jnp.zeros((HEAD_DIM, tm), BF16)
    for g in range(2 * N_HEADS):
        data = qt[g * HEAD_DIM:(g + 1) * HEAD_DIM].astype(BF16)
        lo_rows, hi_rows = (data, zero_rows) if g % 2 == 0 else (zero_rows, data)
        qt_ref[0, g, 0:HEAD_DIM, :] = lo_rows
        qt_ref[0, g, HEAD_DIM:QK_PAD, :] = hi_rows

    kk = jnp.dot(hn, wk_ref[...], preferred_element_type=F32)
    pos = j0 + lax.broadcasted_iota(jnp.int32, (tm, QK_PAD), 0)
    lane = lax.broadcasted_iota(jnp.int32, (tm, QK_PAD), 1)
    hi = (pos & ~(POS_SPLIT - 1)).astype(F32)
    lo = (pos & (POS_SPLIT - 1)).astype(F32)
    for c in range(2):
        f0 = _feature_base(c)
        f_hi, f_lo = f0 + 2 * N_TERMS, f0 + 3 * N_TERMS
        feat = jnp.where((lane >= f_hi) & (lane < f_lo), hi,
                         jnp.where((lane >= f_lo) & (lane < f_lo + N_TERMS), lo, 0.0))
        own = (lane < HEAD_DIM) if c == 0 else (lane >= HEAD_DIM)
        for h in range(N_HEADS):
            g = 2 * h + c
            dims = jnp.where(own, kk[:, h * QK_PAD:(h + 1) * QK_PAD], 0.0)
            k_ref[0, g] = (dims + (feat + kc_ref[g:g + 1, :])).astype(BF16)

    vt = _nt_dot(wvt_ref[...], hn)
    row = lax.broadcasted_iota(jnp.int32, (VT_ROWS, tm), 0)
    for h in range(N_HEADS):
        blk = vt[h * VT_ROWS:(h + 1) * VT_ROWS]
        vt_ref[0, h, 0] = jnp.where(row == V_DIM, 1.0, blk).astype(BF16)

    u_ref[...] = jnp.dot(hn, wu_ref[...], preferred_element_type=F32)


def _feature_base(c):
    return HEAD_DIM if c == 0 else 0


def _key_feature_consts():
    kc = np.zeros((2 * N_HEADS, QK_PAD), np.float32)
    for g in range(2 * N_HEADS):
        for rep in range(2):
            for n, term in enumerate(LOG2E_TERMS):
                kc[g, _feature_base(g % 2) + rep * N_TERMS + n] = term * SLOPES[g // 2]
    return jnp.asarray(kc)


def _in_proj(x2, g1, wqt, wk, wvt, wu, *, batch, seq):
    t, d = x2.shape
    tm = TM_PROJ
    nblk = seq // tm
    kern = functools.partial(_in_proj_kernel, seq=seq)
    full = lambda shape: pl.BlockSpec(shape, lambda i: (0,) * len(shape))
    kc = _key_feature_consts()
    return pl.pallas_call(
        kern,
        grid=(t // tm,),
        in_specs=[
            pl.BlockSpec((tm, d), lambda i: (i, 0)),
            full(g1.shape), full(wqt.shape), full(wk.shape), full(wvt.shape), full(wu.shape),
            full(kc.shape),
        ],
        out_specs=[
            pl.BlockSpec((1, 2 * N_HEADS, QK_PAD, tm), lambda i: (i // nblk, 0, 0, i % nblk)),
            pl.BlockSpec((1, 2 * N_HEADS, tm, QK_PAD), lambda i: (i // nblk, 0, i % nblk, 0)),
            pl.BlockSpec((1, N_HEADS, 1, VT_ROWS, tm), lambda i: (i // nblk, 0, i % nblk, 0, 0)),
            pl.BlockSpec((tm, wu.shape[1]), lambda i: (i, 0)),
        ],
        out_shape=[
            jax.ShapeDtypeStruct((batch, 2 * N_HEADS, QK_PAD, seq), BF16),
            jax.ShapeDtypeStruct((batch, 2 * N_HEADS, seq, QK_PAD), BF16),
            jax.ShapeDtypeStruct((batch, N_HEADS, nblk, VT_ROWS, tm), BF16),
            jax.ShapeDtypeStruct((t, wu.shape[1]), F32),
        ],
        compiler_params=pltpu.CompilerParams(
            dimension_semantics=("arbitrary",), vmem_limit_bytes=VMEM_LIMIT),
        name="in_proj",
    )(x2, g1, wqt, wk, wvt, wu, kc)


def _attention_kernel(lam_ref, qt_ref, k_ref, vt_ref, g_ref, o_ref, zero_hbm,
                      qv_ref, acc_ref, m_ref, sa_ref, sb_ref, mba_ref, mbb_ref, zbuf, zsem):
    step = ((pl.program_id(0) * pl.num_programs(1) + pl.program_id(1)) * pl.num_programs(2)
            + pl.program_id(2))
    zrows = zbuf.shape[0]

    @pl.when(step == 0)
    def _():
        zbuf[...] = jnp.zeros(zbuf.shape, zbuf.dtype)

    zero_copy = pltpu.make_async_copy(zbuf, zero_hbm.at[pl.ds(step * zrows, zrows)], zsem)
    zero_copy.start()

    tq = o_ref.shape[1] // ATTN_QSUB
    for sub in range(ATTN_QSUB):
        _attention_block(
            pl.program_id(2) * ATTN_QSUB + sub, qt_ref.at[:, :, :, sub * tq:(sub + 1) * tq], k_ref, vt_ref,
            qv_ref.at[sub], acc_ref.at[sub], m_ref.at[sub], sa_ref, sb_ref, mba_ref, mbb_ref)
    for sub in range(ATTN_QSUB):
        _attention_finish(lam_ref, g_ref, o_ref.at[:, sub * tq:(sub + 1) * tq, :], acc_ref.at[sub])
    zero_copy.wait()


def _attention_block(qi, qt_ref, k_ref, vt_ref, qv_ref, acc_ref, m_ref, sa_ref, sb_ref, mba_ref, mbb_ref):
    h = pl.program_id(1)
    tq = qt_ref.shape[-1]
    nkb, _, tk = vt_ref.shape[2:]
    jdiag = (qi * tq) // tk

    slope = jnp.where(h == 0, SLOPES[0], jnp.where(h == 1, SLOPES[1],
                      jnp.where(h == 2, SLOPES[2], SLOPES[3]))).astype(F32)

    r = lax.broadcasted_iota(jnp.int32, (QK_PAD, tq), 0)
    ipos = qi * tq + lax.broadcasted_iota(jnp.int32, (QK_PAD, tq), 1)
    ihi = (ipos & ~(POS_SPLIT - 1)).astype(F32)
    ilo = (ipos & (POS_SPLIT - 1)).astype(F32)
    for c in range(2):
        f0 = _feature_base(c)
        fi = jnp.where((r >= f0) & (r < f0 + N_TERMS), ihi,
                       jnp.where((r >= f0 + N_TERMS) & (r < f0 + 2 * N_TERMS), ilo, 0.0))
        fj = jnp.zeros((QK_PAD, tq), F32)
        for rep in range(2):
            for n, term in enumerate(LOG2E_TERMS):
                fj = jnp.where(r == f0 + (2 + rep) * N_TERMS + n, term * slope, fj)
        q = qt_ref[0, c].astype(F32)
        qv_ref[0, c] = (q + fi - fj).astype(BF16)
        qv_ref[1, c] = (q - fi + fj).astype(BF16)

    acc_ref[...] = jnp.zeros(acc_ref.shape, F32)
    m_ref[...] = jnp.full(m_ref.shape, NEG, F32)

    def score_stage(kb, variant, s_ref, mb_ref, bias=None):
        for c in range(2):
            kblk = k_ref[0, c, pl.ds(pl.multiple_of(kb * tk, tk), tk), :]
            qmat = qt_ref[0, c] if variant is None else qv_ref[variant, c]
            st = jnp.dot(kblk, qmat, preferred_element_type=F32)
            if bias is not None:
                st = st + bias
            s_ref[c] = st
            mb_ref[c] = jnp.max(st, axis=0, keepdims=True)

    def softmax_stage(kb, s_ref, mb_ref):
        for c in range(2):
            m_old = m_ref[c]
            m_new = jnp.maximum(m_old, mb_ref[c])
            alpha = jnp.exp2(m_old - m_new)
            p = jnp.exp2(s_ref[c] - m_new).astype(BF16)
            pv = jnp.dot(vt_ref[0, 0, kb], p, preferred_element_type=F32)
            acc_ref[c] = acc_ref[c] * alpha + pv
            m_ref[c] = m_new

    def visit_block(n):
        o = n - 1
        return jnp.where(n == 0, jdiag, o + (o >= jdiag).astype(jnp.int32))

    def visit_variant(kb):
        return jnp.where(kb < jdiag, 1, 0)

    jpos = jdiag * tk + lax.broadcasted_iota(jnp.int32, (tk, tq), 0)
    iposd = qi * tq + lax.broadcasted_iota(jnp.int32, (tk, tq), 1)
    bias = (-LOG2E * slope) * jnp.abs(iposd - jpos).astype(F32)
    score_stage(jdiag, None, sa_ref, mba_ref, bias)

    bufs = ((sa_ref, mba_ref), (sb_ref, mbb_ref))

    def phase(n, parity, with_scores):
        if with_scores:
            kb_next = visit_block(n + 1)
            score_stage(kb_next, visit_variant(kb_next), *bufs[1 - parity])
        softmax_stage(visit_block(n), *bufs[parity])

    def loop_body(t, carry):
        for j in range(ATTN_UNROLL):
            phase(t * ATTN_UNROLL + j, j % 2, True)
        return carry

    trips = (nkb - 1) // ATTN_UNROLL
    lax.fori_loop(0, trips, loop_body, 0)
    for n in range(trips * ATTN_UNROLL, nkb):
        phase(n, n % 2, n + 1 < nkb)


def _attention_finish(lam_ref, g_ref, o_ref, acc_ref):
    lv = lam_ref[...]
    lam = (jnp.exp(jnp.sum(lv[0:1] * lv[1:2], axis=-1, keepdims=True))
           - jnp.exp(jnp.sum(lv[2:3] * lv[3:4], axis=-1, keepdims=True)) + LAM_INIT)
    a0 = acc_ref[0]
    a1 = acc_ref[1]
    o = (a0[:V_DIM] * (1.0 / a0[V_DIM:V_DIM + 1])
         - lam * (a1[:V_DIM] * (1.0 / a1[V_DIM:V_DIM + 1])))
    ms = jnp.mean(o * o, axis=0, keepdims=True)
    y = o * lax.rsqrt(ms + EPS) * g_ref[...] * (1.0 - LAM_INIT)
    o_ref[0] = y.T.astype(BF16)


def _attention(lam_in, qt, k, vt, gsub, zero_shape, *, batch, seq):
    nkb = seq // TK
    steps = batch * N_HEADS * (seq // (ATTN_QSUB * TQ))
    assert zero_shape[0] % steps == 0 and (zero_shape[0] // steps) % 8 == 0
    zrows = zero_shape[0] // steps
    return pl.pallas_call(
        _attention_kernel,
        grid=(batch, N_HEADS, seq // (ATTN_QSUB * TQ)),
        in_specs=[
            pl.BlockSpec(lam_in.shape, lambda b, h, i: (0, 0)),
            pl.BlockSpec((1, 2, QK_PAD, ATTN_QSUB * TQ), lambda b, h, i: (b, h, 0, i)),
            pl.BlockSpec((1, 2, seq, QK_PAD), lambda b, h, i: (b, h, 0, 0)),
            pl.BlockSpec((1, 1, nkb, VT_ROWS, TK), lambda b, h, i: (b, h, 0, 0, 0)),
            pl.BlockSpec(gsub.shape, lambda b, h, i: (0, 0)),
        ],
        out_specs=[pl.BlockSpec((1, ATTN_QSUB * TQ, V_DIM), lambda b, h, i: (b, i, h)),
                   pl.BlockSpec(memory_space=pl.ANY)],
        out_shape=[jax.ShapeDtypeStruct((batch, seq, N_HEADS * V_DIM), BF16),
                   jax.ShapeDtypeStruct(zero_shape, jnp.uint32)],
        scratch_shapes=[
            pltpu.VMEM((ATTN_QSUB, 2, 2, QK_PAD, TQ), BF16),
            pltpu.VMEM((ATTN_QSUB, 2, VT_ROWS, TQ), F32),
            pltpu.VMEM((ATTN_QSUB, 2, 1, TQ), F32),
            pltpu.VMEM((2, TK, TQ), F32),
            pltpu.VMEM((2, TK, TQ), F32),
            pltpu.VMEM((2, 1, TQ), F32),
            pltpu.VMEM((2, 1, TQ), F32),
            pltpu.VMEM((zrows, zero_shape[1]), jnp.uint32),
            pltpu.SemaphoreType.DMA(()),
        ],
        compiler_params=pltpu.CompilerParams(
            dimension_semantics=("arbitrary", "arbitrary", "arbitrary"),
            vmem_limit_bytes=VMEM_LIMIT),
        name="diff_attention",
    )(lam_in, qt, k, vt, gsub)


def _mix_route_kernel(x_ref, a_ref, u_ref, up_ref, un_ref, wp_ref, ps_ref, wo_ref, g2_ref,
                      wr_ref, br_ref, h_ref, xn_ref, gate_ref, id_ref, count_ref, cnt_ref, triu_ref,
                      *, seq):
    tm = x_ref.shape[1]
    j0 = pl.program_id(1) * tm
    halo = up_ref.shape[1]

    @pl.when((pl.program_id(0) == 0) & (pl.program_id(1) == 0))
    def _():
        cnt_ref[...] = jnp.zeros(cnt_ref.shape, F32)
        rr = lax.broadcasted_iota(jnp.int32, (tm, tm), 0)
        cc = lax.broadcasted_iota(jnp.int32, (tm, tm), 1)
        triu_ref[...] = jnp.where(rr < cc, 1.0, 0.0).astype(BF16)

    n = tm + 2 * halo
    ext = jnp.concatenate([jnp.where(j0 > 0, up_ref[0], 0.0), u_ref[0],
                           jnp.where(j0 + tm < seq, un_ref[0], 0.0)], axis=0)
    edge = lax.broadcasted_iota(jnp.int32, (halo, LANES), 0)
    tpos_top = j0 + edge
    tpos_bot = j0 + (tm - halo) + edge
    ys = []
    for g, w in enumerate(POOL_WINDOWS):
        fwd = ext[:, g * LANES:(g + 1) * LANES]
        length = 1
        while 2 * length < w:
            fwd = fwd + pltpu.roll(fwd, n - length, 0)
            length *= 2
        win = (fwd + pltpu.roll(fwd, w // 2, 0))[halo:halo + tm]

        def clipped(tpos, w=w):
            return (jnp.minimum(tpos + w // 2, seq) - jnp.maximum(tpos - w // 2, 0)).astype(F32)

        mean = jnp.concatenate([win[:halo] / clipped(tpos_top), win[halo:tm - halo] * (1.0 / w),
                                win[tm - halo:] / clipped(tpos_bot)], axis=0)
        d = (mean - u_ref[0][:, g * LANES:(g + 1) * LANES]).astype(BF16)
        yg = jnp.dot(d, wp_ref[g], preferred_element_type=F32)
        ys.append(yg * ps_ref[:, g * LANES:(g + 1) * LANES])
    p = jnp.concatenate(ys, axis=-1).astype(BF16)

    mix = jnp.concatenate([a_ref[0], p], axis=-1)
    hcur = x_ref[0] + jnp.dot(mix, wo_ref[...], preferred_element_type=F32)
    h_ref[0] = hcur
    ms = jnp.mean(hcur * hcur, axis=-1, keepdims=True)
    xn = (hcur * lax.rsqrt(ms + EPS) * g2_ref[...]).astype(BF16)
    half = xn.shape[1] // 2
    lo = pltpu.bitcast(xn[:, :half].astype(F32), jnp.uint32) >> 16
    hi = pltpu.bitcast(xn[:, half:].astype(F32), jnp.uint32) & jnp.uint32(0xFFFF0000)
    xn_ref[0] = hi | lo

    logits = _nt_dot(wr_ref[...], xn) + br_ref[...]
    row8 = lax.broadcasted_iota(jnp.int32, (E_PER_GROUP, tm), 0)
    gl = logits[0:E_PER_GROUP]
    gmax = jnp.max(gl, axis=0, keepdims=True)
    gsel = jnp.min(jnp.where(gl == gmax, row8, E_PER_GROUP), axis=0, keepdims=True)
    pg = 1.0 / jnp.sum(jnp.exp(gl - gmax), axis=0, keepdims=True)
    elm = logits[E_PER_GROUP:2 * E_PER_GROUP]
    for g in range(1, N_GROUPS):
        elm = jnp.where(gsel == g, logits[(g + 1) * E_PER_GROUP:(g + 2) * E_PER_GROUP], elm)
    m1 = jnp.max(elm, axis=0, keepdims=True)
    j1 = jnp.min(jnp.where(elm == m1, row8, E_PER_GROUP), axis=0, keepdims=True)
    elm2 = jnp.where(row8 == j1, NEG, elm)
    m2 = jnp.max(elm2, axis=0, keepdims=True)
    j2 = jnp.min(jnp.where(elm2 == m2, row8, E_PER_GROUP), axis=0, keepdims=True)
    i1 = gsel * E_PER_GROUP + j1
    i2 = gsel * E_PER_GROUP + j2
    e2 = jnp.exp(m2 - m1)
    t1 = 1.0 / (1.0 + e2)
    rows = lax.broadcasted_iota(jnp.int32, (LANES, tm), 0)
    gates_t = jnp.where(rows == 0, pg * t1, jnp.where(rows == 1, pg * (e2 * t1), 0.0))
    gate_ref[0] = gates_t.T

    hit1 = rows == i1
    hit2 = rows == i2
    onehot = jnp.where(hit1 | hit2, 1.0, 0.0)
    before = jnp.dot(onehot.astype(BF16), triu_ref[...], preferred_element_type=F32) + cnt_ref[...]
    r1 = jnp.sum(jnp.where(hit1, before, 0.0), axis=0, keepdims=True).astype(jnp.int32)
    r2 = jnp.sum(jnp.where(hit2, before, 0.0), axis=0, keepdims=True).astype(jnp.int32)
    cnt_ref[...] = cnt_ref[...] + jnp.sum(onehot, axis=1, keepdims=True)
    count_ref[...] = cnt_ref[...]
    id_ref[...] = jnp.where(row8 == 0, i1, jnp.where(row8 == 1, i2,
                            jnp.where(row8 == 2, r1, jnp.where(row8 == 3, r2, 0))))


def _mix_route(x, a, u, wp, ps, wo, g2, wr, br, *, batch, seq):
    tm = TM_PROJ
    d = x.shape[-1]
    halo = 8
    nh = tm // halo
    kern = functools.partial(_mix_route_kernel, seq=seq)
    full = lambda arr: pl.BlockSpec(arr.shape, lambda b, i: (0,) * arr.ndim)
    tile = lambda w: pl.BlockSpec((1, tm, w), lambda b, i: (b, i, 0))
    return pl.pallas_call(
        kern,
        grid=(batch, seq // tm),
        in_specs=[
            tile(d), tile(a.shape[-1]), tile(u.shape[-1]),
            pl.BlockSpec((1, halo, u.shape[-1]), lambda b, i: (b, jnp.maximum(i * nh - 1, 0), 0)),
            pl.BlockSpec((1, halo, u.shape[-1]),
                         lambda b, i: (b, jnp.minimum((i + 1) * nh, seq // halo - 1), 0)),
            full(wp), full(ps), full(wo), full(g2), full(wr), full(br),
        ],
        out_specs=[tile(d), tile(d // 2), tile(LANES),
                   pl.BlockSpec((E_PER_GROUP, tm), lambda b, i: (0, b * (seq // tm) + i)),
                   pl.BlockSpec((LANES, 1), lambda b, i: (0, 0))],
        out_shape=[
            jax.ShapeDtypeStruct((batch, seq, d), F32),
            jax.ShapeDtypeStruct((batch, seq, d // 2), jnp.uint32),
            jax.ShapeDtypeStruct((batch, seq, LANES), F32),
            jax.ShapeDtypeStruct((E_PER_GROUP, batch * seq), jnp.int32),
            jax.ShapeDtypeStruct((LANES, 1), F32),
        ],
        scratch_shapes=[pltpu.VMEM((LANES, 1), F32), pltpu.VMEM((tm, tm), BF16)],
        compiler_params=pltpu.CompilerParams(
            dimension_semantics=("arbitrary", "arbitrary"), vmem_limit_bytes=VMEM_LIMIT),
        name="mix_route",
    )(x, a, u, u, u, wp, ps, wo, g2, wr, br)


def _dispatch_kernel(dest_ref, x_ref, xs_in, xs_hbm, sem):
    del xs_in
    tm = x_ref.shape[0]

    for r in range(tm):
        for k in range(TOP_K):
            pltpu.make_async_copy(x_ref.at[r], xs_hbm.at[dest_ref[0, k, r]], sem).start(priority=k)
    for k in range(TOP_K):
        pltpu.make_async_copy(x_ref, xs_hbm.at[pl.ds(0, tm)], sem).wait()


def _dispatch(dest, xn2, xs_zero):
    t, d = xn2.shape
    tm = TM_DISP
    nt = t // tm
    return pl.pallas_call(
        _dispatch_kernel,
        grid=(nt,),
        in_specs=[
            pl.BlockSpec((1, TOP_K, tm), lambda i: (i, 0, 0), memory_space=pltpu.SMEM),
            pl.BlockSpec((tm, d), lambda i: (i, 0)),
            pl.BlockSpec(memory_space=pl.ANY),
        ],
        out_specs=pl.BlockSpec(memory_space=pl.ANY),
        out_shape=jax.ShapeDtypeStruct(xs_zero.shape, xs_zero.dtype),
        input_output_aliases={2: 0},
        scratch_shapes=[pltpu.SemaphoreType.DMA(())],
        compiler_params=pltpu.CompilerParams(
            dimension_semantics=("arbitrary",), vmem_limit_bytes=VMEM_LIMIT),
        name="dispatch",
    )(dest.reshape(TOP_K, nt, tm).transpose(1, 0, 2), xn2, xs_zero)


def _experts_kernel(be_ref, nv_ref, nx_ref, xs_ref, wg_hbm, wu_hbm, wd_hbm, y_ref,
                    wgf, wuf, wdf, wgb, wub, wdb, slot_ref, sem):
    i = pl.program_id(0)
    changed = jnp.logical_or(i == 0, be_ref[i] != be_ref[jnp.maximum(i - 1, 0)])

    def weight_copies(e, slot):
        return (pltpu.make_async_copy(wg_hbm.at[e], wgf.at[slot], sem.at[slot]),
                pltpu.make_async_copy(wu_hbm.at[e], wuf.at[slot], sem.at[slot]),
                pltpu.make_async_copy(wd_hbm.at[e], wdf.at[slot], sem.at[slot]))

    @pl.when(i == 0)
    def _():
        slot_ref[0] = 0
        for cp in weight_copies(be_ref[0], 0):
            cp.start()

    @pl.when(changed)
    def _():
        slot = jnp.where(i == 0, 0, 1 - slot_ref[0])
        slot_ref[0] = slot
        for cp in weight_copies(be_ref[i], slot):
            cp.wait()
        wgb[...] = wgf[slot].astype(BF16)
        wub[...] = wuf[slot].astype(BF16)
        wdb[...] = wdf[slot].astype(BF16)

        @pl.when(nx_ref[i] >= 0)
        def _():
            for cp in weight_copies(nx_ref[i], 1 - slot):
                cp.start()

    blk = xs_ref.shape[0]
    nv = nv_ref[i]

    def mlp(rows):
        words = xs_ref[0:rows, :]
        lo = pltpu.bitcast(words << 16, F32)
        hi = pltpu.bitcast(words & jnp.uint32(0xFFFF0000), F32)
        xb = jnp.concatenate([lo, hi], axis=1).astype(BF16)
        gt = jnp.dot(xb, wgb[...], preferred_element_type=F32)
        up = jnp.dot(xb, wub[...], preferred_element_type=F32)
        hid = (gt * (1.0 / (1.0 + jnp.exp(-gt))) * up).astype(BF16)
        y_ref[0:rows, :] = jnp.dot(hid, wdb[...], preferred_element_type=F32)

    for rows in range(EXP_GRAIN, blk + 1, EXP_GRAIN):
        @pl.when(jnp.logical_and(nv > rows - EXP_GRAIN, nv <= rows))
        def _(rows=rows):
            mlp(rows)
            if rows < blk:
                y_ref[rows:, :] = jnp.zeros((blk - rows, y_ref.shape[1]), F32)

    @pl.when(nv == 0)
    def _():
        y_ref[...] = jnp.zeros(y_ref.shape, F32)


def _experts(block_e, nvalid, next_e, xs, w_gate, w_up, w_down):
    nb = block_e.shape[0]
    d, de = w_gate.shape[-2:]
    grid_spec = pltpu.PrefetchScalarGridSpec(
        num_scalar_prefetch=3,
        grid=(nb,),
        in_specs=[
            pl.BlockSpec((EXP_BLK, xs.shape[-1]), lambda i, be, nv, nx: (i, 0)),
            pl.BlockSpec(memory_space=pl.ANY),
            pl.BlockSpec(memory_space=pl.ANY),
            pl.BlockSpec(memory_space=pl.ANY),
        ],
        out_specs=pl.BlockSpec((EXP_BLK, d), lambda i, be, nv, nx: (i, 0)),
        scratch_shapes=[
            pltpu.VMEM((2, d, de), F32), pltpu.VMEM((2, d, de), F32), pltpu.VMEM((2, de, d), F32),
            pltpu.VMEM((d, de), BF16), pltpu.VMEM((d, de), BF16), pltpu.VMEM((de, d), BF16),
            pltpu.SMEM((1,), jnp.int32), pltpu.SemaphoreType.DMA((2,)),
        ],
    )
    return pl.pallas_call(
        _experts_kernel,
        grid_spec=grid_spec,
        out_shape=jax.ShapeDtypeStruct((nb * EXP_BLK, d), F32),
        compiler_params=pltpu.CompilerParams(
            dimension_semantics=("arbitrary",), vmem_limit_bytes=VMEM_LIMIT),
        name="experts",
    )(block_e, nvalid, next_e, xs, w_gate, w_up, w_down)


def _combine_kernel(dest_ref, dnext_ref, h_ref, gate_ref, y_hbm, g_ref, o_ref, ybuf, sem):
    tm = h_ref.shape[0]
    i = pl.program_id(0)
    slot = i % 2

    def start_rows(idx_ref, s):
        for r in range(tm):
            for k in range(TOP_K):
                pltpu.make_async_copy(y_hbm.at[idx_ref[0, k, r]], ybuf.at[s, k, r], sem.at[s]).start(priority=k)

    @pl.when(i == 0)
    def _():
        start_rows(dest_ref, 0)

    @pl.when(i + 1 < pl.num_programs(0))
    def _():
        start_rows(dnext_ref, 1 - slot)

    for k in range(TOP_K):
        pltpu.make_async_copy(y_hbm.at[pl.ds(0, tm)], ybuf.at[slot, k], sem.at[slot]).wait()
    gates = gate_ref[...]
    hcur = h_ref[...] + (gates[:, 0:1] * ybuf[slot, 0] + gates[:, 1:2] * ybuf[slot, 1])
    ms = jnp.mean(hcur * hcur, axis=-1, keepdims=True)
    o_ref[...] = hcur * lax.rsqrt(ms + EPS) * g_ref[...]


def _combine(dest, h2, gates, ys, gfin):
    t, d = h2.shape
    tm = TM_COMB
    nt = t // tm
    dest3 = dest.reshape(TOP_K, nt, tm).transpose(1, 0, 2)
    return pl.pallas_call(
        _combine_kernel,
        grid=(nt,),
        in_specs=[
            pl.BlockSpec((1, TOP_K, tm), lambda i: (i, 0, 0), memory_space=pltpu.SMEM),
            pl.BlockSpec((1, TOP_K, tm), lambda i: (jnp.minimum(i + 1, nt - 1), 0, 0),
                         memory_space=pltpu.SMEM),
            pl.BlockSpec((tm, d), lambda i: (i, 0)),
            pl.BlockSpec((tm, LANES), lambda i: (i, 0)),
            pl.BlockSpec(memory_space=pl.ANY),
            pl.BlockSpec(gfin.shape, lambda i: (0, 0)),
        ],
        out_specs=pl.BlockSpec((tm, d), lambda i: (i, 0)),
        out_shape=jax.ShapeDtypeStruct((t, d), F32),
        scratch_shapes=[pltpu.VMEM((2, TOP_K, tm, d), F32), pltpu.SemaphoreType.DMA((2,))],
        compiler_params=pltpu.CompilerParams(
            dimension_semantics=("arbitrary",), vmem_limit_bytes=VMEM_LIMIT),
        name="combine",
    )(dest3, dest3, h2, gates, ys, gfin)


def _route_plan(meta, counts, t):
    ids = meta[:TOP_K]
    ranks = meta[TOP_K:2 * TOP_K]
    cnt = counts[:N_EXPERTS, 0].astype(jnp.int32)
    padded = (cnt + EXP_BLK - 1) // EXP_BLK * EXP_BLK
    pad_ends = jnp.cumsum(padded)
    pad_starts = pad_ends - padded
    experts = jnp.arange(N_EXPERTS, dtype=jnp.int32)
    dest = ranks + jnp.sum(jnp.where(ids[:, None, :] == experts[None, :, None],
                                     pad_starts[None, :, None], 0), axis=1)
    nb = (t * TOP_K + EXP_BLK - 1) // EXP_BLK + N_EXPERTS
    bstart = jnp.arange(nb, dtype=jnp.int32) * EXP_BLK
    last_e = jnp.max(jnp.where(cnt > 0, experts, 0))
    block_e = jnp.minimum(jnp.sum((pad_ends[None, :] <= bstart[:, None]).astype(jnp.int32), axis=1), last_e)
    row_end = jnp.sum(jnp.where(block_e[:, None] == experts, pad_starts + cnt, 0), axis=-1)
    nvalid = jnp.clip(row_end - bstart, 0, EXP_BLK)
    later = (experts[None, :] > experts[:, None]) & (cnt[None, :] > 0)
    next_of = jnp.min(jnp.where(later, experts[None, :], N_EXPERTS), axis=1)
    next_of = jnp.where(next_of == N_EXPERTS, -1, next_of)
    next_e = jnp.sum(jnp.where(block_e[:, None] == experts, next_of, 0), axis=-1)
    return (dest.astype(jnp.int32), block_e.astype(jnp.int32), nvalid.astype(jnp.int32),
            next_e.astype(jnp.int32))


def kernel(x, norm1_g, w_in, lambda_q1, lambda_k1, lambda_q2, lambda_k2, subln_g, w_pool, pool_scale,
           w_out, norm2_g, w_group_router, b_group_router, w_expert_router, b_expert_router,
           w_gate, w_up, w_down, final_g):
    batch, seq, d = x.shape
    t = batch * seq
    qk_w = 2 * N_HEADS * HEAD_DIM
    av_w = N_HEADS * V_DIM
    l = 0
    assert seq % (2 * TK) == 0 and seq % TM_PROJ == 0 and TK == TM_PROJ
    assert t % TM_COMB == 0 and t % TM_DISP == 0
    assert seq % (ATTN_QSUB * TQ) == 0 and ATTN_UNROLL % 2 == 0

    wi = w_in[l]
    wqt = wi[:, :qk_w].T.astype(BF16)
    wk = wi[:, qk_w:2 * qk_w].astype(BF16)
    wv = wi[:, 2 * qk_w:2 * qk_w + av_w].reshape(d, N_HEADS, V_DIM)
    wvt = jnp.pad(wv, ((0, 0), (0, 0), (0, VT_ROWS - V_DIM))).reshape(d, N_HEADS * VT_ROWS).T.astype(BF16)
    wu = wi[:, 2 * qk_w + av_w:].astype(BF16)

    qt, k, vt, u = _in_proj(x.reshape(t, d), norm1_g[l].reshape(1, d), wqt, wk, wvt, wu,
                            batch=batch, seq=seq)

    lam_in = jnp.zeros((8, LANES), F32)
    lam_in = lam_in.at[0, :HEAD_DIM].set(lambda_q1[l]).at[1, :HEAD_DIM].set(lambda_k1[l])
    lam_in = lam_in.at[2, :HEAD_DIM].set(lambda_q2[l]).at[3, :HEAD_DIM].set(lambda_k2[l])
    gsub = jnp.broadcast_to(subln_g[l].reshape(V_DIM, 1), (V_DIM, TQ)).astype(F32)
    n_blocks = (t * TOP_K + EXP_BLK - 1) // EXP_BLK + N_EXPERTS
    a, xs_zero = _attention(lam_in, qt, k, vt, gsub, (n_blocks * EXP_BLK, d // 2),
                            batch=batch, seq=seq)

    n_router_rows = E_PER_GROUP + N_EXPERTS + E_PER_GROUP
    wr = jnp.zeros((n_router_rows, d), F32)
    wr = wr.at[:N_GROUPS].set(w_group_router[l].T)
    wr = wr.at[E_PER_GROUP:E_PER_GROUP + N_EXPERTS].set(w_expert_router[l].reshape(d, N_EXPERTS).T).astype(BF16)
    br = jnp.full((n_router_rows, 1), NEG, F32)
    br = br.at[:N_GROUPS, 0].set(b_group_router[l])
    br = br.at[E_PER_GROUP:E_PER_GROUP + N_EXPERTS, 0].set(b_expert_router[l].reshape(N_EXPERTS))
    h, xn2, gates, meta, counts = _mix_route(
        x, a, u.reshape(batch, seq, -1), w_pool[l].astype(BF16), pool_scale[l].reshape(1, -1),
        w_out[l].astype(BF16), norm2_g[l].reshape(1, d), wr, br, batch=batch, seq=seq)

    dest, block_e, nvalid, next_e = _route_plan(meta, counts, t)
    assert block_e.shape[0] == n_blocks
    xs = _dispatch(dest, xn2.reshape(t, d // 2), xs_zero)
    ys = _experts(block_e, nvalid, next_e, xs, w_gate[l], w_up[l], w_down[l])
    out = _combine(dest, h.reshape(t, d), gates.reshape(t, LANES), ys, final_g.reshape(1, d))
    return out.reshape(batch, seq, d)
```

```python
import functools
import math

import jax
import jax.numpy as jnp
import numpy as np
from jax import lax
from jax.experimental import pallas as pl
from jax.experimental.pallas import tpu as pltpu

F32 = jnp.float32
BF16 = jnp.bfloat16

N_HEADS = 4
HEAD_DIM = 64
V_DIM = 128
POOL_WINDOWS = (2, 4, 8, 16)
N_GROUPS = 4
E_PER_GROUP = 8
N_EXPERTS = N_GROUPS * E_PER_GROUP
TOP_K = 2
EPS = 1e-6
LAM_INIT = 0.8 - 0.6 * math.exp(-0.3 * 0)
NEG = -1e30
LOG2E = math.log2(math.e)


def _bf16_terms(value, n):
    terms, rest = [], value
    for _ in range(n):
        term = float(np.asarray(rest, dtype=jnp.bfloat16))
        terms.append(term)
        rest -= term
    return tuple(terms)


LOG2E_TERMS = _bf16_terms(LOG2E, 3)
N_TERMS = len(LOG2E_TERMS)
SLOPES = tuple(2.0 ** (-8.0 * (h + 1) / N_HEADS) for h in range(N_HEADS))

LANES = 128
QK_PAD = 128
VT_ROWS = 144
POS_SPLIT = 128

TM_PROJ = 512
TQ = 512
TK = 512
ATTN_UNROLL = 6
EXP_BLK = 512
EXP_GRAIN = 256
TM_DISP = 1024
TM_COMB = 512
VMEM_LIMIT = 56 * 1024 * 1024


def _nt_dot(a, b):
    return lax.dot_general(a, b, (((1,), (1,)), ((), ())), preferred_element_type=F32)


def _in_proj_kernel(x_ref, g_ref, wqt_ref, wk_ref, wvt_ref, wu_ref, kc_ref,
                    qt_ref, k_ref, vt_ref, u_ref, *, seq):
    tm = x_ref.shape[0]
    i = pl.program_id(0)
    j0 = (i % (seq // tm)) * tm
    x = x_ref[...]
    ms = jnp.mean(x * x, axis=-1, keepdims=True)
    hn = (x * lax.rsqrt(ms + EPS) * g_ref[...]).astype(BF16)

    qt = _nt_dot(wqt_ref[...], hn) * (LOG2E / math.sqrt(HEAD_DIM))
    zero_rows = jnp.zeros((HEAD_DIM, tm), BF16)
    for g in range(2 * N_HEADS):
        data = qt[g * HEAD_DIM:(g + 1) * HEAD_DIM].astype(BF16)
        lo_rows, hi_rows = (data, zero_rows) if g % 2 == 0 else (zero_rows, data)
        qt_ref[0, g, 0:HEAD_DIM, :] = lo_rows
        qt_ref[0, g, HEAD_DIM:QK_PAD, :] = hi_rows

    kk = jnp.dot(hn, wk_ref[...], preferred_element_type=F32)
    pos = j0 + lax.broadcasted_iota(jnp.int32, (tm, QK_PAD), 0)
    lane = lax.broadcasted_iota(jnp.int32, (tm, QK_PAD), 1)
    hi = (pos & ~(POS_SPLIT - 1)).astype(F32)
    lo = (pos & (POS_SPLIT - 1)).astype(F32)
    for c in range(2):
        f0 = _feature_base(c)
        f_hi, f_lo = f0 + 2 * N_TERMS, f0 + 3 * N_TERMS
        feat = jnp.where((lane >= f_hi) & (lane < f_lo), hi,
                         jnp.where((lane >= f_lo) & (lane < f_lo + N_TERMS), lo, 0.0))
        own = (lane < HEAD_DIM) if c == 0 else (lane >= HEAD_DIM)
        for h in range(N_HEADS):
            g = 2 * h + c
            dims = jnp.where(own, kk[:, h * QK_PAD:(h + 1) * QK_PAD], 0.0)
            k_ref[0, g] = (dims + (feat + kc_ref[g:g + 1, :])).astype(BF16)

    vt = _nt_dot(wvt_ref[...], hn)
    row = lax.broadcasted_iota(jnp.int32, (VT_ROWS, tm), 0)
    for h in range(N_HEADS):
        blk = vt[h * VT_ROWS:(h + 1) * VT_ROWS]
        vt_ref[0, h, 0] = jnp.where(row == V_DIM, 1.0, blk).astype(BF16)

    u_ref[...] = jnp.dot(hn, wu_ref[...], preferred_element_type=F32)


def _feature_base(c):
    return HEAD_DIM if c == 0 else 0


def _key_feature_consts():
    kc = np.zeros((2 * N_HEADS, QK_PAD), np.float32)
    for g in range(2 * N_HEADS):
        for rep in range(2):
            for n, term in enumerate(LOG2E_TERMS):
                kc[g, _feature_base(g % 2) + rep * N_TERMS + n] = term * SLOPES[g // 2]
    return jnp.asarray(kc)


def _in_proj(x2, g1, wqt, wk, wvt, wu, *, batch, seq):
    t, d = x2.shape
    tm = TM_PROJ
    nblk = seq // tm
    kern = functools.partial(_in_proj_kernel, seq=seq)
    full = lambda shape: pl.BlockSpec(shape, lambda i: (0,) * len(shape))
    kc = _key_feature_consts()
    return pl.pallas_call(
        kern,
        grid=(t // tm,),
        in_specs=[
            pl.BlockSpec((tm, d), lambda i: (i, 0)),
            full(g1.shape), full(wqt.shape), full(wk.shape), full(wvt.shape), full(wu.shape),
            full(kc.shape),
        ],
        out_specs=[
            pl.BlockSpec((1, 2 * N_HEADS, QK_PAD, tm), lambda i: (i // nblk, 0, 0, i % nblk)),
            pl.BlockSpec((1, 2 * N_HEADS, tm, QK_PAD), lambda i: (i // nblk, 0, i % nblk, 0)),
            pl.BlockSpec((1, N_HEADS, 1, VT_ROWS, tm), lambda i: (i // nblk, 0, i % nblk, 0, 0)),
            pl.BlockSpec((tm, wu.shape[1]), lambda i: (i, 0)),
        ],
        out_shape=[
            jax.ShapeDtypeStruct((batch, 2 * N_HEADS, QK_PAD, seq), BF16),
            jax.ShapeDtypeStruct((batch, 2 * N_HEADS, seq, QK_PAD), BF16),
            jax.ShapeDtypeStruct((batch, N_HEADS, nblk, VT_ROWS, tm), BF16),
            jax.ShapeDtypeStruct((t, wu.shape[1]), F32),
        ],
        compiler_params=pltpu.CompilerParams(
            dimension_semantics=("arbitrary",), vmem_limit_bytes=VMEM_LIMIT),
        name="in_proj",
    )(x2, g1, wqt, wk, wvt, wu, kc)


def _attention_kernel(lam_ref, qt_ref, k_ref, vt_ref, g_ref, o_ref, zero_hbm,
                      qv_ref, acc_ref, m_ref, sa_ref, sb_ref, mba_ref, mbb_ref, zbuf, zsem):
    step = ((pl.program_id(0) * pl.num_programs(1) + pl.program_id(1)) * pl.num_programs(2)
            + pl.program_id(2))
    zrows = zbuf.shape[0]

    @pl.when(step == 0)
    def _():
        zbuf[...] = jnp.zeros(zbuf.shape, zbuf.dtype)

    zero_copy = pltpu.make_async_copy(zbuf, zero_hbm.at[pl.ds(step * zrows, zrows)], zsem)
    zero_copy.start()

    _attention_block(pl.program_id(2), qt_ref, k_ref, vt_ref, qv_ref, acc_ref, m_ref,
                     sa_ref, sb_ref, mba_ref, mbb_ref)
    _attention_finish(lam_ref, g_ref, o_ref, acc_ref)
    zero_copy.wait()


def _attention_block(qi, qt_ref, k_ref, vt_ref, qv_ref, acc_ref, m_ref, sa_ref, sb_ref, mba_ref, mbb_ref):
    h = pl.program_id(1)
    tq = qt_ref.shape[-1]
    nkb, _, tk = vt_ref.shape[2:]
    jdiag = (qi * tq) // tk

    slope = jnp.where(h == 0, SLOPES[0], jnp.where(h == 1, SLOPES[1],
                      jnp.where(h == 2, SLOPES[2], SLOPES[3]))).astype(F32)

    r = lax.broadcasted_iota(jnp.int32, (QK_PAD, tq), 0)
    ipos = qi * tq + lax.broadcasted_iota(jnp.int32, (QK_PAD, tq), 1)
    ihi = (ipos & ~(POS_SPLIT - 1)).astype(F32)
    ilo = (ipos & (POS_SPLIT - 1)).astype(F32)
    for c in range(2):
        f0 = _feature_base(c)
        fi = jnp.where((r >= f0) & (r < f0 + N_TERMS), ihi,
                       jnp.where((r >= f0 + N_TERMS) & (r < f0 + 2 * N_TERMS), ilo, 0.0))
        fj = jnp.zeros((QK_PAD, tq), F32)
        for rep in range(2):
            for n, term in enumerate(LOG2E_TERMS):
                fj = jnp.where(r == f0 + (2 + rep) * N_TERMS + n, term * slope, fj)
        q = qt_ref[0, c].astype(F32)
        qv_ref[0, c] = (q + fi - fj).astype(BF16)
        qv_ref[1, c] = (q - fi + fj).astype(BF16)

    acc_ref[...] = jnp.zeros(acc_ref.shape, F32)
    m_ref[...] = jnp.full(m_ref.shape, NEG, F32)

    def score_stage(kb, variant, s_ref, mb_ref, bias=None):
        for c in range(2):
            kblk = k_ref[0, c, pl.ds(pl.multiple_of(kb * tk, tk), tk), :]
            qmat = qt_ref[0, c] if variant is None else qv_ref[variant, c]
            st = jnp.dot(kblk, qmat, preferred_element_type=F32)
            if bias is not None:
                st = st + bias
            s_ref[c] = st
            mb_ref[c] = jnp.max(st, axis=0, keepdims=True)

    def softmax_stage(kb, s_ref, mb_ref):
        for c in range(2):
            m_old = m_ref[c]
            m_new = jnp.maximum(m_old, mb_ref[c])
            alpha = jnp.exp2(m_old - m_new)
            p = jnp.exp2(s_ref[c] - m_new).astype(BF16)
            pv = jnp.dot(vt_ref[0, 0, kb], p, preferred_element_type=F32)
            acc_ref[c] = acc_ref[c] * alpha + pv
            m_ref[c] = m_new

    def visit_block(n):
        o = n - 1
        return jnp.where(n == 0, jdiag, o + (o >= jdiag).astype(jnp.int32))

    def visit_variant(kb):
        return jnp.where(kb < jdiag, 1, 0)

    jpos = jdiag * tk + lax.broadcasted_iota(jnp.int32, (tk, tq), 0)
    iposd = qi * tq + lax.broadcasted_iota(jnp.int32, (tk, tq), 1)
    bias = (-LOG2E * slope) * jnp.abs(iposd - jpos).astype(F32)
    score_stage(jdiag, None, sa_ref, mba_ref, bias)

    bufs = ((sa_ref, mba_ref), (sb_ref, mbb_ref))

    def phase(n, parity, with_scores):
        if with_scores:
            kb_next = visit_block(n + 1)
            score_stage(kb_next, visit_variant(kb_next), *bufs[1 - parity])
        softmax_stage(visit_block(n), *bufs[parity])

    def loop_body(t, carry):
        for j in range(ATTN_UNROLL):
            phase(t * ATTN_UNROLL + j, j % 2, True)
        return carry

    trips = (nkb - 1) // ATTN_UNROLL
    lax.fori_loop(0, trips, loop_body, 0)
    for n in range(trips * ATTN_UNROLL, nkb):
        phase(n, n % 2, n + 1 < nkb)


def _attention_finish(lam_ref, g_ref, o_ref, acc_ref):
    lv = lam_ref[...]
    lam = (jnp.exp(jnp.sum(lv[0:1] * lv[1:2], axis=-1, keepdims=True))
           - jnp.exp(jnp.sum(lv[2:3] * lv[3:4], axis=-1, keepdims=True)) + LAM_INIT)
    a0 = acc_ref[0]
    a1 = acc_ref[1]
    o = (a0[:V_DIM] * (1.0 / a0[V_DIM:V_DIM + 1])
         - lam * (a1[:V_DIM] * (1.0 / a1[V_DIM:V_DIM + 1])))
    ms = jnp.mean(o * o, axis=0, keepdims=True)
    y = o * lax.rsqrt(ms + EPS) * g_ref[...] * (1.0 - LAM_INIT)
    o_ref[0] = y.T.astype(BF16)


def _attention(lam_in, qt, k, vt, gsub, zero_shape, *, batch, seq):
    nkb = seq // TK
    steps = batch * N_HEADS * (seq // TQ)
    assert zero_shape[0] % steps == 0 and (zero_shape[0] // steps) % 8 == 0
    zrows = zero_shape[0] // steps
    return pl.pallas_call(
        _attention_kernel,
        grid=(batch, N_HEADS, seq // TQ),
        in_specs=[
            pl.BlockSpec(lam_in.shape, lambda b, h, i: (0, 0)),
            pl.BlockSpec((1, 2, QK_PAD, TQ), lambda b, h, i: (b, h, 0, i)),
            pl.BlockSpec((1, 2, seq, QK_PAD), lambda b, h, i: (b, h, 0, 0)),
            pl.BlockSpec((1, 1, nkb, VT_ROWS, TK), lambda b, h, i: (b, h, 0, 0, 0)),
            pl.BlockSpec(gsub.shape, lambda b, h, i: (0, 0)),
        ],
        out_specs=[pl.BlockSpec((1, TQ, V_DIM), lambda b, h, i: (b, i, h)),
                   pl.BlockSpec(memory_space=pl.ANY)],
        out_shape=[jax.ShapeDtypeStruct((batch, seq, N_HEADS * V_DIM), BF16),
                   jax.ShapeDtypeStruct(zero_shape, jnp.uint32)],
        scratch_shapes=[
            pltpu.VMEM((2, 2, QK_PAD, TQ), BF16),
            pltpu.VMEM((2, VT_ROWS, TQ), F32),
            pltpu.VMEM((2, 1, TQ), F32),
            pltpu.VMEM((2, TK, TQ), F32),
            pltpu.VMEM((2, TK, TQ), F32),
            pltpu.VMEM((2, 1, TQ), F32),
            pltpu.VMEM((2, 1, TQ), F32),
            pltpu.VMEM((zrows, zero_shape[1]), jnp.uint32),
            pltpu.SemaphoreType.DMA(()),
        ],
        compiler_params=pltpu.CompilerParams(
            dimension_semantics=("arbitrary", "arbitrary", "arbitrary"),
            vmem_limit_bytes=VMEM_LIMIT),
        name="diff_attention",
    )(lam_in, qt, k, vt, gsub)


def _mix_route_kernel(x_ref, a_ref, u_ref, up_ref, un_ref, wp_ref, ps_ref, wo_ref, g2_ref,
                      wr_ref, br_ref, h_ref, xn_ref, gate_ref, id_ref, count_ref, cnt_ref, triu_ref,
                      *, seq):
    tm = x_ref.shape[1]
    j0 = pl.program_id(1) * tm
    halo = up_ref.shape[1]

    @pl.when((pl.program_id(0) == 0) & (pl.program_id(1) == 0))
    def _():
        cnt_ref[...] = jnp.zeros(cnt_ref.shape, F32)
        rr = lax.broadcasted_iota(jnp.int32, (tm, tm), 0)
        cc = lax.broadcasted_iota(jnp.int32, (tm, tm), 1)
        triu_ref[...] = jnp.where(rr < cc, 1.0, 0.0).astype(BF16)

    n = tm + 2 * halo
    ext = jnp.concatenate([jnp.where(j0 > 0, up_ref[0], 0.0), u_ref[0],
                           jnp.where(j0 + tm < seq, un_ref[0], 0.0)], axis=0)
    edge = lax.broadcasted_iota(jnp.int32, (halo, LANES), 0)
    tpos_top = j0 + edge
    tpos_bot = j0 + (tm - halo) + edge
    ys = []
    for g, w in enumerate(POOL_WINDOWS):
        fwd = ext[:, g * LANES:(g + 1) * LANES]
        length = 1
        while 2 * length < w:
            fwd = fwd + pltpu.roll(fwd, n - length, 0)
            length *= 2
        win = (fwd + pltpu.roll(fwd, w // 2, 0))[halo:halo + tm]

        def clipped(tpos, w=w):
            return (jnp.minimum(tpos + w // 2, seq) - jnp.maximum(tpos - w // 2, 0)).astype(F32)

        mean = jnp.concatenate([win[:halo] / clipped(tpos_top), win[halo:tm - halo] * (1.0 / w),
                                win[tm - halo:] / clipped(tpos_bot)], axis=0)
        d = (mean - u_ref[0][:, g * LANES:(g + 1) * LANES]).astype(BF16)
        yg = jnp.dot(d, wp_ref[g], preferred_element_type=F32)
        ys.append(yg * ps_ref[:, g * LANES:(g + 1) * LANES])
    p = jnp.concatenate(ys, axis=-1).astype(BF16)

    mix = jnp.concatenate([a_ref[0], p], axis=-1)
    hcur = x_ref[0] + jnp.dot(mix, wo_ref[...], preferred_element_type=F32)
    h_ref[0] = hcur
    ms = jnp.mean(hcur * hcur, axis=-1, keepdims=True)
    xn = (hcur * lax.rsqrt(ms + EPS) * g2_ref[...]).astype(BF16)
    half = xn.shape[1] // 2
    lo = pltpu.bitcast(xn[:, :half].astype(F32), jnp.uint32) >> 16
    hi = pltpu.bitcast(xn[:, half:].astype(F32), jnp.uint32) & jnp.uint32(0xFFFF0000)
    xn_ref[0] = hi | lo

    logits = _nt_dot(wr_ref[...], xn) + br_ref[...]
    row8 = lax.broadcasted_iota(jnp.int32, (E_PER_GROUP, tm), 0)
    gl = logits[0:E_PER_GROUP]
    gmax = jnp.max(gl, axis=0, keepdims=True)
    gsel = jnp.min(jnp.where(gl == gmax, row8, E_PER_GROUP), axis=0, keepdims=True)
    pg = 1.0 / jnp.sum(jnp.exp(gl - gmax), axis=0, keepdims=True)
    elm = logits[E_PER_GROUP:2 * E_PER_GROUP]
    for g in range(1, N_GROUPS):
        elm = jnp.where(gsel == g, logits[(g + 1) * E_PER_GROUP:(g + 2) * E_PER_GROUP], elm)
    m1 = jnp.max(elm, axis=0, keepdims=True)
    j1 = jnp.min(jnp.where(elm == m1, row8, E_PER_GROUP), axis=0, keepdims=True)
    elm2 = jnp.where(row8 == j1, NEG, elm)
    m2 = jnp.max(elm2, axis=0, keepdims=True)
    j2 = jnp.min(jnp.where(elm2 == m2, row8, E_PER_GROUP), axis=0, keepdims=True)
    i1 = gsel * E_PER_GROUP + j1
    i2 = gsel * E_PER_GROUP + j2
    e2 = jnp.exp(m2 - m1)
    t1 = 1.0 / (1.0 + e2)
    rows = lax.broadcasted_iota(jnp.int32, (LANES, tm), 0)
    gates_t = jnp.where(rows == 0, pg * t1, jnp.where(rows == 1, pg * (e2 * t1), 0.0))
    gate_ref[0] = gates_t.T

    hit1 = rows == i1
    hit2 = rows == i2
    onehot = jnp.where(hit1 | hit2, 1.0, 0.0)
    before = jnp.dot(onehot.astype(BF16), triu_ref[...], preferred_element_type=F32) + cnt_ref[...]
    r1 = jnp.sum(jnp.where(hit1, before, 0.0), axis=0, keepdims=True).astype(jnp.int32)
    r2 = jnp.sum(jnp.where(hit2, before, 0.0), axis=0, keepdims=True).astype(jnp.int32)
    cnt_ref[...] = cnt_ref[...] + jnp.sum(onehot, axis=1, keepdims=True)
    count_ref[...] = cnt_ref[...]
    id_ref[...] = jnp.where(row8 == 0, i1, jnp.where(row8 == 1, i2,
                            jnp.where(row8 == 2, r1, jnp.where(row8 == 3, r2, 0))))


def _mix_route(x, a, u, wp, ps, wo, g2, wr, br, *, batch, seq):
    tm = TM_PROJ
    d = x.shape[-1]
    halo = 8
    nh = tm // halo
    kern = functools.partial(_mix_route_kernel, seq=seq)
    full = lambda arr: pl.BlockSpec(arr.shape, lambda b, i: (0,) * arr.ndim)
    tile = lambda w: pl.BlockSpec((1, tm, w), lambda b, i: (b, i, 0))
    return pl.pallas_call(
        kern,
        grid=(batch, seq // tm),
        in_specs=[
            tile(d), tile(a.shape[-1]), tile(u.shape[-1]),
            pl.BlockSpec((1, halo, u.shape[-1]), lambda b, i: (b, jnp.maximum(i * nh - 1, 0), 0)),
            pl.BlockSpec((1, halo, u.shape[-1]),
                         lambda b, i: (b, jnp.minimum((i + 1) * nh, seq // halo - 1), 0)),
            full(wp), full(ps), full(wo), full(g2), full(wr), full(br),
        ],
        out_specs=[tile(d), tile(d // 2), tile(LANES),
                   pl.BlockSpec((E_PER_GROUP, tm), lambda b, i: (0, b * (seq // tm) + i)),
                   pl.BlockSpec((LANES, 1), lambda b, i: (0, 0))],
        out_shape=[
            jax.ShapeDtypeStruct((batch, seq, d), F32),
            jax.ShapeDtypeStruct((batch, seq, d // 2), jnp.uint32),
            jax.ShapeDtypeStruct((batch, seq, LANES), F32),
            jax.ShapeDtypeStruct((E_PER_GROUP, batch * seq), jnp.int32),
            jax.ShapeDtypeStruct((LANES, 1), F32),
        ],
        scratch_shapes=[pltpu.VMEM((LANES, 1), F32), pltpu.VMEM((tm, tm), BF16)],
        compiler_params=pltpu.CompilerParams(
            dimension_semantics=("arbitrary", "arbitrary"), vmem_limit_bytes=VMEM_LIMIT),
        name="mix_route",
    )(x, a, u, u, u, wp, ps, wo, g2, wr, br)


def _dispatch_kernel(dest_ref, x_ref, xs_in, xs_hbm, sem):
    del xs_in
    tm = x_ref.shape[0]

    for r in range(tm):
        for k in range(TOP_K):
            pltpu.make_async_copy(x_ref.at[r], xs_hbm.at[dest_ref[0, k, r]], sem).start(priority=k)
    for k in range(TOP_K):
        pltpu.make_async_copy(x_ref, xs_hbm.at[pl.ds(0, tm)], sem).wait()


def _dispatch(dest, xn2, xs_zero):
    t, d = xn2.shape
    tm = TM_DISP
    nt = t // tm
    return pl.pallas_call(
        _dispatch_kernel,
        grid=(nt,),
        in_specs=[
            pl.BlockSpec((1, TOP_K, tm), lambda i: (i, 0, 0), memory_space=pltpu.SMEM),
            pl.BlockSpec((tm, d), lambda i: (i, 0)),
            pl.BlockSpec(memory_space=pl.ANY),
        ],
        out_specs=pl.BlockSpec(memory_space=pl.ANY),
        out_shape=jax.ShapeDtypeStruct(xs_zero.shape, xs_zero.dtype),
        input_output_aliases={2: 0},
        scratch_shapes=[pltpu.SemaphoreType.DMA(())],
        compiler_params=pltpu.CompilerParams(
            dimension_semantics=("arbitrary",), vmem_limit_bytes=VMEM_LIMIT),
        name="dispatch",
    )(dest.reshape(TOP_K, nt, tm).transpose(1, 0, 2), xn2, xs_zero)


def _experts_kernel(be_ref, nv_ref, nx_ref, xs_ref, wg_hbm, wu_hbm, wd_hbm, y_ref,
                    wgf, wuf, wdf, wgb, wub, wdb, slot_ref, sem):
    i = pl.program_id(0)
    changed = jnp.logical_or(i == 0, be_ref[i] != be_ref[jnp.maximum(i - 1, 0)])

    def weight_copies(e, slot):
        return (pltpu.make_async_copy(wg_hbm.at[e], wgf.at[slot], sem.at[slot]),
                pltpu.make_async_copy(wu_hbm.at[e], wuf.at[slot], sem.at[slot]),
                pltpu.make_async_copy(wd_hbm.at[e], wdf.at[slot], sem.at[slot]))

    @pl.when(i == 0)
    def _():
        slot_ref[0] = 0
        for cp in weight_copies(be_ref[0], 0):
            cp.start()

    @pl.when(changed)
    def _():
        slot = jnp.where(i == 0, 0, 1 - slot_ref[0])
        slot_ref[0] = slot
        for cp in weight_copies(be_ref[i], slot):
            cp.wait()
        wgb[...] = wgf[slot].astype(BF16)
        wub[...] = wuf[slot].astype(BF16)
        wdb[...] = wdf[slot].astype(BF16)

        @pl.when(nx_ref[i] >= 0)
        def _():
            for cp in weight_copies(nx_ref[i], 1 - slot):
                cp.start()

    blk = xs_ref.shape[0]
    nv = nv_ref[i]

    def mlp(rows):
        words = xs_ref[0:rows, :]
        lo = pltpu.bitcast(words << 16, F32)
        hi = pltpu.bitcast(words & jnp.uint32(0xFFFF0000), F32)
        xb = jnp.concatenate([lo, hi], axis=1).astype(BF16)
        gt = jnp.dot(xb, wgb[...], preferred_element_type=F32)
        up = jnp.dot(xb, wub[...], preferred_element_type=F32)
        hid = (gt * (1.0 / (1.0 + jnp.exp(-gt))) * up).astype(BF16)
        y_ref[0:rows, :] = jnp.dot(hid, wdb[...], preferred_element_type=F32)

    for rows in range(EXP_GRAIN, blk + 1, EXP_GRAIN):
        @pl.when(jnp.logical_and(nv > rows - EXP_GRAIN, nv <= rows))
        def _(rows=rows):
            mlp(rows)
            if rows < blk:
                y_ref[rows:, :] = jnp.zeros((blk - rows, y_ref.shape[1]), F32)

    @pl.when(nv == 0)
    def _():
        y_ref[...] = jnp.zeros(y_ref.shape, F32)


def _experts(block_e, nvalid, next_e, xs, w_gate, w_up, w_down):
    nb = block_e.shape[0]
    d, de = w_gate.shape[-2:]
    grid_spec = pltpu.PrefetchScalarGridSpec(
        num_scalar_prefetch=3,
        grid=(nb,),
        in_specs=[
            pl.BlockSpec((EXP_BLK, xs.shape[-1]), lambda i, be, nv, nx: (i, 0)),
            pl.BlockSpec(memory_space=pl.ANY),
            pl.BlockSpec(memory_space=pl.ANY),
            pl.BlockSpec(memory_space=pl.ANY),
        ],
        out_specs=pl.BlockSpec((EXP_BLK, d), lambda i, be, nv, nx: (i, 0)),
        scratch_shapes=[
            pltpu.VMEM((2, d, de), F32), pltpu.VMEM((2, d, de), F32), pltpu.VMEM((2, de, d), F32),
            pltpu.VMEM((d, de), BF16), pltpu.VMEM((d, de), BF16), pltpu.VMEM((de, d), BF16),
            pltpu.SMEM((1,), jnp.int32), pltpu.SemaphoreType.DMA((2,)),
        ],
    )
    return pl.pallas_call(
        _experts_kernel,
        grid_spec=grid_spec,
        out_shape=jax.ShapeDtypeStruct((nb * EXP_BLK, d), F32),
        compiler_params=pltpu.CompilerParams(
            dimension_semantics=("arbitrary",), vmem_limit_bytes=VMEM_LIMIT),
        name="experts",
    )(block_e, nvalid, next_e, xs, w_gate, w_up, w_down)


def _combine_kernel(dest_ref, dnext_ref, h_ref, gate_ref, y_hbm, g_ref, o_ref, ybuf, sem):
    tm = h_ref.shape[0]
    i = pl.program_id(0)
    slot = i % 2

    def start_rows(idx_ref, s):
        for r in range(tm):
            for k in range(TOP_K):
                pltpu.make_async_copy(y_hbm.at[idx_ref[0, k, r]], ybuf.at[s, k, r], sem.at[s]).start(priority=k)

    @pl.when(i == 0)
    def _():
        start_rows(dest_ref, 0)

    @pl.when(i + 1 < pl.num_programs(0))
    def _():
        start_rows(dnext_ref, 1 - slot)

    for k in range(TOP_K):
        pltpu.make_async_copy(y_hbm.at[pl.ds(0, tm)], ybuf.at[slot, k], sem.at[slot]).wait()
    gates = gate_ref[...]
    hcur = h_ref[...] + (gates[:, 0:1] * ybuf[slot, 0] + gates[:, 1:2] * ybuf[slot, 1])
    ms = jnp.mean(hcur * hcur, axis=-1, keepdims=True)
    o_ref[...] = hcur * lax.rsqrt(ms + EPS) * g_ref[...]


def _combine(dest, h2, gates, ys, gfin):
    t, d = h2.shape
    tm = TM_COMB
    nt = t // tm
    dest3 = dest.reshape(TOP_K, nt, tm).transpose(1, 0, 2)
    return pl.pallas_call(
        _combine_kernel,
        grid=(nt,),
        in_specs=[
            pl.BlockSpec((1, TOP_K, tm), lambda i: (i, 0, 0), memory_space=pltpu.SMEM),
            pl.BlockSpec((1, TOP_K, tm), lambda i: (jnp.minimum(i + 1, nt - 1), 0, 0),
                         memory_space=pltpu.SMEM),
            pl.BlockSpec((tm, d), lambda i: (i, 0)),
            pl.BlockSpec((tm, LANES), lambda i: (i, 0)),
            pl.BlockSpec(memory_space=pl.ANY),
            pl.BlockSpec(gfin.shape, lambda i: (0, 0)),
        ],
        out_specs=pl.BlockSpec((tm, d), lambda i: (i, 0)),
        out_shape=jax.ShapeDtypeStruct((t, d), F32),
        scratch_shapes=[pltpu.VMEM((2, TOP_K, tm, d), F32), pltpu.SemaphoreType.DMA((2,))],
        compiler_params=pltpu.CompilerParams(
            dimension_semantics=("arbitrary",), vmem_limit_bytes=VMEM_LIMIT),
        name="combine",
    )(dest3, dest3, h2, gates, ys, gfin)


def _route_plan(meta, counts, t):
    ids = meta[:TOP_K]
    ranks = meta[TOP_K:2 * TOP_K]
    cnt = counts[:N_EXPERTS, 0].astype(jnp.int32)
    padded = (cnt + EXP_BLK - 1) // EXP_BLK * EXP_BLK
    pad_ends = jnp.cumsum(padded)
    pad_starts = pad_ends - padded
    experts = jnp.arange(N_EXPERTS, dtype=jnp.int32)
    dest = ranks + jnp.sum(jnp.where(ids[:, None, :] == experts[None, :, None],
                                     pad_starts[None, :, None], 0), axis=1)
    nb = (t * TOP_K + EXP_BLK - 1) // EXP_BLK + N_EXPERTS
    bstart = jnp.arange(nb, dtype=jnp.int32) * EXP_BLK
    last_e = jnp.max(jnp.where(cnt > 0, experts, 0))
    block_e = jnp.minimum(jnp.sum((pad_ends[None, :] <= bstart[:, None]).astype(jnp.int32), axis=1), last_e)
    row_end = jnp.sum(jnp.where(block_e[:, None] == experts, pad_starts + cnt, 0), axis=-1)
    nvalid = jnp.clip(row_end - bstart, 0, EXP_BLK)
    later = (experts[None, :] > experts[:, None]) & (cnt[None, :] > 0)
    next_of = jnp.min(jnp.where(later, experts[None, :], N_EXPERTS), axis=1)
    next_of = jnp.where(next_of == N_EXPERTS, -1, next_of)
    next_e = jnp.sum(jnp.where(block_e[:, None] == experts, next_of, 0), axis=-1)
    return (dest.astype(jnp.int32), block_e.astype(jnp.int32), nvalid.astype(jnp.int32),
            next_e.astype(jnp.int32))


def kernel(x, norm1_g, w_in, lambda_q1, lambda_k1, lambda_q2, lambda_k2, subln_g, w_pool, pool_scale,
           w_out, norm2_g, w_group_router, b_group_router, w_expert_router, b_expert_router,
           w_gate, w_up, w_down, final_g):
    batch, seq, d = x.shape
    t = batch * seq
    qk_w = 2 * N_HEADS * HEAD_DIM
    av_w = N_HEADS * V_DIM
    l = 0
    assert seq % (2 * TK) == 0 and seq % TM_PROJ == 0 and TK == TM_PROJ
    assert t % TM_COMB == 0 and t % TM_DISP == 0
    assert seq % TQ == 0 and ATTN_UNROLL % 2 == 0

    wi = w_in[l]
    wqt = wi[:, :qk_w].T.astype(BF16)
    wk = wi[:, qk_w:2 * qk_w].astype(BF16)
    wv = wi[:, 2 * qk_w:2 * qk_w + av_w].reshape(d, N_HEADS, V_DIM)
    wvt = jnp.pad(wv, ((0, 0), (0, 0), (0, VT_ROWS - V_DIM))).reshape(d, N_HEADS * VT_ROWS).T.astype(BF16)
    wu = wi[:, 2 * qk_w + av_w:].astype(BF16)

    qt, k, vt, u = _in_proj(x.reshape(t, d), norm1_g[l].reshape(1, d), wqt, wk, wvt, wu,
                            batch=batch, seq=seq)

    lam_in = jnp.zeros((8, LANES), F32)
    lam_in = lam_in.at[0, :HEAD_DIM].set(lambda_q1[l]).at[1, :HEAD_DIM].set(lambda_k1[l])
    lam_in = lam_in.at[2, :HEAD_DIM].set(lambda_q2[l]).at[3, :HEAD_DIM].set(lambda_k2[l])
    gsub = jnp.broadcast_to(subln_g[l].reshape(V_DIM, 1), (V_DIM, TQ)).astype(F32)
    n_blocks = (t * TOP_K + EXP_BLK - 1) // EXP_BLK + N_EXPERTS
    a, xs_zero = _attention(lam_in, qt, k, vt, gsub, (n_blocks * EXP_BLK, d // 2),
                            batch=batch, seq=seq)

    n_router_rows = E_PER_GROUP + N_EXPERTS + E_PER_GROUP
    wr = jnp.zeros((n_router_rows, d), F32)
    wr = wr.at[:N_GROUPS].set(w_group_router[l].T)
    wr = wr.at[E_PER_GROUP:E_PER_GROUP + N_EXPERTS].set(w_expert_router[l].reshape(d, N_EXPERTS).T).astype(BF16)
    br = jnp.full((n_router_rows, 1), NEG, F32)
    br = br.at[:N_GROUPS, 0].set(b_group_router[l])
    br = br.at[E_PER_GROUP:E_PER_GROUP + N_EXPERTS, 0].set(b_expert_router[l].reshape(N_EXPERTS))
    h, xn2, gates, meta, counts = _mix_route(
        x, a, u.reshape(batch, seq, -1), w_pool[l].astype(BF16), pool_scale[l].reshape(1, -1),
        w_out[l].astype(BF16), norm2_g[l].reshape(1, d), wr, br, batch=batch, seq=seq)

    dest, block_e, nvalid, next_e = _route_plan(meta, counts, t)
    assert block_e.shape[0] == n_blocks
    xs = _dispatch(dest, xn2.reshape(t, d // 2), xs_zero)
    ys = _experts(block_e, nvalid, next_e, xs, w_gate[l], w_up[l], w_down[l])
    out = _combine(dest, h.reshape(t, d), gates.reshape(t, LANES), ys, final_g.reshape(1, d))
    return out.reshape(batch, seq, d)
```
